```python
import jax, jax.numpy as jnp
from jax import lax
import numpy as np

D_MODEL = 1024
BATCH = 8
SEQ = 2048
DEPTH = 4
DEC_BATCH = 16
DEC_SEQ = 64
PAST_LEN = 4096

CHUNK = 64
Q_BLOCK = 128
A_HEADS = 8
NOPE_DIM = 64
ROPE_DIM = 32
V_DIM = 64
Q_LORA = 384
KV_LORA = 256
ROPE_THETA = 10000.0
SM_SCALE = (NOPE_DIM + ROPE_DIM) ** -0.5
R_HEADS = 8
R_HEAD_DIM = 64
R_WIDTH = R_HEADS * R_HEAD_DIM
DECAY_LORA = 64
AAA_LORA = 64
GATE_LORA = 128
RW_COLS = 3 * R_WIDTH + DECAY_LORA + AAA_LORA + GATE_LORA
MLA_COLS = Q_LORA + KV_LORA + ROPE_DIM
IN_COLS = MLA_COLS + RW_COLS
D_MIX = A_HEADS * V_DIM + R_WIDTH
D_FF = -(-8 * D_MODEL // (3 * 256)) * 256
RMS_EPS = 1e-6
LNX_EPS = 64e-5

kernel_name = 'hybrid_mla_rwkv7_stream_step'


def rms_norm(x, g):
    xf = x.astype(jnp.float32)
    y = xf * lax.rsqrt(jnp.mean(xf * xf, axis=-1, keepdims=True) + RMS_EPS)
    return (y * g.astype(jnp.float32)).astype(x.dtype)


def rope_tables(pos):
    inv = ROPE_THETA ** (-jnp.arange(0, ROPE_DIM, 2, dtype=jnp.float32) / ROPE_DIM)
    ang = pos.astype(jnp.float32)[:, None] * inv[None, :]
    return jnp.cos(ang), jnp.sin(ang)


def apply_rope(x, cos, sin):
    xf = x.astype(jnp.float32)
    x1, x2 = jnp.split(xf, 2, axis=-1)
    return jnp.concatenate([x1 * cos - x2 * sin, x1 * sin + x2 * cos], axis=-1).astype(x.dtype)


def mla_attend(q_lat, q_rope, ckv, krope, mask):
    s = (jnp.einsum('bqhc,bkc->bhqk', q_lat, ckv, preferred_element_type=jnp.float32)
         + jnp.einsum('bqhr,bkr->bhqk', q_rope, krope, preferred_element_type=jnp.float32)) * SM_SCALE
    if mask is not None:
        s = jnp.where(mask, s, -jnp.inf)
    p = jax.nn.softmax(s, axis=-1).astype(ckv.dtype)
    return jnp.einsum('bhqk,bkc->bqhc', p, ckv)


def prompt_attention(q_lat, q_rope, ckv, krope):
    B, T = q_lat.shape[:2]
    n_blocks = T // Q_BLOCK
    k_chunk = jnp.arange(T) // CHUNK

    def one_block(i):
        q0 = i * Q_BLOCK
        ql = lax.dynamic_slice_in_dim(q_lat, q0, Q_BLOCK, axis=1)
        qr = lax.dynamic_slice_in_dim(q_rope, q0, Q_BLOCK, axis=1)
        q_chunk = (q0 + jnp.arange(Q_BLOCK)) // CHUNK
        mask = k_chunk[None, :] <= q_chunk[:, None]
        return mla_attend(ql, qr, ckv, krope, mask)

    out = lax.map(one_block, jnp.arange(n_blocks))
    return jnp.moveaxis(out, 0, 1).reshape(B, T, A_HEADS, KV_LORA)


def wkv_scan(S0, r, decay, k, v, kk, a):
    def step(S, inp):
        r_t, w_t, k_t, v_t, kk_t, a_t = inp
        sa = jnp.einsum('bhvk,bhk->bhv', S, -kk_t)
        S = (S * w_t[:, :, None, :] + sa[..., None] * (kk_t * a_t)[:, :, None, :]
             + v_t[..., None] * k_t[:, :, None, :])
        return S, jnp.einsum('bhvk,bhk->bhv', S, r_t)
    xs = tuple(jnp.moveaxis(t.astype(jnp.float32), 1, 0) for t in (r, decay, k, v, kk, a))
    S, ys = lax.scan(step, S0.astype(jnp.float32), xs)
    return S.astype(S0.dtype), jnp.moveaxis(ys, 0, 1)


def head_group_norm(y, g, b):
    mu = jnp.mean(y, axis=-1, keepdims=True)
    var = jnp.mean(jnp.square(y - mu), axis=-1, keepdims=True)
    yn = (y - mu) * lax.rsqrt(var + LNX_EPS)
    B, T = y.shape[:2]
    return yn.reshape(B, T, R_WIDTH) * g.astype(jnp.float32) + b.astype(jnp.float32)


def mixer(h, pos, ckv_past, krope_past, wkv0, shift0, lw):
    B, T, _ = h.shape
    proj = h @ lw['w_in']
    c_q, c_kv, k_r, rw = jnp.split(proj, [Q_LORA, Q_LORA + KV_LORA, MLA_COLS], axis=-1)
    cos, sin = rope_tables(pos)
    q = (rms_norm(c_q, lw['q_norm_g']) @ lw['w_uq']).reshape(B, T, A_HEADS, NOPE_DIM + ROPE_DIM)
    q_nope, q_rope = jnp.split(q, [NOPE_DIM], axis=-1)
    q_rope = apply_rope(q_rope, cos[:, None, :], sin[:, None, :])
    ckv_new = rms_norm(c_kv, lw['kv_norm_g'])
    krope_new = apply_rope(k_r, cos, sin)
    q_lat = jnp.einsum('bqhd,chd->bqhc', q_nope, lw['w_uk'])
    if ckv_past is None:
        o_lat = prompt_attention(q_lat, q_rope, ckv_new, krope_new)
    else:
        ckv_all = jnp.concatenate([ckv_past, ckv_new], axis=1)
        krope_all = jnp.concatenate([krope_past, krope_new], axis=1)
        o_lat = mla_attend(q_lat, q_rope, ckv_all, krope_all, None)
    o_attn = jnp.einsum('bqhc,chd->bqhd', o_lat, lw['w_uv']).reshape(B, T, A_HEADS * V_DIM)
    rw_prev = jnp.concatenate([shift0[:, None, :].astype(rw.dtype), rw[:, :-1]], axis=1)
    rw_mix = rw + (rw_prev - rw) * lw['rw_mu']
    shift_new = rw[:, -1]
    r, k, v, w_lo, a_lo, g_lo = jnp.split(
        rw_mix, [R_WIDTH, 2 * R_WIDTH, 3 * R_WIDTH, 3 * R_WIDTH + DECAY_LORA,
                 3 * R_WIDTH + DECAY_LORA + AAA_LORA], axis=-1)
    w_raw = (lw['rw_w0'] + jnp.tanh(w_lo) @ lw['rw_w2']).astype(jnp.float32)
    decay = jnp.exp(-jnp.exp(-jax.nn.softplus(-w_raw) - 0.5))
    a = jax.nn.sigmoid(lw['rw_a0'] + a_lo @ lw['rw_a2'])
    g = jax.nn.sigmoid(g_lo) @ lw['rw_g2']
    heads = lambda t: t.reshape(B, T, R_HEADS, R_HEAD_DIM)
    kk = heads(k * lw['rw_kk']).astype(jnp.float32)
    kk = kk / jnp.maximum(jnp.sqrt(jnp.sum(kk * kk, axis=-1, keepdims=True)), 1e-12)
    k = k * (1 + (a - 1) * lw['rw_ka'])
    S_new, y = wkv_scan(wkv0, heads(r), heads(decay), heads(k), heads(v), kk, heads(a))
    y = head_group_norm(y, lw['lnx_g'], lw['lnx_b']).astype(h.dtype)
    bonus = jnp.sum(heads(r) * heads(k) * lw['rw_rk'], axis=-1, keepdims=True) * heads(v)
    o_rw = (y + bonus.reshape(B, T, R_WIDTH)) * g
    out = jnp.concatenate([o_attn, o_rw], axis=-1) @ lw['w_out']
    return out, ckv_new, krope_new, S_new, shift_new


def trunk_layer(x, mod, pos, ckv_past, krope_past, wkv0, shift0, lw):
    sh_a, sc_a, gt_a, sh_f, sc_f, gt_f = jnp.split(mod[:, None, :], 6, axis=-1)
    h = rms_norm(x, lw['norm_g'][0]) * (1 + sc_a) + sh_a
    m, ckv_new, krope_new, S_new, shift_new = mixer(h, pos, ckv_past, krope_past, wkv0, shift0, lw)
    x = x + gt_a * rms_norm(m, lw['norm_g'][1])
    h = rms_norm(x, lw['norm_g'][2]) * (1 + sc_f) + sh_f
    f = (jax.nn.silu(h @ lw['w_gate']) * (h @ lw['w_up'])) @ lw['w_down']
    x = x + gt_f * rms_norm(f, lw['norm_g'][3])
    return x, ckv_new, krope_new, S_new, shift_new


def setup_inputs(seed: int = 0) -> dict:
    key = jax.random.key(seed)
    ks = jax.random.split(key, 32)

    def nrm(i, shape, scale=1.0, shift=0.0):
        return shift + scale * jax.random.normal(ks[i], shape, jnp.float32)

    return {
        'x_prompt': nrm(0, (BATCH, SEQ, D_MODEL)),
        'x_sample': nrm(1, (DEC_BATCH, DEC_SEQ, D_MODEL)),
        'c_prompt': nrm(2, (BATCH, D_MODEL)),
        'c_sample': nrm(3, (DEC_BATCH, D_MODEL)),
        'cache_ckv': nrm(4, (DEPTH, DEC_BATCH, PAST_LEN, KV_LORA)),
        'cache_krope': nrm(5, (DEPTH, DEC_BATCH, PAST_LEN, ROPE_DIM)),
        'state_wkv': nrm(6, (DEPTH, DEC_BATCH, R_HEADS, R_HEAD_DIM, R_HEAD_DIM), 0.5),
        'state_shift': nrm(7, (DEPTH, DEC_BATCH, RW_COLS)),
        'ada_w': nrm(8, (DEPTH, D_MODEL, 6 * D_MODEL), 0.5 * D_MODEL ** -0.5),
        'ada_b': nrm(9, (DEPTH, 6 * D_MODEL), 0.01),
        'norm_g': nrm(10, (DEPTH, 4, D_MODEL), 0.05, 1.0),
        'w_in': nrm(11, (DEPTH, D_MODEL, IN_COLS), D_MODEL ** -0.5),
        'q_norm_g': nrm(12, (DEPTH, Q_LORA), 0.05, 1.0),
        'kv_norm_g': nrm(13, (DEPTH, KV_LORA), 0.05, 1.0),
        'w_uq': nrm(14, (DEPTH, Q_LORA, A_HEADS * (NOPE_DIM + ROPE_DIM)), Q_LORA ** -0.5),
        'w_uk': nrm(15, (DEPTH, KV_LORA, A_HEADS, NOPE_DIM), KV_LORA ** -0.5),
        'w_uv': nrm(16, (DEPTH, KV_LORA, A_HEADS, V_DIM), KV_LORA ** -0.5),
        'rw_mu': jax.random.uniform(ks[17], (DEPTH, RW_COLS), jnp.float32),
        'rw_w0': nrm(18, (DEPTH, R_WIDTH), 0.5, -1.0),
        'rw_w2': nrm(19, (DEPTH, DECAY_LORA, R_WIDTH), DECAY_LORA ** -0.5),
        'rw_a0': nrm(20, (DEPTH, R_WIDTH), 0.5),
        'rw_a2': nrm(21, (DEPTH, AAA_LORA, R_WIDTH), AAA_LORA ** -0.5),
        'rw_g2': nrm(22, (DEPTH, GATE_LORA, R_WIDTH), GATE_LORA ** -0.5),
        'rw_kk': nrm(23, (DEPTH, R_WIDTH), 0.05, 0.85),
        'rw_ka': nrm(24, (DEPTH, R_WIDTH), 0.05, 1.0),
        'rw_rk': nrm(25, (DEPTH, R_HEADS, R_HEAD_DIM), 0.1),
        'lnx_g': nrm(26, (DEPTH, R_WIDTH), 0.05, 1.0),
        'lnx_b': nrm(27, (DEPTH, R_WIDTH), 0.01),
        'w_out': nrm(28, (DEPTH, D_MIX, D_MODEL), D_MIX ** -0.5),
        'w_gate': nrm(29, (DEPTH, D_MODEL, D_FF), D_MODEL ** -0.5),
        'w_up': nrm(30, (DEPTH, D_MODEL, D_FF), D_MODEL ** -0.5),
        'w_down': nrm(31, (DEPTH, D_FF, D_MODEL), D_FF ** -0.5),
    }


def reference(x_prompt, x_sample, c_prompt, c_sample, cache_ckv, cache_krope, state_wkv, state_shift,
              ada_w, ada_b, norm_g, w_in, q_norm_g, kv_norm_g, w_uq, w_uk, w_uv,
              rw_mu, rw_w0, rw_w2, rw_a0, rw_a2, rw_g2, rw_kk, rw_ka, rw_rk, lnx_g, lnx_b,
              w_out, w_gate, w_up, w_down):
    xp, xs = x_prompt, x_sample
    Bp, Tp = xp.shape[0], xp.shape[1]
    Ts = xs.shape[1]
    past = cache_ckv.shape[2]
    pos_p = jnp.arange(Tp)
    pos_s = past + jnp.arange(Ts)
    wkv_zero = jnp.zeros((Bp, R_HEADS, R_HEAD_DIM, R_HEAD_DIM), xp.dtype)
    shift_zero = jnp.zeros((Bp, RW_COLS), xp.dtype)
    ckv_p, kr_p, wkv_p, sh_p = [], [], [], []
    ckv_s, kr_s, wkv_s, sh_s = [], [], [], []
    for l in range(DEPTH):
        lw = {'norm_g': norm_g[l], 'w_in': w_in[l], 'q_norm_g': q_norm_g[l], 'kv_norm_g': kv_norm_g[l],
              'w_uq': w_uq[l], 'w_uk': w_uk[l], 'w_uv': w_uv[l], 'rw_mu': rw_mu[l], 'rw_w0': rw_w0[l],
              'rw_w2': rw_w2[l], 'rw_a0': rw_a0[l], 'rw_a2': rw_a2[l], 'rw_g2': rw_g2[l],
              'rw_kk': rw_kk[l], 'rw_ka': rw_ka[l], 'rw_rk': rw_rk[l], 'lnx_g': lnx_g[l],
              'lnx_b': lnx_b[l], 'w_out': w_out[l], 'w_gate': w_gate[l], 'w_up': w_up[l],
              'w_down': w_down[l]}
        mod_p = jax.nn.silu(c_prompt) @ ada_w[l] + ada_b[l]
        mod_s = jax.nn.silu(c_sample) @ ada_w[l] + ada_b[l]
        xp, c1, k1, s1, h1 = trunk_layer(xp, mod_p, pos_p, None, None, wkv_zero, shift_zero, lw)
        xs, c2, k2, s2, h2 = trunk_layer(xs, mod_s, pos_s, cache_ckv[l], cache_krope[l],
                                         state_wkv[l], state_shift[l], lw)
        ckv_p.append(c1); kr_p.append(k1); wkv_p.append(s1); sh_p.append(h1)
        ckv_s.append(c2); kr_s.append(k2); wkv_s.append(s2); sh_s.append(h2)
    ckv_prompt = jnp.stack(ckv_p)
    krope_prompt = jnp.stack(kr_p)
    wkv_prompt = jnp.stack(wkv_p)
    shift_prompt = jnp.stack(sh_p)
    ckv_sample = jnp.stack(ckv_s)
    krope_sample = jnp.stack(kr_s)
    wkv_sample = jnp.stack(wkv_s)
    shift_sample = jnp.stack(sh_s)
    return (xp, xs, ckv_prompt, krope_prompt, wkv_prompt, shift_prompt,
            ckv_sample, krope_sample, wkv_sample, shift_sample)
```

```python
import functools

import jax
import jax.numpy as jnp
from jax import lax
from jax.experimental import pallas as pl
from jax.experimental.pallas import tpu as pltpu

D_MODEL = 1024
CHUNK = 64
A_HEADS = 8
NOPE_DIM = 64
ROPE_DIM = 32
V_DIM = 64
Q_LORA = 384
KV_LORA = 256
ROPE_THETA = 10000.0
SM_SCALE = (NOPE_DIM + ROPE_DIM) ** -0.5
R_HEADS = 8
R_HEAD_DIM = 64
R_WIDTH = R_HEADS * R_HEAD_DIM
DECAY_LORA = 64
AAA_LORA = 64
GATE_LORA = 128
RW_COLS = 3 * R_WIDTH + DECAY_LORA + AAA_LORA + GATE_LORA
MLA_COLS = Q_LORA + KV_LORA + ROPE_DIM
D_MIX = A_HEADS * V_DIM + R_WIDTH
D_FF = 2816
RMS_EPS = 1e-6
LNX_EPS = 64e-5

LANE = 128
QK_WIDTH = KV_LORA + LANE
RW_OFF = Q_LORA + KV_LORA + LANE
IN_PACKED = RW_OFF + RW_COLS
VMEM_LIMIT = 56 * 1024 * 1024

BF16 = jnp.bfloat16
F32 = jnp.float32


def _cparams(sem):
    return pltpu.CompilerParams(dimension_semantics=sem, vmem_limit_bytes=VMEM_LIMIT)


def _const_spec(shape):
    n = len(shape)
    return pl.BlockSpec(shape, lambda *_: (0,) * n)


def _bdot(a, b):
    return jnp.dot(a.astype(BF16), b.astype(BF16), preferred_element_type=F32)


def _split_dot(a, b):
    hi = a.astype(BF16)
    lo = (a - hi.astype(F32)).astype(BF16)
    return (jnp.dot(hi, b, preferred_element_type=F32)
            + jnp.dot(lo, b, preferred_element_type=F32))


def _rms(x, g):
    return x * lax.rsqrt(jnp.mean(x * x, axis=-1, keepdims=True) + RMS_EPS) * g


def _sigmoid(x):
    return 1.0 / (1.0 + jnp.exp(-x))


def _mod_kernel(c_ref, w_ref, b_ref, o_ref):
    c = c_ref[...]
    s = c * _sigmoid(c)
    o_ref[0] = _bdot(s, w_ref[0]) + b_ref[0]


def _modulation(c_all, ada_w, ada_b):
    depth = ada_w.shape[0]
    n = c_all.shape[0]
    tn = 1024
    return pl.pallas_call(
        _mod_kernel,
        grid=(depth, 6 * D_MODEL // tn),
        in_specs=[
            pl.BlockSpec((n, D_MODEL), lambda l, j: (0, 0)),
            pl.BlockSpec((1, D_MODEL, tn), lambda l, j: (l, 0, j)),
            pl.BlockSpec((1, 1, tn), lambda l, j: (l, 0, j)),
        ],
        out_specs=pl.BlockSpec((1, n, tn), lambda l, j: (l, 0, j)),
        out_shape=jax.ShapeDtypeStruct((depth, n, 6 * D_MODEL), F32),
        compiler_params=_cparams(("parallel", "parallel")),
        name="adaln_mod",
    )(c_all, ada_w, ada_b.reshape(depth, 1, 6 * D_MODEL))


def _in_proj_kernel(x_ref, mod_ref, g0_ref, win_ref, qg_ref, kvg_ref, wuq_ref, wuk_ref,
                    cck_ref, ssk_ref, ccq_ref, ssq_ref,
                    ckv_ref, krope_ref, kcat_ref, qcat_ref, rw_ref):
    x = x_ref[0]
    mod = mod_ref[0]
    sh = mod[:, 0:D_MODEL]
    sc = mod[:, D_MODEL:2 * D_MODEL]
    h = _rms(x, g0_ref[...]) * (1.0 + sc) + sh
    proj = _bdot(h, win_ref[...])
    c_q = proj[:, 0:Q_LORA]
    c_kv = proj[:, Q_LORA:Q_LORA + KV_LORA]
    k_r = proj[:, Q_LORA + KV_LORA:Q_LORA + KV_LORA + ROPE_DIM]
    k_rs = proj[:, Q_LORA + KV_LORA + ROPE_DIM:Q_LORA + KV_LORA + 2 * ROPE_DIM]
    rw_ref[0] = proj[:, RW_OFF:IN_PACKED]

    ckv = _rms(c_kv, kvg_ref[...])
    krope = k_r * cck_ref[...] + k_rs * ssk_ref[...]
    ckv_ref[0] = ckv
    krope_ref[0] = krope
    tm = x.shape[0]
    zpad = jnp.zeros((tm, LANE - ROPE_DIM), F32)
    kcat_ref[0] = jnp.concatenate([ckv, krope, zpad], axis=-1).astype(BF16)

    q = _bdot(_rms(c_q, qg_ref[...]), wuq_ref[...])
    nope_w = A_HEADS * NOPE_DIM
    rope_w = A_HEADS * ROPE_DIM
    q_rope = (q[:, nope_w:nope_w + rope_w] * ccq_ref[...]
              + q[:, nope_w + rope_w:nope_w + 2 * rope_w] * ssq_ref[...])
    for hd in range(A_HEADS):
        q_lat = _bdot(q[:, hd * NOPE_DIM:(hd + 1) * NOPE_DIM], wuk_ref[hd])
        qr = q_rope[:, hd * ROPE_DIM:(hd + 1) * ROPE_DIM]
        qcat_ref[0, hd] = (jnp.concatenate([q_lat, qr, zpad], axis=-1) * SM_SCALE).astype(BF16)


def _in_proj(x, mod, g0, win, qg, kvg, wuq, wuk, cck, ssk, ccq, ssq, tm):
    b, t, _ = x.shape
    grid = (b, t // tm)
    row = lambda w: pl.BlockSpec((1, tm, w), lambda i, j: (i, j, 0))
    tab = lambda w: pl.BlockSpec((tm, w), lambda i, j: (j, 0))
    return pl.pallas_call(
        _in_proj_kernel,
        grid=grid,
        in_specs=[
            row(D_MODEL),
            pl.BlockSpec((1, 1, 6 * D_MODEL), lambda i, j: (i, 0, 0)),
            _const_spec((1, D_MODEL)),
            _const_spec((D_MODEL, IN_PACKED)),
            _const_spec((1, Q_LORA)),
            _const_spec((1, KV_LORA)),
            _const_spec((Q_LORA, wuq.shape[1])),
            _const_spec((A_HEADS, NOPE_DIM, KV_LORA)),
            tab(ROPE_DIM), tab(ROPE_DIM), tab(A_HEADS * ROPE_DIM), tab(A_HEADS * ROPE_DIM),
        ],
        out_specs=[
            row(KV_LORA), row(ROPE_DIM), row(QK_WIDTH),
            pl.BlockSpec((1, A_HEADS, tm, QK_WIDTH), lambda i, j: (i, 0, j, 0)),
            row(RW_COLS),
        ],
        out_shape=[
            jax.ShapeDtypeStruct((b, t, KV_LORA), F32),
            jax.ShapeDtypeStruct((b, t, ROPE_DIM), F32),
            jax.ShapeDtypeStruct((b, t, QK_WIDTH), BF16),
            jax.ShapeDtypeStruct((b, A_HEADS, t, QK_WIDTH), BF16),
            jax.ShapeDtypeStruct((b, t, RW_COLS), F32),
        ],
        compiler_params=_cparams(("parallel", "parallel")),
        name="in_proj",
    )(x, mod, g0, win, qg, kvg, wuq, wuk, cck, ssk, ccq, ssq)


def _softmax_step(s, v, m_ref, l_ref, acc_ref):
    m_prev = m_ref[...]
    m_new = jnp.maximum(m_prev, jnp.max(s, axis=-1, keepdims=True))
    alpha = jnp.exp(m_prev - m_new)
    p = jnp.exp(s - m_new)
    l_ref[...] = alpha * l_ref[...] + jnp.sum(p, axis=-1, keepdims=True)
    acc_ref[...] = alpha * acc_ref[...] + jnp.dot(p.astype(BF16), v, preferred_element_type=F32)
    m_ref[...] = m_new


def _attn_finish(l_ref, acc_ref, wuv_ref, o_ref, tq):
    o_lat = acc_ref[...] / l_ref[...]
    outs = [_bdot(o_lat[hd * tq:(hd + 1) * tq], wuv_ref[hd]) for hd in range(A_HEADS)]
    o_ref[0] = jnp.concatenate(outs, axis=-1).astype(o_ref.dtype)


def _qk(q, k):
    return lax.dot_general(q, k, (((1,), (1,)), ((), ())), preferred_element_type=F32)


def _prompt_attn_kernel(q_ref, k_ref, wuv_ref, o_ref, m_ref, l_ref, acc_ref, *, tq):
    i = pl.program_id(1)
    rows = A_HEADS * tq
    q = q_ref[0].reshape(rows, QK_WIDTH)
    m_ref[...] = jnp.full((rows, 1), -jnp.inf, F32)
    l_ref[...] = jnp.zeros((rows, 1), F32)
    acc_ref[...] = jnp.zeros((rows, KV_LORA), F32)

    def full_tile(j, carry):
        k = k_ref[0, pl.ds(pl.multiple_of(j * tq, tq), tq), :]
        _softmax_step(_qk(q, k), k[:, 0:KV_LORA], m_ref, l_ref, acc_ref)
        return carry

    lax.fori_loop(0, i, full_tile, 0)

    k = k_ref[0, pl.ds(pl.multiple_of(i * tq, tq), tq), :]
    s = _qk(q, k)
    q_chunk = (lax.broadcasted_iota(jnp.int32, (rows, tq), 0) % tq) // CHUNK
    k_chunk = lax.broadcasted_iota(jnp.int32, (rows, tq), 1) // CHUNK
    s = jnp.where(k_chunk <= q_chunk, s, -jnp.inf)
    _softmax_step(s, k[:, 0:KV_LORA], m_ref, l_ref, acc_ref)
    _attn_finish(l_ref, acc_ref, wuv_ref, o_ref, tq)


def _prompt_attention(qcat, kcat, wuv, tq):
    b, _, t, _ = qcat.shape
    rows = A_HEADS * tq
    return pl.pallas_call(
        functools.partial(_prompt_attn_kernel, tq=tq),
        grid=(b, t // tq),
        in_specs=[
            pl.BlockSpec((1, A_HEADS, tq, QK_WIDTH), lambda i, j: (i, 0, j, 0)),
            pl.BlockSpec((1, t, QK_WIDTH), lambda i, j: (i, 0, 0)),
            _const_spec((A_HEADS, KV_LORA, V_DIM)),
        ],
        out_specs=pl.BlockSpec((1, tq, A_HEADS * V_DIM), lambda i, j: (i, j, 0)),
        out_shape=jax.ShapeDtypeStruct((b, t, A_HEADS * V_DIM), BF16),
        scratch_shapes=[
            pltpu.VMEM((rows, 1), F32),
            pltpu.VMEM((rows, 1), F32),
            pltpu.VMEM((rows, KV_LORA), F32),
        ],
        compiler_params=_cparams(("parallel", "arbitrary")),
        name="prompt_attn",
    )(qcat, kcat, wuv)


def _sample_attn_kernel(q_ref, kn_ref, cckv_ref, ckr_ref, wuv_ref, o_ref, m_ref, l_ref, acc_ref, *, tq):
    j = pl.program_id(1)
    rows = A_HEADS * tq
    q = q_ref[0].reshape(rows, QK_WIDTH)

    @pl.when(j == 0)
    def _():
        m_ref[...] = jnp.full((rows, 1), -jnp.inf, F32)
        l_ref[...] = jnp.zeros((rows, 1), F32)
        acc_ref[...] = jnp.zeros((rows, KV_LORA), F32)

    ckv = cckv_ref[0, 0].astype(BF16)
    kr = ckr_ref[0, 0].astype(BF16)
    s = _qk(q[:, 0:KV_LORA], ckv) + _qk(q[:, KV_LORA:KV_LORA + ROPE_DIM], kr)
    _softmax_step(s, ckv, m_ref, l_ref, acc_ref)

    @pl.when(j == pl.num_programs(1) - 1)
    def _():
        kn = kn_ref[0]
        _softmax_step(_qk(q, kn), kn[:, 0:KV_LORA], m_ref, l_ref, acc_ref)
        _attn_finish(l_ref, acc_ref, wuv_ref, o_ref, tq)


def _sample_attention(qcat, kcat, cache_ckv, cache_krope, layer, wuv, tk):
    b, _, tq, _ = qcat.shape
    past = cache_ckv.shape[2]
    rows = A_HEADS * tq
    return pl.pallas_call(
        functools.partial(_sample_attn_kernel, tq=tq),
        grid=(b, past // tk),
        in_specs=[
            pl.BlockSpec((1, A_HEADS, tq, QK_WIDTH), lambda i, j: (i, 0, 0, 0)),
            pl.BlockSpec((1, tq, QK_WIDTH), lambda i, j: (i, 0, 0)),
            pl.BlockSpec((1, 1, tk, KV_LORA), lambda i, j: (layer, i, j, 0)),
            pl.BlockSpec((1, 1, tk, ROPE_DIM), lambda i, j: (layer, i, j, 0)),
            _const_spec((A_HEADS, KV_LORA, V_DIM)),
        ],
        out_specs=pl.BlockSpec((1, tq, A_HEADS * V_DIM), lambda i, j: (i, 0, 0)),
        out_shape=jax.ShapeDtypeStruct((b, tq, A_HEADS * V_DIM), BF16),
        scratch_shapes=[
            pltpu.VMEM((rows, 1), F32),
            pltpu.VMEM((rows, 1), F32),
            pltpu.VMEM((rows, KV_LORA), F32),
        ],
        compiler_params=_cparams(("parallel", "arbitrary")),
        name="sample_attn",
    )(qcat, kcat, cache_ckv, cache_krope, wuv)


def _rwkv_prep_kernel(rw_ref, prev_ref, shift_ref, mu_ref, w0_ref, w2_ref, a0_ref, a2_ref, g2_ref,
                      kkw_ref, ka_ref, rk_ref, bd_ref,
                      r_ref, w_ref, k_ref, v_ref, al_ref, be_ref, g_ref, bonus_ref):
    j = pl.program_id(1)
    rw = rw_ref[0]
    tt = rw.shape[0]
    first = jnp.where(j == 0, shift_ref[0], prev_ref[0, 7:8, :])
    rolled = pltpu.roll(rw, 1, 0)
    row_id = lax.broadcasted_iota(jnp.int32, (tt, 1), 0)
    rw_prev = jnp.where(row_id == 0, first, rolled)
    mix = rw + (rw_prev - rw) * mu_ref[...]
    r = mix[:, 0:R_WIDTH]
    k = mix[:, R_WIDTH:2 * R_WIDTH]
    v = mix[:, 2 * R_WIDTH:3 * R_WIDTH]
    o = 3 * R_WIDTH
    w_lo = mix[:, o:o + DECAY_LORA]
    a_lo = mix[:, o + DECAY_LORA:o + DECAY_LORA + AAA_LORA]
    g_lo = mix[:, o + DECAY_LORA + AAA_LORA:RW_COLS]
    w_raw = w0_ref[...] + _bdot(jnp.tanh(w_lo), w2_ref[...])
    nw = -w_raw
    softplus = jnp.maximum(nw, 0.0) + jnp.log(1.0 + jnp.exp(-jnp.abs(nw)))
    decay = jnp.exp(-jnp.exp(-softplus - 0.5))
    a = _sigmoid(a0_ref[...] + _bdot(a_lo, a2_ref[...]))
    g = _bdot(_sigmoid(g_lo), g2_ref[...])
    bd = bd_ref[...]
    kk = k * kkw_ref[...]
    kk = kk / jnp.maximum(jnp.sqrt(_split_dot(kk * kk, bd)), 1e-12)
    k2 = k * (1.0 + (a - 1.0) * ka_ref[...])
    bonus = _split_dot(r * k2 * rk_ref[...], bd) * v
    r_ref[0] = r
    w_ref[0] = decay
    k_ref[0] = k2
    v_ref[0] = v
    al_ref[0] = -kk
    be_ref[0] = kk * a
    g_ref[0] = g
    bonus_ref[0] = bonus


def _rwkv_prep(rw, shift0, mu, w0, w2, a0, a2, g2, kkw, ka, rk, bd, tt):
    b, t, _ = rw.shape
    sub = tt // 8
    row = pl.BlockSpec((1, tt, R_WIDTH), lambda i, j: (i, j, 0))
    vec = lambda w: _const_spec((1, w))
    out = jax.ShapeDtypeStruct((b, t, R_WIDTH), F32)
    return pl.pallas_call(
        _rwkv_prep_kernel,
        grid=(b, t // tt),
        in_specs=[
            pl.BlockSpec((1, tt, RW_COLS), lambda i, j: (i, j, 0)),
            pl.BlockSpec((1, 8, RW_COLS), lambda i, j: (i, jnp.maximum(j * sub - 1, 0), 0)),
            pl.BlockSpec((1, 1, RW_COLS), lambda i, j: (i, 0, 0)),
            vec(RW_COLS), vec(R_WIDTH), _const_spec((DECAY_LORA, R_WIDTH)),
            vec(R_WIDTH), _const_spec((AAA_LORA, R_WIDTH)), _const_spec((GATE_LORA, R_WIDTH)),
            vec(R_WIDTH), vec(R_WIDTH), vec(R_WIDTH), _const_spec((R_WIDTH, R_WIDTH)),
        ],
        out_specs=[row] * 8,
        out_shape=[out] * 8,
        compiler_params=_cparams(("parallel", "parallel")),
        name="rwkv_prep",
    )(rw, rw, shift0, mu, w0, w2, a0, a2, g2, kkw, ka, rk, bd)


def _rwkv_scan_kernel(r_ref, w_ref, k_ref, v_ref, al_ref, be_ref, g_ref, bonus_ref, s0_ref,
                      lg_ref, lb_ref, bd_ref, eye_ref,
                      o_ref, sout_ref, s_ref, y_ref, *, nb, tt):
    j = pl.program_id(1)
    n = R_HEAD_DIM

    @pl.when(j == 0)
    def _():
        for b in range(nb):
            s_ref[b] = jnp.concatenate([s0_ref[b, h] for h in range(R_HEADS)], axis=-1)

    bd = bd_ref[...]
    eye = eye_ref[...]

    def step(t, carry):
        lhs = []
        for b in range(nb):
            s = s_ref[b]
            lhs.append(s * al_ref[b, pl.ds(t, 1), :])
            lhs.append(eye * v_ref[b, pl.ds(t, 1), :])
        red = _split_dot(jnp.concatenate(lhs, axis=0), bd)
        ys = []
        for b in range(nb):
            z = red[(2 * b) * n:(2 * b + 1) * n]
            vcol = red[(2 * b + 1) * n:(2 * b + 2) * n]
            s_new = (s_ref[b] * w_ref[b, pl.ds(t, 1), :] + z * be_ref[b, pl.ds(t, 1), :]
                     + vcol * k_ref[b, pl.ds(t, 1), :])
            s_ref[b] = s_new
            ys.append(s_new * r_ref[b, pl.ds(t, 1), :])
        yred = _split_dot(jnp.concatenate(ys, axis=0), bd)
        for b in range(nb):
            y_ref[b, pl.ds(t, 1), :] = jnp.sum(yred[b * n:(b + 1) * n] * eye, axis=0, keepdims=True)
        return carry

    lax.fori_loop(0, tt, step, 0)

    inv_n = 1.0 / n
    for b in range(nb):
        y = y_ref[b]
        mu = _split_dot(y, bd) * inv_n
        d = y - mu
        var = _split_dot(d * d, bd) * inv_n
        yn = d * lax.rsqrt(var + LNX_EPS) * lg_ref[...] + lb_ref[...]
        o_ref[b] = ((yn + bonus_ref[b]) * g_ref[b]).astype(o_ref.dtype)

    @pl.when(j == pl.num_programs(1) - 1)
    def _():
        for b in range(nb):
            s = s_ref[b]
            for h in range(R_HEADS):
                sout_ref[b, h] = s[:, h * n:(h + 1) * n]


def _rwkv_scan(r, w, k, v, al, be, g, bonus, s0, lg, lb, bd, eye, nb, tt):
    b, t, _ = r.shape
    row = pl.BlockSpec((nb, tt, R_WIDTH), lambda i, j: (i, j, 0))
    st = pl.BlockSpec((nb, R_HEADS, R_HEAD_DIM, R_HEAD_DIM), lambda i, j: (i, 0, 0, 0))
    return pl.pallas_call(
        functools.partial(_rwkv_scan_kernel, nb=nb, tt=tt),
        grid=(b // nb, t // tt),
        in_specs=[row] * 8 + [st, _const_spec((1, R_WIDTH)), _const_spec((1, R_WIDTH)),
                              _const_spec((R_WIDTH, R_WIDTH)), _const_spec((R_HEAD_DIM, R_WIDTH))],
        out_specs=[row, st],
        out_shape=[jax.ShapeDtypeStruct((b, t, R_WIDTH), BF16),
                   jax.ShapeDtypeStruct((b, R_HEADS, R_HEAD_DIM, R_HEAD_DIM), F32)],
        scratch_shapes=[pltpu.VMEM((nb, R_HEAD_DIM, R_WIDTH), F32),
                        pltpu.VMEM((nb, tt, R_WIDTH), F32)],
        compiler_params=_cparams(("parallel", "arbitrary")),
        name="rwkv_scan",
    )(r, w, k, v, al, be, g, bonus, s0, lg, lb, bd, eye)


def _out_ffn_kernel(x_ref, oa_ref, orw_ref, mod_ref, ng_ref, wout_ref, wg_ref, wu_ref, wd_ref, o_ref):
    x = x_ref[0]
    mod = mod_ref[0]
    gt_a = mod[:, 2 * D_MODEL:3 * D_MODEL]
    sh_f = mod[:, 3 * D_MODEL:4 * D_MODEL]
    sc_f = mod[:, 4 * D_MODEL:5 * D_MODEL]
    gt_f = mod[:, 5 * D_MODEL:6 * D_MODEL]
    half = A_HEADS * V_DIM
    m = (jnp.dot(oa_ref[0], wout_ref[0:half, :], preferred_element_type=F32)
         + jnp.dot(orw_ref[0], wout_ref[half:D_MIX, :], preferred_element_type=F32))
    x = x + gt_a * _rms(m, ng_ref[1:2, :])
    h = (_rms(x, ng_ref[2:3, :]) * (1.0 + sc_f) + sh_f).astype(BF16)
    f = jnp.zeros_like(x)
    n_split = 2
    cw = D_FF // n_split
    for c in range(n_split):
        gate = jnp.dot(h, wg_ref[:, c * cw:(c + 1) * cw], preferred_element_type=F32)
        up = jnp.dot(h, wu_ref[:, c * cw:(c + 1) * cw], preferred_element_type=F32)
        act = (gate * _sigmoid(gate) * up).astype(BF16)
        f = f + jnp.dot(act, wd_ref[c * cw:(c + 1) * cw, :], preferred_element_type=F32)
    o_ref[0] = x + gt_f * _rms(f, ng_ref[3:4, :])


def _out_ffn(x, o_attn, o_rw, mod, ng, wout, wg, wu, wd, tm):
    b, t, _ = x.shape
    row = lambda w: pl.BlockSpec((1, tm, w), lambda i, j: (i, j, 0))
    single = dict(pipeline_mode=pl.Buffered(1))
    wspec = lambda shape: pl.BlockSpec(shape, lambda i, j: (0, 0), **single)
    return pl.pallas_call(
        _out_ffn_kernel,
        grid=(b, t // tm),
        in_specs=[
            row(D_MODEL), row(A_HEADS * V_DIM), row(R_WIDTH),
            pl.BlockSpec((1, 1, 6 * D_MODEL), lambda i, j: (i, 0, 0)),
            _const_spec((4, D_MODEL)),
            wspec((D_MIX, D_MODEL)), wspec((D_MODEL, D_FF)), wspec((D_MODEL, D_FF)),
            wspec((D_FF, D_MODEL)),
        ],
        out_specs=row(D_MODEL),
        out_shape=jax.ShapeDtypeStruct((b, t, D_MODEL), F32),
        compiler_params=_cparams(("parallel", "parallel")),
        name="out_ffn",
    )(x, o_attn, o_rw, mod, ng, wout, wg, wu, wd)


def _rope_tables(pos):
    inv = ROPE_THETA ** (-jnp.arange(0, ROPE_DIM, 2, dtype=F32) / ROPE_DIM)
    ang = pos.astype(F32)[:, None] * inv[None, :]
    cos, sin = jnp.cos(ang), jnp.sin(ang)
    cc = jnp.concatenate([cos, cos], axis=-1)
    ss = jnp.concatenate([-sin, sin], axis=-1)
    return cc, ss, jnp.tile(cc, (1, A_HEADS)), jnp.tile(ss, (1, A_HEADS))


def _pack_weights(w_in, w_uq, w_uk, w_uv):
    depth = w_in.shape[0]
    half = ROPE_DIM // 2
    o = Q_LORA + KV_LORA
    k_r = w_in[:, :, o:o + ROPE_DIM]
    k_rs = jnp.concatenate([k_r[..., half:], k_r[..., :half]], axis=-1)
    pad = jnp.zeros((depth, D_MODEL, LANE - 2 * ROPE_DIM), w_in.dtype)
    win = jnp.concatenate([w_in[:, :, :o], k_r, k_rs, pad, w_in[:, :, MLA_COLS:]], axis=-1)
    uq = w_uq.reshape(depth, Q_LORA, A_HEADS, NOPE_DIM + ROPE_DIM)
    nope = uq[..., :NOPE_DIM].reshape(depth, Q_LORA, A_HEADS * NOPE_DIM)
    rope = uq[..., NOPE_DIM:]
    rope_s = jnp.concatenate([rope[..., half:], rope[..., :half]], axis=-1)
    wuq = jnp.concatenate([nope, rope.reshape(depth, Q_LORA, -1), rope_s.reshape(depth, Q_LORA, -1)], axis=-1)
    wuk = jnp.transpose(w_uk, (0, 2, 3, 1))
    wuv = jnp.transpose(w_uv, (0, 2, 1, 3))
    return win.astype(BF16), wuq.astype(BF16), wuk.astype(BF16), wuv.astype(BF16)


def _block_diag_ones():
    hid = jnp.arange(R_WIDTH) // R_HEAD_DIM
    bd = (hid[:, None] == hid[None, :]).astype(BF16)
    eye = (jnp.arange(R_HEAD_DIM)[:, None] == (jnp.arange(R_WIDTH) % R_HEAD_DIM)[None, :]).astype(F32)
    return bd, eye


def _row_tile(t, pref):
    return pref if t % pref == 0 else t


def kernel(x_prompt, x_sample, c_prompt, c_sample, cache_ckv, cache_krope, state_wkv, state_shift,
           ada_w, ada_b, norm_g, w_in, q_norm_g, kv_norm_g, w_uq, w_uk, w_uv,
           rw_mu, rw_w0, rw_w2, rw_a0, rw_a2, rw_g2, rw_kk, rw_ka, rw_rk, lnx_g, lnx_b,
           w_out, w_gate, w_up, w_down):
    depth = w_in.shape[0]
    bp, tp, _ = x_prompt.shape
    bs, ts, _ = x_sample.shape
    past = cache_ckv.shape[2]

    mod_all = _modulation(jnp.concatenate([c_prompt, c_sample], axis=0), ada_w, ada_b)
    win, wuq, wuk, wuv = _pack_weights(w_in, w_uq, w_uk, w_uv)
    wout, wg, wu, wd = (w.astype(BF16) for w in (w_out, w_gate, w_up, w_down))
    w2, a2, g2 = (w.astype(BF16) for w in (rw_w2, rw_a2, rw_g2))
    bd, eye = _block_diag_ones()
    tabs_p = _rope_tables(jnp.arange(tp))
    tabs_s = _rope_tables(past + jnp.arange(ts))
    vec = lambda a, l: a[l].reshape(1, -1)

    def layer(l, x, mod, tabs, s0, shift0, cache):
        b, t, _ = x.shape
        tm = _row_tile(t, 512)
        ckv, krope, kcat, qcat, rw = _in_proj(
            x, mod, vec(norm_g[:, 0], l), win[l], vec(q_norm_g, l), vec(kv_norm_g, l),
            wuq[l], wuk[l], *tabs, tm)
        if cache is None:
            o_attn = _prompt_attention(qcat, kcat, wuv[l], _row_tile(t, 256))
        else:
            o_attn = _sample_attention(qcat, kcat, cache[0], cache[1], l, wuv[l], 512)
        tt = _row_tile(t, 256)
        r, w, k, v, al, be, g, bonus = _rwkv_prep(
            rw, shift0, vec(rw_mu, l), vec(rw_w0, l), w2[l], vec(rw_a0, l), a2[l], g2[l],
            vec(rw_kk, l), vec(rw_ka, l), rw_rk[l].reshape(1, -1), bd, tt)
        ts_ = _row_tile(t, 128)
        o_rw, s_new = _rwkv_scan(r, w, k, v, al, be, g, bonus, s0, vec(lnx_g, l), vec(lnx_b, l),
                                 bd, eye, 8 if b % 8 == 0 else b, ts_)
        x_new = _out_ffn(x, o_attn, o_rw, mod, norm_g[l], wout[l], wg[l], wu[l], wd[l], tm)
        return x_new, ckv, krope, s_new, rw[:, -1]

    xp, xs = x_prompt, x_sample
    wkv_zero = jnp.zeros((bp, R_HEADS, R_HEAD_DIM, R_HEAD_DIM), F32)
    shift_zero = jnp.zeros((bp, 1, RW_COLS), F32)
    outs_p, outs_s = [], []
    for l in range(depth):
        mod_p = mod_all[l, :bp].reshape(bp, 1, 6 * D_MODEL)
        mod_s = mod_all[l, bp:].reshape(bs, 1, 6 * D_MODEL)
        xp, *rest_p = layer(l, xp, mod_p, tabs_p, wkv_zero, shift_zero, None)
        xs, *rest_s = layer(l, xs, mod_s, tabs_s, state_wkv[l],
                            state_shift[l].reshape(bs, 1, RW_COLS), (cache_ckv, cache_krope))
        outs_p.append(rest_p)
        outs_s.append(rest_s)
    stack = lambda outs, i: jnp.stack([o[i] for o in outs])
    return (xp, xs,
            stack(outs_p, 0), stack(outs_p, 1), stack(outs_p, 2), stack(outs_p, 3),
            stack(outs_s, 0), stack(outs_s, 1), stack(outs_s, 2), stack(outs_s, 3))
```

```python
import functools

import jax
import jax.numpy as jnp
from jax import lax
from jax.experimental import pallas as pl
from jax.experimental.pallas import tpu as pltpu

D_MODEL = 1024
CHUNK = 64
A_HEADS = 8
NOPE_DIM = 64
ROPE_DIM = 32
V_DIM = 64
Q_LORA = 384
KV_LORA = 256
ROPE_THETA = 10000.0
SM_SCALE = (NOPE_DIM + ROPE_DIM) ** -0.5
R_HEADS = 8
R_HEAD_DIM = 64
R_WIDTH = R_HEADS * R_HEAD_DIM
DECAY_LORA = 64
AAA_LORA = 64
GATE_LORA = 128
RW_COLS = 3 * R_WIDTH + DECAY_LORA + AAA_LORA + GATE_LORA
MLA_COLS = Q_LORA + KV_LORA + ROPE_DIM
D_MIX = A_HEADS * V_DIM + R_WIDTH
D_FF = 2816
RMS_EPS = 1e-6
LNX_EPS = 64e-5

LANE = 128
QK_WIDTH = KV_LORA + LANE
RW_OFF = Q_LORA + KV_LORA + LANE
IN_PACKED = RW_OFF + RW_COLS
VMEM_LIMIT = 56 * 1024 * 1024

BF16 = jnp.bfloat16
F32 = jnp.float32


def _cparams(sem):
    return pltpu.CompilerParams(dimension_semantics=sem, vmem_limit_bytes=VMEM_LIMIT)


def _const_spec(shape):
    n = len(shape)
    return pl.BlockSpec(shape, lambda *_: (0,) * n)


def _bdot(a, b):
    return jnp.dot(a.astype(BF16), b.astype(BF16), preferred_element_type=F32)


def _split_dot(a, b):
    hi = a.astype(BF16)
    lo = (a - hi.astype(F32)).astype(BF16)
    return (jnp.dot(hi, b, preferred_element_type=F32)
            + jnp.dot(lo, b, preferred_element_type=F32))


def _rms(x, g):
    return x * lax.rsqrt(jnp.mean(x * x, axis=-1, keepdims=True) + RMS_EPS) * g


def _sigmoid(x):
    return 1.0 / (1.0 + jnp.exp(-x))


def _mod_kernel(c_ref, w_ref, b_ref, o_ref):
    c = c_ref[...]
    s = c * _sigmoid(c)
    o_ref[0] = _bdot(s, w_ref[0]) + b_ref[0]


def _modulation(c_all, ada_w, ada_b):
    depth = ada_w.shape[0]
    n = c_all.shape[0]
    tn = 1024
    return pl.pallas_call(
        _mod_kernel,
        grid=(depth, 6 * D_MODEL // tn),
        in_specs=[
            pl.BlockSpec((n, D_MODEL), lambda l, j: (0, 0)),
            pl.BlockSpec((1, D_MODEL, tn), lambda l, j: (l, 0, j)),
            pl.BlockSpec((1, 1, tn), lambda l, j: (l, 0, j)),
        ],
        out_specs=pl.BlockSpec((1, n, tn), lambda l, j: (l, 0, j)),
        out_shape=jax.ShapeDtypeStruct((depth, n, 6 * D_MODEL), F32),
        compiler_params=_cparams(("parallel", "parallel")),
        name="adaln_mod",
    )(c_all, ada_w, ada_b.reshape(depth, 1, 6 * D_MODEL))


def _in_proj_kernel(x_ref, mod_ref, g0_ref, win_ref, qg_ref, kvg_ref, wuq_ref, wuk_ref,
                    cck_ref, ssk_ref, ccq_ref, ssq_ref,
                    ckv_ref, krope_ref, kcat_ref, vt_ref, qt_ref, rw_ref, *, tq):
    x = x_ref[0]
    mod = mod_ref[0]
    sh = mod[:, 0:D_MODEL]
    sc = mod[:, D_MODEL:2 * D_MODEL]
    h = _rms(x, g0_ref[...]) * (1.0 + sc) + sh
    proj = _bdot(h, win_ref[...])
    c_q = proj[:, 0:Q_LORA]
    c_kv = proj[:, Q_LORA:Q_LORA + KV_LORA]
    k_r = proj[:, Q_LORA + KV_LORA:Q_LORA + KV_LORA + ROPE_DIM]
    k_rs = proj[:, Q_LORA + KV_LORA + ROPE_DIM:Q_LORA + KV_LORA + 2 * ROPE_DIM]
    rw_ref[0] = proj[:, RW_OFF:IN_PACKED]

    ckv = _rms(c_kv, kvg_ref[...])
    krope = k_r * cck_ref[...] + k_rs * ssk_ref[...]
    ckv_ref[0] = ckv
    krope_ref[0] = krope
    tm = x.shape[0]
    zpad = jnp.zeros((tm, LANE - ROPE_DIM), F32)
    kcat_ref[0] = jnp.concatenate([ckv, krope, zpad], axis=-1).astype(BF16)
    vt_ref[0] = ckv.T.astype(BF16)

    cqn_t = _rms(c_q, qg_ref[...]).T.astype(BF16)
    q_t = jnp.dot(wuq_ref[...], cqn_t, preferred_element_type=F32)
    nope_w = A_HEADS * NOPE_DIM
    rope_w = A_HEADS * ROPE_DIM
    q_rope = (q_t[nope_w:nope_w + rope_w] * ccq_ref[...]
              + q_t[nope_w + rope_w:nope_w + 2 * rope_w] * ssq_ref[...])
    zrows = jnp.zeros((LANE - ROPE_DIM, tm), F32)
    for hd in range(A_HEADS):
        q_lat = jnp.dot(wuk_ref[hd], q_t[hd * NOPE_DIM:(hd + 1) * NOPE_DIM].astype(BF16),
                        preferred_element_type=F32)
        qr = q_rope[hd * ROPE_DIM:(hd + 1) * ROPE_DIM]
        qh = (jnp.concatenate([q_lat, qr, zrows], axis=0) * SM_SCALE).astype(BF16)
        for qb in range(tm // tq):
            qt_ref[0, qb, :, hd * tq:(hd + 1) * tq] = qh[:, qb * tq:(qb + 1) * tq]


def _in_proj(x, mod, g0, win, qg, kvg, wuq, wuk, cck, ssk, ccq, ssq, tm, tq):
    b, t, _ = x.shape
    grid = (b, t // tm)
    row = lambda w: pl.BlockSpec((1, tm, w), lambda i, j: (i, j, 0))
    tab = lambda w: pl.BlockSpec((tm, w), lambda i, j: (j, 0))
    tab_t = pl.BlockSpec((A_HEADS * ROPE_DIM, tm), lambda i, j: (0, j))
    nqb = tm // tq
    return pl.pallas_call(
        functools.partial(_in_proj_kernel, tq=tq),
        grid=grid,
        in_specs=[
            row(D_MODEL),
            pl.BlockSpec((1, 1, 6 * D_MODEL), lambda i, j: (i, 0, 0)),
            _const_spec((1, D_MODEL)),
            _const_spec((D_MODEL, IN_PACKED)),
            _const_spec((1, Q_LORA)),
            _const_spec((1, KV_LORA)),
            _const_spec((wuq.shape[0], Q_LORA)),
            _const_spec((A_HEADS, KV_LORA, NOPE_DIM)),
            tab(ROPE_DIM), tab(ROPE_DIM), tab_t, tab_t,
        ],
        out_specs=[
            row(KV_LORA), row(ROPE_DIM), row(QK_WIDTH),
            pl.BlockSpec((1, KV_LORA, tm), lambda i, j: (i, 0, j)),
            pl.BlockSpec((1, nqb, QK_WIDTH, A_HEADS * tq), lambda i, j: (i, j, 0, 0)),
            row(RW_COLS),
        ],
        out_shape=[
            jax.ShapeDtypeStruct((b, t, KV_LORA), F32),
            jax.ShapeDtypeStruct((b, t, ROPE_DIM), F32),
            jax.ShapeDtypeStruct((b, t, QK_WIDTH), BF16),
            jax.ShapeDtypeStruct((b, KV_LORA, t), BF16),
            jax.ShapeDtypeStruct((b, t // tq, QK_WIDTH, A_HEADS * tq), BF16),
            jax.ShapeDtypeStruct((b, t, RW_COLS), F32),
        ],
        compiler_params=_cparams(("parallel", "parallel")),
        name="in_proj",
    )(x, mod, g0, win, qg, kvg, wuq, wuk, cck, ssk, ccq, ssq)


COLS = 2 * LANE


def _softmax_update(s, vt, c, m_ref, l_ref, acc_ref):
    m_prev = m_ref[c]
    m_new = jnp.maximum(m_prev, jnp.max(s, axis=0, keepdims=True))
    alpha = jnp.exp(m_prev - m_new)
    p = jnp.exp(s - m_new)
    l_ref[c] = alpha * l_ref[c] + jnp.sum(p, axis=0, keepdims=True)
    acc_ref[c] = alpha * acc_ref[c] + jnp.dot(vt, p.astype(BF16), preferred_element_type=F32)
    m_ref[c] = m_new


def _attn_init(m_ref, l_ref, acc_ref):
    m_ref[...] = jnp.full(m_ref.shape, -jnp.inf, F32)
    l_ref[...] = jnp.zeros(l_ref.shape, F32)
    acc_ref[...] = jnp.zeros(acc_ref.shape, F32)


def _attn_finish(l_ref, acc_ref, wuv_ref, o_ref, tq):
    outs = []
    for hd in range(A_HEADS):
        c, off = divmod(hd * tq, COLS)
        o_lat = acc_ref[c, :, off:off + tq] / l_ref[c, :, off:off + tq]
        outs.append(jnp.dot(wuv_ref[hd], o_lat.astype(BF16), preferred_element_type=F32))
    o_ref[0] = jnp.concatenate(outs, axis=0).T.astype(o_ref.dtype)


def _attn_scratch(tq):
    nc = A_HEADS * tq // COLS
    return [pltpu.VMEM((nc, 1, COLS), F32), pltpu.VMEM((nc, 1, COLS), F32),
            pltpu.VMEM((nc, KV_LORA, COLS), F32)]


def _prompt_attn_kernel(q_ref, k_ref, vt_ref, wuv_ref, o_ref, m_ref, l_ref, acc_ref, *, tq):
    i = pl.program_id(1)
    nc = A_HEADS * tq // COLS
    _attn_init(m_ref, l_ref, acc_ref)

    def tile(j, bias):
        off = pl.multiple_of(j * tq, tq)
        k = k_ref[0, pl.ds(off, tq), :]
        vt = vt_ref[0, :, pl.ds(off, tq)]
        for c in range(nc):
            s = jnp.dot(k, q_ref[0, 0, :, c * COLS:(c + 1) * COLS], preferred_element_type=F32)
            if bias is not None:
                s = s + bias
            _softmax_update(s, vt, c, m_ref, l_ref, acc_ref)

    def full_tile(j, carry):
        tile(j, None)
        return carry

    lax.fori_loop(0, i, full_tile, 0)

    k_chunk = lax.broadcasted_iota(jnp.int32, (tq, COLS), 0) // CHUNK
    q_chunk = (lax.broadcasted_iota(jnp.int32, (tq, COLS), 1) % tq) // CHUNK
    tile(i, jnp.where(k_chunk <= q_chunk, 0.0, -jnp.inf).astype(F32))
    _attn_finish(l_ref, acc_ref, wuv_ref, o_ref, tq)


def _prompt_attention(qt, kcat, vt, wuv, tq):
    b, t, _ = kcat.shape
    return pl.pallas_call(
        functools.partial(_prompt_attn_kernel, tq=tq),
        grid=(b, t // tq),
        in_specs=[
            pl.BlockSpec((1, 1, QK_WIDTH, A_HEADS * tq), lambda i, j: (i, j, 0, 0)),
            pl.BlockSpec((1, t, QK_WIDTH), lambda i, j: (i, 0, 0)),
            pl.BlockSpec((1, KV_LORA, t), lambda i, j: (i, 0, 0)),
            _const_spec((A_HEADS, V_DIM, KV_LORA)),
        ],
        out_specs=pl.BlockSpec((1, tq, A_HEADS * V_DIM), lambda i, j: (i, j, 0)),
        out_shape=jax.ShapeDtypeStruct((b, t, A_HEADS * V_DIM), BF16),
        scratch_shapes=_attn_scratch(tq),
        compiler_params=_cparams(("parallel", "arbitrary")),
        name="prompt_attn",
    )(qt, kcat, vt, wuv)


def _sample_attn_kernel(q_ref, kn_ref, vtn_ref, cckv_ref, ckr_ref, wuv_ref, o_ref,
                        m_ref, l_ref, acc_ref, *, tq):
    j = pl.program_id(1)
    nc = A_HEADS * tq // COLS

    @pl.when(j == 0)
    def _():
        _attn_init(m_ref, l_ref, acc_ref)

    ckv = cckv_ref[0, 0]
    ckv_b = ckv.astype(BF16)
    vt = ckv.T.astype(BF16)
    kr = ckr_ref[0, 0].astype(BF16)
    for c in range(nc):
        cols = slice(c * COLS, (c + 1) * COLS)
        s = (jnp.dot(ckv_b, q_ref[0, 0, 0:KV_LORA, cols], preferred_element_type=F32)
             + jnp.dot(kr, q_ref[0, 0, KV_LORA:KV_LORA + ROPE_DIM, cols], preferred_element_type=F32))
        _softmax_update(s, vt, c, m_ref, l_ref, acc_ref)

    @pl.when(j == pl.num_programs(1) - 1)
    def _():
        kn = kn_ref[0]
        vtn = vtn_ref[0]
        for c in range(nc):
            s = jnp.dot(kn, q_ref[0, 0, :, c * COLS:(c + 1) * COLS], preferred_element_type=F32)
            _softmax_update(s, vtn, c, m_ref, l_ref, acc_ref)
        _attn_finish(l_ref, acc_ref, wuv_ref, o_ref, tq)


def _sample_attention(qt, kcat, vt, cache_ckv, cache_krope, layer, wuv, tk):
    b, tq, _ = kcat.shape
    past = cache_ckv.shape[2]
    return pl.pallas_call(
        functools.partial(_sample_attn_kernel, tq=tq),
        grid=(b, past // tk),
        in_specs=[
            pl.BlockSpec((1, 1, QK_WIDTH, A_HEADS * tq), lambda i, j: (i, 0, 0, 0)),
            pl.BlockSpec((1, tq, QK_WIDTH), lambda i, j: (i, 0, 0)),
            pl.BlockSpec((1, KV_LORA, tq), lambda i, j: (i, 0, 0)),
            pl.BlockSpec((1, 1, tk, KV_LORA), lambda i, j: (layer, i, j, 0)),
            pl.BlockSpec((1, 1, tk, ROPE_DIM), lambda i, j: (layer, i, j, 0)),
            _const_spec((A_HEADS, V_DIM, KV_LORA)),
        ],
        out_specs=pl.BlockSpec((1, tq, A_HEADS * V_DIM), lambda i, j: (i, 0, 0)),
        out_shape=jax.ShapeDtypeStruct((b, tq, A_HEADS * V_DIM), BF16),
        scratch_shapes=_attn_scratch(tq),
        compiler_params=_cparams(("parallel", "arbitrary")),
        name="sample_attn",
    )(qt, kcat, vt, cache_ckv, cache_krope, wuv)


def _rwkv_prep_kernel(rw_ref, prev_ref, shift_ref, mu_ref, w0_ref, w2_ref, a0_ref, a2_ref, g2_ref,
                      kkw_ref, ka_ref, rk_ref, bd_ref,
                      r_ref, w_ref, k_ref, v_ref, al_ref, be_ref, g_ref, bonus_ref):
    j = pl.program_id(1)
    rw = rw_ref[0]
    tt = rw.shape[0]
    first = jnp.where(j == 0, shift_ref[0], prev_ref[0, 7:8, :])
    rolled = pltpu.roll(rw, 1, 0)
    row_id = lax.broadcasted_iota(jnp.int32, (tt, 1), 0)
    rw_prev = jnp.where(row_id == 0, first, rolled)
    mix = rw + (rw_prev - rw) * mu_ref[...]
    r = mix[:, 0:R_WIDTH]
    k = mix[:, R_WIDTH:2 * R_WIDTH]
    v = mix[:, 2 * R_WIDTH:3 * R_WIDTH]
    o = 3 * R_WIDTH
    w_lo = mix[:, o:o + DECAY_LORA]
    a_lo = mix[:, o + DECAY_LORA:o + DECAY_LORA + AAA_LORA]
    g_lo = mix[:, o + DECAY_LORA + AAA_LORA:RW_COLS]
    w_raw = w0_ref[...] + _bdot(jnp.tanh(w_lo), w2_ref[...])
    nw = -w_raw
    softplus = jnp.maximum(nw, 0.0) + jnp.log(1.0 + jnp.exp(-jnp.abs(nw)))
    decay = jnp.exp(-jnp.exp(-softplus - 0.5))
    a = _sigmoid(a0_ref[...] + _bdot(a_lo, a2_ref[...]))
    g = _bdot(_sigmoid(g_lo), g2_ref[...])
    bd = bd_ref[...]
    kk = k * kkw_ref[...]
    kk = kk / jnp.maximum(jnp.sqrt(_split_dot(kk * kk, bd)), 1e-12)
    k2 = k * (1.0 + (a - 1.0) * ka_ref[...])
    bonus = _split_dot(r * k2 * rk_ref[...], bd) * v
    r_ref[0] = r
    w_ref[0] = decay
    k_ref[0] = k2
    v_ref[0] = v
    al_ref[0] = -kk
    be_ref[0] = kk * a
    g_ref[0] = g
    bonus_ref[0] = bonus


def _rwkv_prep(rw, shift0, mu, w0, w2, a0, a2, g2, kkw, ka, rk, bd, tt):
    b, t, _ = rw.shape
    sub = tt // 8
    row = pl.BlockSpec((1, tt, R_WIDTH), lambda i, j: (i, j, 0))
    vec = lambda w: _const_spec((1, w))
    out = jax.ShapeDtypeStruct((b, t, R_WIDTH), F32)
    return pl.pallas_call(
        _rwkv_prep_kernel,
        grid=(b, t // tt),
        in_specs=[
            pl.BlockSpec((1, tt, RW_COLS), lambda i, j: (i, j, 0)),
            pl.BlockSpec((1, 8, RW_COLS), lambda i, j: (i, jnp.maximum(j * sub - 1, 0), 0)),
            pl.BlockSpec((1, 1, RW_COLS), lambda i, j: (i, 0, 0)),
            vec(RW_COLS), vec(R_WIDTH), _const_spec((DECAY_LORA, R_WIDTH)),
            vec(R_WIDTH), _const_spec((AAA_LORA, R_WIDTH)), _const_spec((GATE_LORA, R_WIDTH)),
            vec(R_WIDTH), vec(R_WIDTH), vec(R_WIDTH), _const_spec((R_WIDTH, R_WIDTH)),
        ],
        out_specs=[row] * 8,
        out_shape=[out] * 8,
        compiler_params=_cparams(("parallel", "parallel")),
        name="rwkv_prep",
    )(rw, rw, shift0, mu, w0, w2, a0, a2, g2, kkw, ka, rk, bd)


def _rwkv_scan_kernel(r_ref, w_ref, k_ref, v_ref, al_ref, be_ref, g_ref, bonus_ref, s0_ref,
                      lg_ref, lb_ref, bd_ref, eye_ref,
                      o_ref, sout_ref, s_ref, y_ref, *, nb, tt):
    j = pl.program_id(1)
    n = R_HEAD_DIM

    @pl.when(j == 0)
    def _():
        for b in range(nb):
            s_ref[b] = jnp.concatenate([s0_ref[b, h] for h in range(R_HEADS)], axis=-1)

    bd = bd_ref[...]
    eye = eye_ref[...]
    half = R_WIDTH // 2
    bd_half = bd[0:half, 0:half]

    def head_sums(x):
        xb = x.astype(BF16)
        return jnp.concatenate(
            [jnp.dot(xb[:, 0:half], bd_half, preferred_element_type=F32),
             jnp.dot(xb[:, half:R_WIDTH], bd_half, preferred_element_type=F32)], axis=-1)

    def to_row(x):
        return jnp.sum(x * eye, axis=0, keepdims=True)

    def step(t, carry):
        tp = jnp.maximum(t - 1, 0)
        lhs = []
        for b in range(nb):
            s = s_ref[b]
            lhs.append(s * al_ref[b, pl.ds(t, 1), :])
            lhs.append(s * r_ref[b, pl.ds(tp, 1), :])
            lhs.append(eye * v_ref[b, pl.ds(t, 1), :])
        red = head_sums(jnp.concatenate(lhs, axis=0))
        for b in range(nb):
            z = red[(3 * b) * n:(3 * b + 1) * n]
            y_ref[b, pl.ds(tp, 1), :] = to_row(red[(3 * b + 1) * n:(3 * b + 2) * n])
            vcol = red[(3 * b + 2) * n:(3 * b + 3) * n]
            s_ref[b] = (s_ref[b] * w_ref[b, pl.ds(t, 1), :] + z * be_ref[b, pl.ds(t, 1), :]
                        + vcol * k_ref[b, pl.ds(t, 1), :])
        return carry

    lax.fori_loop(0, tt, step, 0)
    last = head_sums(jnp.concatenate([s_ref[b] * r_ref[b, tt - 1:tt, :] for b in range(nb)], axis=0))
    for b in range(nb):
        y_ref[b, tt - 1:tt, :] = to_row(last[b * n:(b + 1) * n])

    inv_n = 1.0 / n
    for b in range(nb):
        y = y_ref[b]
        mu = _split_dot(y, bd) * inv_n
        d = y - mu
        var = _split_dot(d * d, bd) * inv_n
        yn = d * lax.rsqrt(var + LNX_EPS) * lg_ref[...] + lb_ref[...]
        o_ref[b] = ((yn + bonus_ref[b]) * g_ref[b]).astype(o_ref.dtype)

    @pl.when(j == pl.num_programs(1) - 1)
    def _():
        for b in range(nb):
            s = s_ref[b]
            for h in range(R_HEADS):
                sout_ref[b, h] = s[:, h * n:(h + 1) * n]


def _rwkv_scan(r, w, k, v, al, be, g, bonus, s0, lg, lb, bd, eye, nb, tt):
    b, t, _ = r.shape
    row = pl.BlockSpec((nb, tt, R_WIDTH), lambda i, j: (i, j, 0))
    st = pl.BlockSpec((nb, R_HEADS, R_HEAD_DIM, R_HEAD_DIM), lambda i, j: (i, 0, 0, 0))
    return pl.pallas_call(
        functools.partial(_rwkv_scan_kernel, nb=nb, tt=tt),
        grid=(b // nb, t // tt),
        in_specs=[row] * 8 + [st, _const_spec((1, R_WIDTH)), _const_spec((1, R_WIDTH)),
                              _const_spec((R_WIDTH, R_WIDTH)), _const_spec((R_HEAD_DIM, R_WIDTH))],
        out_specs=[row, st],
        out_shape=[jax.ShapeDtypeStruct((b, t, R_WIDTH), BF16),
                   jax.ShapeDtypeStruct((b, R_HEADS, R_HEAD_DIM, R_HEAD_DIM), F32)],
        scratch_shapes=[pltpu.VMEM((nb, R_HEAD_DIM, R_WIDTH), F32),
                        pltpu.VMEM((nb, tt, R_WIDTH), F32)],
        compiler_params=_cparams(("parallel", "arbitrary")),
        name="rwkv_scan",
    )(r, w, k, v, al, be, g, bonus, s0, lg, lb, bd, eye)


def _out_ffn_kernel(x_ref, oa_ref, orw_ref, mod_ref, ng_ref, wout_ref, wg_ref, wu_ref, wd_ref, o_ref):
    x = x_ref[0]
    mod = mod_ref[0]
    gt_a = mod[:, 2 * D_MODEL:3 * D_MODEL]
    sh_f = mod[:, 3 * D_MODEL:4 * D_MODEL]
    sc_f = mod[:, 4 * D_MODEL:5 * D_MODEL]
    gt_f = mod[:, 5 * D_MODEL:6 * D_MODEL]
    half = A_HEADS * V_DIM
    m = (jnp.dot(oa_ref[0], wout_ref[0:half, :], preferred_element_type=F32)
         + jnp.dot(orw_ref[0], wout_ref[half:D_MIX, :], preferred_element_type=F32))
    x = x + gt_a * _rms(m, ng_ref[1:2, :])
    h = (_rms(x, ng_ref[2:3, :]) * (1.0 + sc_f) + sh_f).astype(BF16)
    f = jnp.zeros_like(x)
    n_split = 2
    cw = D_FF // n_split
    for c in range(n_split):
        gate = jnp.dot(h, wg_ref[:, c * cw:(c + 1) * cw], preferred_element_type=F32)
        up = jnp.dot(h, wu_ref[:, c * cw:(c + 1) * cw], preferred_element_type=F32)
        act = (gate * _sigmoid(gate) * up).astype(BF16)
        f = f + jnp.dot(act, wd_ref[c * cw:(c + 1) * cw, :], preferred_element_type=F32)
    o_ref[0] = x + gt_f * _rms(f, ng_ref[3:4, :])


def _out_ffn(x, o_attn, o_rw, mod, ng, wout, wg, wu, wd, tm):
    b, t, _ = x.shape
    row = lambda w: pl.BlockSpec((1, tm, w), lambda i, j: (i, j, 0))
    single = dict(pipeline_mode=pl.Buffered(1))
    wspec = lambda shape: pl.BlockSpec(shape, lambda i, j: (0, 0), **single)
    return pl.pallas_call(
        _out_ffn_kernel,
        grid=(b, t // tm),
        in_specs=[
            row(D_MODEL), row(A_HEADS * V_DIM), row(R_WIDTH),
            pl.BlockSpec((1, 1, 6 * D_MODEL), lambda i, j: (i, 0, 0)),
            _const_spec((4, D_MODEL)),
            wspec((D_MIX, D_MODEL)), wspec((D_MODEL, D_FF)), wspec((D_MODEL, D_FF)),
            wspec((D_FF, D_MODEL)),
        ],
        out_specs=row(D_MODEL),
        out_shape=jax.ShapeDtypeStruct((b, t, D_MODEL), F32),
        compiler_params=_cparams(("parallel", "parallel")),
        name="out_ffn",
    )(x, o_attn, o_rw, mod, ng, wout, wg, wu, wd)


def _rope_tables(pos):
    inv = ROPE_THETA ** (-jnp.arange(0, ROPE_DIM, 2, dtype=F32) / ROPE_DIM)
    ang = pos.astype(F32)[:, None] * inv[None, :]
    cos, sin = jnp.cos(ang), jnp.sin(ang)
    cc = jnp.concatenate([cos, cos], axis=-1)
    ss = jnp.concatenate([-sin, sin], axis=-1)
    return cc, ss, jnp.tile(cc, (1, A_HEADS)).T, jnp.tile(ss, (1, A_HEADS)).T


def _pack_weights(w_in, w_uq, w_uk, w_uv):
    depth = w_in.shape[0]
    half = ROPE_DIM // 2
    o = Q_LORA + KV_LORA
    k_r = w_in[:, :, o:o + ROPE_DIM]
    k_rs = jnp.concatenate([k_r[..., half:], k_r[..., :half]], axis=-1)
    pad = jnp.zeros((depth, D_MODEL, LANE - 2 * ROPE_DIM), w_in.dtype)
    win = jnp.concatenate([w_in[:, :, :o], k_r, k_rs, pad, w_in[:, :, MLA_COLS:]], axis=-1)
    uq = w_uq.reshape(depth, Q_LORA, A_HEADS, NOPE_DIM + ROPE_DIM)
    nope = uq[..., :NOPE_DIM].reshape(depth, Q_LORA, A_HEADS * NOPE_DIM)
    rope = uq[..., NOPE_DIM:]
    rope_s = jnp.concatenate([rope[..., half:], rope[..., :half]], axis=-1)
    wuq = jnp.concatenate([nope, rope.reshape(depth, Q_LORA, -1), rope_s.reshape(depth, Q_LORA, -1)], axis=-1)
    wuq = jnp.transpose(wuq, (0, 2, 1))
    wuk = jnp.transpose(w_uk, (0, 2, 1, 3))
    wuv = jnp.transpose(w_uv, (0, 2, 3, 1))
    return win.astype(BF16), wuq.astype(BF16), wuk.astype(BF16), wuv.astype(BF16)


def _block_diag_ones():
    hid = jnp.arange(R_WIDTH) // R_HEAD_DIM
    bd = (hid[:, None] == hid[None, :]).astype(BF16)
    eye = (jnp.arange(R_HEAD_DIM)[:, None] == (jnp.arange(R_WIDTH) % R_HEAD_DIM)[None, :]).astype(F32)
    return bd, eye


def _row_tile(t, pref):
    return pref if t % pref == 0 else t


def kernel(x_prompt, x_sample, c_prompt, c_sample, cache_ckv, cache_krope, state_wkv, state_shift,
           ada_w, ada_b, norm_g, w_in, q_norm_g, kv_norm_g, w_uq, w_uk, w_uv,
           rw_mu, rw_w0, rw_w2, rw_a0, rw_a2, rw_g2, rw_kk, rw_ka, rw_rk, lnx_g, lnx_b,
           w_out, w_gate, w_up, w_down):
    depth = w_in.shape[0]
    bp, tp, _ = x_prompt.shape
    bs, ts, _ = x_sample.shape
    past = cache_ckv.shape[2]

    mod_all = _modulation(jnp.concatenate([c_prompt, c_sample], axis=0), ada_w, ada_b)
    win, wuq, wuk, wuv = _pack_weights(w_in, w_uq, w_uk, w_uv)
    wout, wg, wu, wd = (w.astype(BF16) for w in (w_out, w_gate, w_up, w_down))
    w2, a2, g2 = (w.astype(BF16) for w in (rw_w2, rw_a2, rw_g2))
    bd, eye = _block_diag_ones()
    tabs_p = _rope_tables(jnp.arange(tp))
    tabs_s = _rope_tables(past + jnp.arange(ts))
    vec = lambda a, l: a[l].reshape(1, -1)

    def layer(l, x, mod, tabs, s0, shift0, cache):
        b, t, _ = x.shape
        tm = _row_tile(t, 512)
        tq = _row_tile(t, COLS)
        ckv, krope, kcat, vt, qt, rw = _in_proj(
            x, mod, vec(norm_g[:, 0], l), win[l], vec(q_norm_g, l), vec(kv_norm_g, l),
            wuq[l], wuk[l], *tabs, tm, tq)
        if cache is None:
            o_attn = _prompt_attention(qt, kcat, vt, wuv[l], tq)
        else:
            o_attn = _sample_attention(qt, kcat, vt, cache[0], cache[1], l, wuv[l], 512)
        tt = _row_tile(t, 256)
        r, w, k, v, al, be, g, bonus = _rwkv_prep(
            rw, shift0, vec(rw_mu, l), vec(rw_w0, l), w2[l], vec(rw_a0, l), a2[l], g2[l],
            vec(rw_kk, l), vec(rw_ka, l), rw_rk[l].reshape(1, -1), bd, tt)
        ts_ = _row_tile(t, 128)
        o_rw, s_new = _rwkv_scan(r, w, k, v, al, be, g, bonus, s0, vec(lnx_g, l), vec(lnx_b, l),
                                 bd, eye, 8 if b % 8 == 0 else b, ts_)
        x_new = _out_ffn(x, o_attn, o_rw, mod, norm_g[l], wout[l], wg[l], wu[l], wd[l], tm)
        return x_new, ckv, krope, s_new, rw[:, -1]

    xp, xs = x_prompt, x_sample
    wkv_zero = jnp.zeros((bp, R_HEADS, R_HEAD_DIM, R_HEAD_DIM), F32)
    shift_zero = jnp.zeros((bp, 1, RW_COLS), F32)
    outs_p, outs_s = [], []
    for l in range(depth):
        mod_p = mod_all[l, :bp].reshape(bp, 1, 6 * D_MODEL)
        mod_s = mod_all[l, bp:].reshape(bs, 1, 6 * D_MODEL)
        xp, *rest_p = layer(l, xp, mod_p, tabs_p, wkv_zero, shift_zero, None)
        xs, *rest_s = layer(l, xs, mod_s, tabs_s, state_wkv[l],
                            state_shift[l].reshape(bs, 1, RW_COLS), (cache_ckv, cache_krope))
        outs_p.append(rest_p)
        outs_s.append(rest_s)
    stack = lambda outs, i: jnp.stack([o[i] for o in outs])
    return (xp, xs,
            stack(outs_p, 0), stack(outs_p, 1), stack(outs_p, 2), stack(outs_p, 3),
            stack(outs_s, 0), stack(outs_s, 1), stack(outs_s, 2), stack(outs_s, 3))
```

```python
import functools

import jax
import jax.numpy as jnp
from jax import lax
from jax.experimental import pallas as pl
from jax.experimental.pallas import tpu as pltpu

D_MODEL = 1024
CHUNK = 64
A_HEADS = 8
NOPE_DIM = 64
ROPE_DIM = 32
V_DIM = 64
Q_LORA = 384
KV_LORA = 256
ROPE_THETA = 10000.0
SM_SCALE = (NOPE_DIM + ROPE_DIM) ** -0.5
R_HEADS = 8
R_HEAD_DIM = 64
R_WIDTH = R_HEADS * R_HEAD_DIM
DECAY_LORA = 64
AAA_LORA = 64
GATE_LORA = 128
RW_COLS = 3 * R_WIDTH + DECAY_LORA + AAA_LORA + GATE_LORA
MLA_COLS = Q_LORA + KV_LORA + ROPE_DIM
D_MIX = A_HEADS * V_DIM + R_WIDTH
D_FF = 2816
RMS_EPS = 1e-6
LNX_EPS = 64e-5

LANE = 128
QK_WIDTH = KV_LORA + LANE
RW_OFF = Q_LORA + KV_LORA + LANE
IN_PACKED = RW_OFF + RW_COLS
VMEM_LIMIT = 56 * 1024 * 1024

BF16 = jnp.bfloat16
F32 = jnp.float32


def _cparams(sem):
    return pltpu.CompilerParams(dimension_semantics=sem, vmem_limit_bytes=VMEM_LIMIT)


def _const_spec(shape):
    n = len(shape)
    return pl.BlockSpec(shape, lambda *_: (0,) * n)


def _bdot(a, b):
    return jnp.dot(a.astype(BF16), b.astype(BF16), preferred_element_type=F32)


def _split_dot(a, b):
    hi = a.astype(BF16)
    lo = (a - hi.astype(F32)).astype(BF16)
    return (jnp.dot(hi, b, preferred_element_type=F32)
            + jnp.dot(lo, b, preferred_element_type=F32))


def _rms(x, g):
    return x * lax.rsqrt(jnp.mean(x * x, axis=-1, keepdims=True) + RMS_EPS) * g


def _sigmoid(x):
    return 1.0 / (1.0 + jnp.exp(-x))


def _mod_kernel(c_ref, w_ref, b_ref, o_ref):
    c = c_ref[...]
    s = c * _sigmoid(c)
    o_ref[0] = _bdot(s, w_ref[0]) + b_ref[0]


def _modulation(c_all, ada_w, ada_b):
    depth = ada_w.shape[0]
    n = c_all.shape[0]
    tn = 1024
    return pl.pallas_call(
        _mod_kernel,
        grid=(depth, 6 * D_MODEL // tn),
        in_specs=[
            pl.BlockSpec((n, D_MODEL), lambda l, j: (0, 0)),
            pl.BlockSpec((1, D_MODEL, tn), lambda l, j: (l, 0, j)),
            pl.BlockSpec((1, 1, tn), lambda l, j: (l, 0, j)),
        ],
        out_specs=pl.BlockSpec((1, n, tn), lambda l, j: (l, 0, j)),
        out_shape=jax.ShapeDtypeStruct((depth, n, 6 * D_MODEL), F32),
        compiler_params=_cparams(("parallel", "parallel")),
        name="adaln_mod",
    )(c_all, ada_w, ada_b.reshape(depth, 1, 6 * D_MODEL))


def _in_proj_kernel(x_ref, mod_ref, g0_ref, win_ref, qg_ref, kvg_ref, wuq_ref, wuk_ref,
                    cck_ref, ssk_ref, ccq_ref, ssq_ref,
                    ckv_ref, krope_ref, kcat_ref, vt_ref, qt_ref, rw_ref, *, tq):
    x = x_ref[0]
    mod = mod_ref[0]
    sh = mod[:, 0:D_MODEL]
    sc = mod[:, D_MODEL:2 * D_MODEL]
    h = _rms(x, g0_ref[...]) * (1.0 + sc) + sh
    proj = _bdot(h, win_ref[...])
    c_q = proj[:, 0:Q_LORA]
    c_kv = proj[:, Q_LORA:Q_LORA + KV_LORA]
    k_r = proj[:, Q_LORA + KV_LORA:Q_LORA + KV_LORA + ROPE_DIM]
    k_rs = proj[:, Q_LORA + KV_LORA + ROPE_DIM:Q_LORA + KV_LORA + 2 * ROPE_DIM]
    rw_ref[0] = proj[:, RW_OFF:IN_PACKED]

    ckv = _rms(c_kv, kvg_ref[...])
    krope = k_r * cck_ref[...] + k_rs * ssk_ref[...]
    ckv_ref[0] = ckv
    krope_ref[0] = krope
    tm = x.shape[0]
    zpad = jnp.zeros((tm, LANE - ROPE_DIM), F32)
    kcat_ref[0] = jnp.concatenate([ckv, krope, zpad], axis=-1).astype(BF16)
    vt_ref[0] = ckv.T.astype(BF16)

    cqn_t = _rms(c_q, qg_ref[...]).T.astype(BF16)
    q_t = jnp.dot(wuq_ref[...], cqn_t, preferred_element_type=F32)
    nope_w = A_HEADS * NOPE_DIM
    rope_w = A_HEADS * ROPE_DIM
    q_rope = (q_t[nope_w:nope_w + rope_w] * ccq_ref[...]
              + q_t[nope_w + rope_w:nope_w + 2 * rope_w] * ssq_ref[...])
    zrows = jnp.zeros((LANE - ROPE_DIM, tm), F32)
    for hd in range(A_HEADS):
        q_lat = jnp.dot(wuk_ref[hd], q_t[hd * NOPE_DIM:(hd + 1) * NOPE_DIM].astype(BF16),
                        preferred_element_type=F32)
        qr = q_rope[hd * ROPE_DIM:(hd + 1) * ROPE_DIM]
        qh = (jnp.concatenate([q_lat, qr, zrows], axis=0) * SM_SCALE).astype(BF16)
        for qb in range(tm // tq):
            qt_ref[0, qb, :, hd * tq:(hd + 1) * tq] = qh[:, qb * tq:(qb + 1) * tq]


def _in_proj(x, mod, g0, win, qg, kvg, wuq, wuk, cck, ssk, ccq, ssq, tm, tq):
    b, t, _ = x.shape
    grid = (b, t // tm)
    row = lambda w: pl.BlockSpec((1, tm, w), lambda i, j: (i, j, 0))
    tab = lambda w: pl.BlockSpec((tm, w), lambda i, j: (j, 0))
    tab_t = pl.BlockSpec((A_HEADS * ROPE_DIM, tm), lambda i, j: (0, j))
    nqb = tm // tq
    return pl.pallas_call(
        functools.partial(_in_proj_kernel, tq=tq),
        grid=grid,
        in_specs=[
            row(D_MODEL),
            pl.BlockSpec((1, 1, 6 * D_MODEL), lambda i, j: (i, 0, 0)),
            _const_spec((1, D_MODEL)),
            _const_spec((D_MODEL, IN_PACKED)),
            _const_spec((1, Q_LORA)),
            _const_spec((1, KV_LORA)),
            _const_spec((wuq.shape[0], Q_LORA)),
            _const_spec((A_HEADS, KV_LORA, NOPE_DIM)),
            tab(ROPE_DIM), tab(ROPE_DIM), tab_t, tab_t,
        ],
        out_specs=[
            row(KV_LORA), row(ROPE_DIM), row(QK_WIDTH),
            pl.BlockSpec((1, KV_LORA, tm), lambda i, j: (i, 0, j)),
            pl.BlockSpec((1, nqb, QK_WIDTH, A_HEADS * tq), lambda i, j: (i, j, 0, 0)),
            row(RW_COLS),
        ],
        out_shape=[
            jax.ShapeDtypeStruct((b, t, KV_LORA), F32),
            jax.ShapeDtypeStruct((b, t, ROPE_DIM), F32),
            jax.ShapeDtypeStruct((b, t, QK_WIDTH), BF16),
            jax.ShapeDtypeStruct((b, KV_LORA, t), BF16),
            jax.ShapeDtypeStruct((b, t // tq, QK_WIDTH, A_HEADS * tq), BF16),
            jax.ShapeDtypeStruct((b, t, RW_COLS), F32),
        ],
        compiler_params=_cparams(("parallel", "parallel")),
        name="in_proj",
    )(x, mod, g0, win, qg, kvg, wuq, wuk, cck, ssk, ccq, ssq)


COLS = 2 * LANE
SCORE_LOOKAHEAD = 4
SAMPLE_SUB_KEYS = 512


def _softmax_update(s, vt, c, m_ref, l_ref, acc_ref):
    m_prev = m_ref[c]
    m_new = jnp.maximum(m_prev, jnp.max(s, axis=0, keepdims=True))
    alpha = jnp.exp(m_prev - m_new)
    p = jnp.exp(s - m_new)
    l_ref[c] = alpha * l_ref[c] + jnp.sum(p, axis=0, keepdims=True)
    acc_ref[c] = alpha * acc_ref[c] + jnp.dot(vt, p.astype(BF16), preferred_element_type=F32)
    m_ref[c] = m_new


def _pipelined_updates(units, m_ref, l_ref, acc_ref):
    n = len(units)
    pending = [units[u][0]() for u in range(min(SCORE_LOOKAHEAD, n))]
    for u in range(n):
        if u + SCORE_LOOKAHEAD < n:
            pending.append(units[u + SCORE_LOOKAHEAD][0]())
        _softmax_update(pending.pop(0), units[u][1], units[u][2], m_ref, l_ref, acc_ref)


def _pipelined_chunks(scores, vt, nc, m_ref, l_ref, acc_ref):
    _pipelined_updates([(functools.partial(scores, c), vt, c) for c in range(nc)], m_ref, l_ref, acc_ref)


def _attn_init(m_ref, l_ref, acc_ref):
    m_ref[...] = jnp.full(m_ref.shape, -jnp.inf, F32)
    l_ref[...] = jnp.zeros(l_ref.shape, F32)
    acc_ref[...] = jnp.zeros(acc_ref.shape, F32)


def _attn_finish(l_ref, acc_ref, wuv_ref, o_ref, tq):
    outs = []
    for hd in range(A_HEADS):
        c, off = divmod(hd * tq, COLS)
        o_lat = acc_ref[c, :, off:off + tq] / l_ref[c, :, off:off + tq]
        outs.append(jnp.dot(wuv_ref[hd], o_lat.astype(BF16), preferred_element_type=F32))
    o_ref[0] = jnp.concatenate(outs, axis=0).T.astype(o_ref.dtype)


def _attn_scratch(tq):
    nc = A_HEADS * tq // COLS
    return [pltpu.VMEM((nc, 1, COLS), F32), pltpu.VMEM((nc, 1, COLS), F32),
            pltpu.VMEM((nc, KV_LORA, COLS), F32)]


def _prompt_attn_kernel(q_ref, k_ref, vt_ref, wuv_ref, o_ref, m_ref, l_ref, acc_ref, *, tq):
    i = pl.program_id(1)
    nc = A_HEADS * tq // COLS
    _attn_init(m_ref, l_ref, acc_ref)

    def tile(j, bias):
        off = pl.multiple_of(j * tq, tq)
        k = k_ref[0, pl.ds(off, tq), :]
        vt = vt_ref[0, :, pl.ds(off, tq)]

        def scores(c):
            s = jnp.dot(k, q_ref[0, 0, :, c * COLS:(c + 1) * COLS], preferred_element_type=F32)
            return s if bias is None else s + bias

        _pipelined_chunks(scores, vt, nc, m_ref, l_ref, acc_ref)

    def full_tile(j, carry):
        tile(j, None)
        return carry

    lax.fori_loop(0, i, full_tile, 0)

    k_chunk = lax.broadcasted_iota(jnp.int32, (tq, COLS), 0) // CHUNK
    q_chunk = (lax.broadcasted_iota(jnp.int32, (tq, COLS), 1) % tq) // CHUNK
    tile(i, jnp.where(k_chunk <= q_chunk, 0.0, -jnp.inf).astype(F32))
    _attn_finish(l_ref, acc_ref, wuv_ref, o_ref, tq)


def _prompt_attention(qt, kcat, vt, wuv, tq):
    b, t, _ = kcat.shape
    return pl.pallas_call(
        functools.partial(_prompt_attn_kernel, tq=tq),
        grid=(b, t // tq),
        in_specs=[
            pl.BlockSpec((1, 1, QK_WIDTH, A_HEADS * tq), lambda i, j: (i, j, 0, 0)),
            pl.BlockSpec((1, t, QK_WIDTH), lambda i, j: (i, 0, 0)),
            pl.BlockSpec((1, KV_LORA, t), lambda i, j: (i, 0, 0)),
            _const_spec((A_HEADS, V_DIM, KV_LORA)),
        ],
        out_specs=pl.BlockSpec((1, tq, A_HEADS * V_DIM), lambda i, j: (i, j, 0)),
        out_shape=jax.ShapeDtypeStruct((b, t, A_HEADS * V_DIM), BF16),
        scratch_shapes=_attn_scratch(tq),
        compiler_params=_cparams(("parallel", "arbitrary")),
        name="prompt_attn",
    )(qt, kcat, vt, wuv)


def _sample_attn_kernel(q_ref, kn_ref, vtn_ref, cckv_ref, ckr_ref, wuv_ref, o_ref,
                        m_ref, l_ref, acc_ref, *, tq):
    j = pl.program_id(1)
    nc = A_HEADS * tq // COLS

    @pl.when(j == 0)
    def _():
        _attn_init(m_ref, l_ref, acc_ref)

    def cache_scores(ckv_b, kr, c):
        cols = slice(c * COLS, (c + 1) * COLS)
        return (jnp.dot(ckv_b, q_ref[0, 0, 0:KV_LORA, cols], preferred_element_type=F32)
                + jnp.dot(kr, q_ref[0, 0, KV_LORA:KV_LORA + ROPE_DIM, cols], preferred_element_type=F32))

    units = []
    for sub in range(cckv_ref.shape[2] // SAMPLE_SUB_KEYS):
        keys = slice(sub * SAMPLE_SUB_KEYS, (sub + 1) * SAMPLE_SUB_KEYS)
        ckv = cckv_ref[0, 0, keys, :]
        ckv_b = ckv.astype(BF16)
        vt = ckv.T.astype(BF16)
        kr = ckr_ref[0, 0, keys, :].astype(BF16)
        units += [(functools.partial(cache_scores, ckv_b, kr, c), vt, c) for c in range(nc)]
    _pipelined_updates(units, m_ref, l_ref, acc_ref)

    @pl.when(j == pl.num_programs(1) - 1)
    def _():
        kn = kn_ref[0]
        new_scores = lambda c: jnp.dot(kn, q_ref[0, 0, :, c * COLS:(c + 1) * COLS],
                                       preferred_element_type=F32)
        _pipelined_chunks(new_scores, vtn_ref[0], nc, m_ref, l_ref, acc_ref)
        _attn_finish(l_ref, acc_ref, wuv_ref, o_ref, tq)


def _sample_attention(qt, kcat, vt, cache_ckv, cache_krope, layer, wuv, tk):
    b, tq, _ = kcat.shape
    past = cache_ckv.shape[2]
    return pl.pallas_call(
        functools.partial(_sample_attn_kernel, tq=tq),
        grid=(b, past // tk),
        in_specs=[
            pl.BlockSpec((1, 1, QK_WIDTH, A_HEADS * tq), lambda i, j: (i, 0, 0, 0)),
            pl.BlockSpec((1, tq, QK_WIDTH), lambda i, j: (i, 0, 0)),
            pl.BlockSpec((1, KV_LORA, tq), lambda i, j: (i, 0, 0)),
            pl.BlockSpec((1, 1, tk, KV_LORA), lambda i, j: (layer, i, j, 0)),
            pl.BlockSpec((1, 1, tk, ROPE_DIM), lambda i, j: (layer, i, j, 0)),
            _const_spec((A_HEADS, V_DIM, KV_LORA)),
        ],
        out_specs=pl.BlockSpec((1, tq, A_HEADS * V_DIM), lambda i, j: (i, 0, 0)),
        out_shape=jax.ShapeDtypeStruct((b, tq, A_HEADS * V_DIM), BF16),
        scratch_shapes=_attn_scratch(tq),
        compiler_params=_cparams(("parallel", "arbitrary")),
        name="sample_attn",
    )(qt, kcat, vt, cache_ckv, cache_krope, wuv)


def _rwkv_prep_kernel(rw_ref, prev_ref, shift_ref, mu_ref, w0_ref, w2_ref, a0_ref, a2_ref, g2_ref,
                      kkw_ref, ka_ref, rk_ref, bd_ref,
                      r_ref, w_ref, k_ref, v_ref, al_ref, be_ref, g_ref, bonus_ref):
    j = pl.program_id(1)
    rw = rw_ref[0]
    tt = rw.shape[0]
    first = jnp.where(j == 0, shift_ref[0], prev_ref[0, 7:8, :])
    rolled = pltpu.roll(rw, 1, 0)
    row_id = lax.broadcasted_iota(jnp.int32, (tt, 1), 0)
    rw_prev = jnp.where(row_id == 0, first, rolled)
    mix = rw + (rw_prev - rw) * mu_ref[...]
    r = mix[:, 0:R_WIDTH]
    k = mix[:, R_WIDTH:2 * R_WIDTH]
    v = mix[:, 2 * R_WIDTH:3 * R_WIDTH]
    o = 3 * R_WIDTH
    w_lo = mix[:, o:o + DECAY_LORA]
    a_lo = mix[:, o + DECAY_LORA:o + DECAY_LORA + AAA_LORA]
    g_lo = mix[:, o + DECAY_LORA + AAA_LORA:RW_COLS]
    w_raw = w0_ref[...] + _bdot(jnp.tanh(w_lo), w2_ref[...])
    nw = -w_raw
    softplus = jnp.maximum(nw, 0.0) + jnp.log(1.0 + jnp.exp(-jnp.abs(nw)))
    decay = jnp.exp(-jnp.exp(-softplus - 0.5))
    a = _sigmoid(a0_ref[...] + _bdot(a_lo, a2_ref[...]))
    g = _bdot(_sigmoid(g_lo), g2_ref[...])
    bd = bd_ref[...]
    kk = k * kkw_ref[...]
    kk = kk / jnp.maximum(jnp.sqrt(_split_dot(kk * kk, bd)), 1e-12)
    k2 = k * (1.0 + (a - 1.0) * ka_ref[...])
    bonus = _split_dot(r * k2 * rk_ref[...], bd) * v
    r_ref[0] = r
    w_ref[0] = decay
    k_ref[0] = k2
    v_ref[0] = v
    al_ref[0] = -kk
    be_ref[0] = kk * a
    g_ref[0] = g
    bonus_ref[0] = bonus


def _rwkv_prep(rw, shift0, mu, w0, w2, a0, a2, g2, kkw, ka, rk, bd, tt):
    b, t, _ = rw.shape
    sub = tt // 8
    row = pl.BlockSpec((1, tt, R_WIDTH), lambda i, j: (i, j, 0))
    vec = lambda w: _const_spec((1, w))
    out = jax.ShapeDtypeStruct((b, t, R_WIDTH), F32)
    return pl.pallas_call(
        _rwkv_prep_kernel,
        grid=(b, t // tt),
        in_specs=[
            pl.BlockSpec((1, tt, RW_COLS), lambda i, j: (i, j, 0)),
            pl.BlockSpec((1, 8, RW_COLS), lambda i, j: (i, jnp.maximum(j * sub - 1, 0), 0)),
            pl.BlockSpec((1, 1, RW_COLS), lambda i, j: (i, 0, 0)),
            vec(RW_COLS), vec(R_WIDTH), _const_spec((DECAY_LORA, R_WIDTH)),
            vec(R_WIDTH), _const_spec((AAA_LORA, R_WIDTH)), _const_spec((GATE_LORA, R_WIDTH)),
            vec(R_WIDTH), vec(R_WIDTH), vec(R_WIDTH), _const_spec((R_WIDTH, R_WIDTH)),
        ],
        out_specs=[row] * 8,
        out_shape=[out] * 8,
        compiler_params=_cparams(("parallel", "parallel")),
        name="rwkv_prep",
    )(rw, rw, shift0, mu, w0, w2, a0, a2, g2, kkw, ka, rk, bd)


def _rwkv_scan_kernel(r_ref, w_ref, k_ref, v_ref, al_ref, be_ref, g_ref, bonus_ref, s0_ref,
                      lg_ref, lb_ref, bd_ref, eye_ref,
                      o_ref, sout_ref, s_ref, y_ref, *, nb, tt):
    j = pl.program_id(1)
    n = R_HEAD_DIM

    @pl.when(j == 0)
    def _():
        for b in range(nb):
            s_ref[b] = jnp.concatenate([s0_ref[b, h] for h in range(R_HEADS)], axis=-1)

    bd = bd_ref[...]
    eye = eye_ref[...]
    half = R_WIDTH // 2
    bd_half = bd[0:half, 0:half]

    def head_sums(x):
        xb = x.astype(BF16)
        return jnp.concatenate(
            [jnp.dot(xb[:, 0:half], bd_half, preferred_element_type=F32),
             jnp.dot(xb[:, half:R_WIDTH], bd_half, preferred_element_type=F32)], axis=-1)

    def to_row(x):
        return jnp.sum(x * eye, axis=0, keepdims=True)

    def step(t, carry):
        tp = jnp.maximum(t - 1, 0)
        lhs = []
        for b in range(nb):
            s = s_ref[b]
            lhs.append(s * al_ref[b, pl.ds(t, 1), :])
            lhs.append(s * r_ref[b, pl.ds(tp, 1), :])
            lhs.append(eye * v_ref[b, pl.ds(t, 1), :])
        red = head_sums(jnp.concatenate(lhs, axis=0))
        for b in range(nb):
            z = red[(3 * b) * n:(3 * b + 1) * n]
            y_ref[b, pl.ds(tp, 1), :] = to_row(red[(3 * b + 1) * n:(3 * b + 2) * n])
            vcol = red[(3 * b + 2) * n:(3 * b + 3) * n]
            s_ref[b] = (s_ref[b] * w_ref[b, pl.ds(t, 1), :] + z * be_ref[b, pl.ds(t, 1), :]
                        + vcol * k_ref[b, pl.ds(t, 1), :])
        return carry

    lax.fori_loop(0, tt, step, 0)
    last = head_sums(jnp.concatenate([s_ref[b] * r_ref[b, tt - 1:tt, :] for b in range(nb)], axis=0))
    for b in range(nb):
        y_ref[b, tt - 1:tt, :] = to_row(last[b * n:(b + 1) * n])

    inv_n = 1.0 / n
    for b in range(nb):
        y = y_ref[b]
        mu = _split_dot(y, bd) * inv_n
        d = y - mu
        var = _split_dot(d * d, bd) * inv_n
        yn = d * lax.rsqrt(var + LNX_EPS) * lg_ref[...] + lb_ref[...]
        o_ref[b] = ((yn + bonus_ref[b]) * g_ref[b]).astype(o_ref.dtype)

    @pl.when(j == pl.num_programs(1) - 1)
    def _():
        for b in range(nb):
            s = s_ref[b]
            for h in range(R_HEADS):
                sout_ref[b, h] = s[:, h * n:(h + 1) * n]


def _rwkv_scan(r, w, k, v, al, be, g, bonus, s0, lg, lb, bd, eye, nb, tt):
    b, t, _ = r.shape
    row = pl.BlockSpec((nb, tt, R_WIDTH), lambda i, j: (i, j, 0))
    st = pl.BlockSpec((nb, R_HEADS, R_HEAD_DIM, R_HEAD_DIM), lambda i, j: (i, 0, 0, 0))
    return pl.pallas_call(
        functools.partial(_rwkv_scan_kernel, nb=nb, tt=tt),
        grid=(b // nb, t // tt),
        in_specs=[row] * 8 + [st, _const_spec((1, R_WIDTH)), _const_spec((1, R_WIDTH)),
                              _const_spec((R_WIDTH, R_WIDTH)), _const_spec((R_HEAD_DIM, R_WIDTH))],
        out_specs=[row, st],
        out_shape=[jax.ShapeDtypeStruct((b, t, R_WIDTH), BF16),
                   jax.ShapeDtypeStruct((b, R_HEADS, R_HEAD_DIM, R_HEAD_DIM), F32)],
        scratch_shapes=[pltpu.VMEM((nb, R_HEAD_DIM, R_WIDTH), F32),
                        pltpu.VMEM((nb, tt, R_WIDTH), F32)],
        compiler_params=_cparams(("parallel", "arbitrary")),
        name="rwkv_scan",
    )(r, w, k, v, al, be, g, bonus, s0, lg, lb, bd, eye)


def _out_ffn_kernel(x_ref, oa_ref, orw_ref, mod_ref, ng_ref, wout_ref, wg_ref, wu_ref, wd_ref, o_ref):
    x = x_ref[0]
    mod = mod_ref[0]
    gt_a = mod[:, 2 * D_MODEL:3 * D_MODEL]
    sh_f = mod[:, 3 * D_MODEL:4 * D_MODEL]
    sc_f = mod[:, 4 * D_MODEL:5 * D_MODEL]
    gt_f = mod[:, 5 * D_MODEL:6 * D_MODEL]
    half = A_HEADS * V_DIM
    m = (jnp.dot(oa_ref[0], wout_ref[0:half, :], preferred_element_type=F32)
         + jnp.dot(orw_ref[0], wout_ref[half:D_MIX, :], preferred_element_type=F32))
    x = x + gt_a * _rms(m, ng_ref[1:2, :])
    h = (_rms(x, ng_ref[2:3, :]) * (1.0 + sc_f) + sh_f).astype(BF16)
    f = jnp.zeros_like(x)
    n_split = 2
    cw = D_FF // n_split
    for c in range(n_split):
        gate = jnp.dot(h, wg_ref[:, c * cw:(c + 1) * cw], preferred_element_type=F32)
        up = jnp.dot(h, wu_ref[:, c * cw:(c + 1) * cw], preferred_element_type=F32)
        act = (gate * _sigmoid(gate) * up).astype(BF16)
        f = f + jnp.dot(act, wd_ref[c * cw:(c + 1) * cw, :], preferred_element_type=F32)
    o_ref[0] = x + gt_f * _rms(f, ng_ref[3:4, :])


def _out_ffn(x, o_attn, o_rw, mod, ng, wout, wg, wu, wd, tm):
    b, t, _ = x.shape
    row = lambda w: pl.BlockSpec((1, tm, w), lambda i, j: (i, j, 0))
    single = dict(pipeline_mode=pl.Buffered(1))
    wspec = lambda shape: pl.BlockSpec(shape, lambda i, j: (0, 0), **single)
    return pl.pallas_call(
        _out_ffn_kernel,
        grid=(b, t // tm),
        in_specs=[
            row(D_MODEL), row(A_HEADS * V_DIM), row(R_WIDTH),
            pl.BlockSpec((1, 1, 6 * D_MODEL), lambda i, j: (i, 0, 0)),
            _const_spec((4, D_MODEL)),
            wspec((D_MIX, D_MODEL)), wspec((D_MODEL, D_FF)), wspec((D_MODEL, D_FF)),
            wspec((D_FF, D_MODEL)),
        ],
        out_specs=row(D_MODEL),
        out_shape=jax.ShapeDtypeStruct((b, t, D_MODEL), F32),
        compiler_params=_cparams(("parallel", "parallel")),
        name="out_ffn",
    )(x, o_attn, o_rw, mod, ng, wout, wg, wu, wd)


def _rope_tables(pos):
    inv = ROPE_THETA ** (-jnp.arange(0, ROPE_DIM, 2, dtype=F32) / ROPE_DIM)
    ang = pos.astype(F32)[:, None] * inv[None, :]
    cos, sin = jnp.cos(ang), jnp.sin(ang)
    cc = jnp.concatenate([cos, cos], axis=-1)
    ss = jnp.concatenate([-sin, sin], axis=-1)
    return cc, ss, jnp.tile(cc, (1, A_HEADS)).T, jnp.tile(ss, (1, A_HEADS)).T


def _pack_weights(w_in, w_uq, w_uk, w_uv):
    depth = w_in.shape[0]
    half = ROPE_DIM // 2
    o = Q_LORA + KV_LORA
    k_r = w_in[:, :, o:o + ROPE_DIM]
    k_rs = jnp.concatenate([k_r[..., half:], k_r[..., :half]], axis=-1)
    pad = jnp.zeros((depth, D_MODEL, LANE - 2 * ROPE_DIM), w_in.dtype)
    win = jnp.concatenate([w_in[:, :, :o], k_r, k_rs, pad, w_in[:, :, MLA_COLS:]], axis=-1)
    uq = w_uq.reshape(depth, Q_LORA, A_HEADS, NOPE_DIM + ROPE_DIM)
    nope = uq[..., :NOPE_DIM].reshape(depth, Q_LORA, A_HEADS * NOPE_DIM)
    rope = uq[..., NOPE_DIM:]
    rope_s = jnp.concatenate([rope[..., half:], rope[..., :half]], axis=-1)
    wuq = jnp.concatenate([nope, rope.reshape(depth, Q_LORA, -1), rope_s.reshape(depth, Q_LORA, -1)], axis=-1)
    wuq = jnp.transpose(wuq, (0, 2, 1))
    wuk = jnp.transpose(w_uk, (0, 2, 1, 3))
    wuv = jnp.transpose(w_uv, (0, 2, 3, 1))
    return win.astype(BF16), wuq.astype(BF16), wuk.astype(BF16), wuv.astype(BF16)


def _block_diag_ones():
    hid = jnp.arange(R_WIDTH) // R_HEAD_DIM
    bd = (hid[:, None] == hid[None, :]).astype(BF16)
    eye = (jnp.arange(R_HEAD_DIM)[:, None] == (jnp.arange(R_WIDTH) % R_HEAD_DIM)[None, :]).astype(F32)
    return bd, eye


def _row_tile(t, pref):
    return pref if t % pref == 0 else t


def kernel(x_prompt, x_sample, c_prompt, c_sample, cache_ckv, cache_krope, state_wkv, state_shift,
           ada_w, ada_b, norm_g, w_in, q_norm_g, kv_norm_g, w_uq, w_uk, w_uv,
           rw_mu, rw_w0, rw_w2, rw_a0, rw_a2, rw_g2, rw_kk, rw_ka, rw_rk, lnx_g, lnx_b,
           w_out, w_gate, w_up, w_down):
    depth = w_in.shape[0]
    bp, tp, _ = x_prompt.shape
    bs, ts, _ = x_sample.shape
    past = cache_ckv.shape[2]

    mod_all = _modulation(jnp.concatenate([c_prompt, c_sample], axis=0), ada_w, ada_b)
    win, wuq, wuk, wuv = _pack_weights(w_in, w_uq, w_uk, w_uv)
    wout, wg, wu, wd = (w.astype(BF16) for w in (w_out, w_gate, w_up, w_down))
    w2, a2, g2 = (w.astype(BF16) for w in (rw_w2, rw_a2, rw_g2))
    bd, eye = _block_diag_ones()
    tabs_p = _rope_tables(jnp.arange(tp))
    tabs_s = _rope_tables(past + jnp.arange(ts))
    vec = lambda a, l: a[l].reshape(1, -1)

    def layer(l, x, mod, tabs, s0, shift0, cache):
        b, t, _ = x.shape
        tm = _row_tile(t, 512)
        tq = _row_tile(t, COLS)
        ckv, krope, kcat, vt, qt, rw = _in_proj(
            x, mod, vec(norm_g[:, 0], l), win[l], vec(q_norm_g, l), vec(kv_norm_g, l),
            wuq[l], wuk[l], *tabs, tm, tq)
        if cache is None:
            o_attn = _prompt_attention(qt, kcat, vt, wuv[l], tq)
        else:
            o_attn = _sample_attention(qt, kcat, vt, cache[0], cache[1], l, wuv[l], min(2048, past))
        tt = _row_tile(t, 256)
        r, w, k, v, al, be, g, bonus = _rwkv_prep(
            rw, shift0, vec(rw_mu, l), vec(rw_w0, l), w2[l], vec(rw_a0, l), a2[l], g2[l],
            vec(rw_kk, l), vec(rw_ka, l), rw_rk[l].reshape(1, -1), bd, tt)
        ts_ = _row_tile(t, 128)
        o_rw, s_new = _rwkv_scan(r, w, k, v, al, be, g, bonus, s0, vec(lnx_g, l), vec(lnx_b, l),
                                 bd, eye, 8 if b % 8 == 0 else b, ts_)
        x_new = _out_ffn(x, o_attn, o_rw, mod, norm_g[l], wout[l], wg[l], wu[l], wd[l], tm)
        return x_new, ckv, krope, s_new, rw[:, -1]

    xp, xs = x_prompt, x_sample
    wkv_zero = jnp.zeros((bp, R_HEADS, R_HEAD_DIM, R_HEAD_DIM), F32)
    shift_zero = jnp.zeros((bp, 1, RW_COLS), F32)
    outs_p, outs_s = [], []
    for l in range(depth):
        mod_p = mod_all[l, :bp].reshape(bp, 1, 6 * D_MODEL)
        mod_s = mod_all[l, bp:].reshape(bs, 1, 6 * D_MODEL)
        xp, *rest_p = layer(l, xp, mod_p, tabs_p, wkv_zero, shift_zero, None)
        xs, *rest_s = layer(l, xs, mod_s, tabs_s, state_wkv[l],
                            state_shift[l].reshape(bs, 1, RW_COLS), (cache_ckv, cache_krope))
        outs_p.append(rest_p)
        outs_s.append(rest_s)
    stack = lambda outs, i: jnp.stack([o[i] for o in outs])
    return (xp, xs,
            stack(outs_p, 0), stack(outs_p, 1), stack(outs_p, 2), stack(outs_p, 3),
            stack(outs_s, 0), stack(outs_s, 1), stack(outs_s, 2), stack(outs_s, 3))
```

```python
import functools

import jax
import jax.numpy as jnp
from jax import lax
from jax.experimental import pallas as pl
from jax.experimental.pallas import tpu as pltpu

D_MODEL = 1024
CHUNK = 64
A_HEADS = 8
NOPE_DIM = 64
ROPE_DIM = 32
V_DIM = 64
Q_LORA = 384
KV_LORA = 256
ROPE_THETA = 10000.0
SM_SCALE = (NOPE_DIM + ROPE_DIM) ** -0.5
R_HEADS = 8
R_HEAD_DIM = 64
R_WIDTH = R_HEADS * R_HEAD_DIM
DECAY_LORA = 64
AAA_LORA = 64
GATE_LORA = 128
RW_COLS = 3 * R_WIDTH + DECAY_LORA + AAA_LORA + GATE_LORA
MLA_COLS = Q_LORA + KV_LORA + ROPE_DIM
D_MIX = A_HEADS * V_DIM + R_WIDTH
D_FF = 2816
RMS_EPS = 1e-6
LNX_EPS = 64e-5

LANE = 128
QK_WIDTH = KV_LORA + LANE
RW_OFF = Q_LORA + KV_LORA + LANE
IN_PACKED = RW_OFF + RW_COLS
VMEM_LIMIT = 56 * 1024 * 1024

BF16 = jnp.bfloat16
F32 = jnp.float32


def _cparams(sem):
    return pltpu.CompilerParams(dimension_semantics=sem, vmem_limit_bytes=VMEM_LIMIT)


def _const_spec(shape):
    n = len(shape)
    return pl.BlockSpec(shape, lambda *_: (0,) * n)


def _bdot(a, b):
    return jnp.dot(a.astype(BF16), b.astype(BF16), preferred_element_type=F32)


def _split_dot(a, b):
    hi = a.astype(BF16)
    lo = (a - hi.astype(F32)).astype(BF16)
    return (jnp.dot(hi, b, preferred_element_type=F32)
            + jnp.dot(lo, b, preferred_element_type=F32))


def _rms(x, g):
    return x * lax.rsqrt(jnp.mean(x * x, axis=-1, keepdims=True) + RMS_EPS) * g


def _sigmoid(x):
    return 1.0 / (1.0 + jnp.exp(-x))


def _mod_kernel(c_ref, w_ref, b_ref, o_ref):
    c = c_ref[...]
    s = c * _sigmoid(c)
    o_ref[0] = _bdot(s, w_ref[0]) + b_ref[0]


def _modulation(c_all, ada_w, ada_b):
    depth = ada_w.shape[0]
    n = c_all.shape[0]
    tn = 1024
    return pl.pallas_call(
        _mod_kernel,
        grid=(depth, 6 * D_MODEL // tn),
        in_specs=[
            pl.BlockSpec((n, D_MODEL), lambda l, j: (0, 0)),
            pl.BlockSpec((1, D_MODEL, tn), lambda l, j: (l, 0, j)),
            pl.BlockSpec((1, 1, tn), lambda l, j: (l, 0, j)),
        ],
        out_specs=pl.BlockSpec((1, n, tn), lambda l, j: (l, 0, j)),
        out_shape=jax.ShapeDtypeStruct((depth, n, 6 * D_MODEL), F32),
        compiler_params=_cparams(("parallel", "parallel")),
        name="adaln_mod",
    )(c_all, ada_w, ada_b.reshape(depth, 1, 6 * D_MODEL))


def _in_proj_kernel(x_ref, mod_ref, g0_ref, win_ref, qg_ref, kvg_ref, wuq_ref, wuk_ref,
                    cck_ref, ssk_ref, ccq_ref, ssq_ref,
                    ckv_ref, krope_ref, kcat_ref, vt_ref, qt_ref, rw_ref, *, tq):
    x = x_ref[0]
    mod = mod_ref[0]
    sh = mod[:, 0:D_MODEL]
    sc = mod[:, D_MODEL:2 * D_MODEL]
    h = _rms(x, g0_ref[...]) * (1.0 + sc) + sh
    proj = _bdot(h, win_ref[...])
    c_q = proj[:, 0:Q_LORA]
    c_kv = proj[:, Q_LORA:Q_LORA + KV_LORA]
    k_r = proj[:, Q_LORA + KV_LORA:Q_LORA + KV_LORA + ROPE_DIM]
    k_rs = proj[:, Q_LORA + KV_LORA + ROPE_DIM:Q_LORA + KV_LORA + 2 * ROPE_DIM]
    rw_ref[0] = proj[:, RW_OFF:IN_PACKED]

    ckv = _rms(c_kv, kvg_ref[...])
    krope = k_r * cck_ref[...] + k_rs * ssk_ref[...]
    ckv_ref[0] = ckv
    krope_ref[0] = krope
    tm = x.shape[0]
    zpad = jnp.zeros((tm, LANE - ROPE_DIM), F32)
    kcat_ref[0] = jnp.concatenate([ckv, krope, zpad], axis=-1).astype(BF16)
    vt_ref[0] = ckv.T.astype(BF16)

    cqn_t = _rms(c_q, qg_ref[...]).T.astype(BF16)
    q_t = jnp.dot(wuq_ref[...], cqn_t, preferred_element_type=F32)
    nope_w = A_HEADS * NOPE_DIM
    rope_w = A_HEADS * ROPE_DIM
    q_rope = (q_t[nope_w:nope_w + rope_w] * ccq_ref[...]
              + q_t[nope_w + rope_w:nope_w + 2 * rope_w] * ssq_ref[...])
    zrows = jnp.zeros((LANE - ROPE_DIM, tm), F32)
    for hd in range(A_HEADS):
        q_lat = jnp.dot(wuk_ref[hd], q_t[hd * NOPE_DIM:(hd + 1) * NOPE_DIM].astype(BF16),
                        preferred_element_type=F32)
        qr = q_rope[hd * ROPE_DIM:(hd + 1) * ROPE_DIM]
        qh = (jnp.concatenate([q_lat, qr, zrows], axis=0) * SM_SCALE).astype(BF16)
        for qb in range(tm // tq):
            qt_ref[0, qb, :, hd * tq:(hd + 1) * tq] = qh[:, qb * tq:(qb + 1) * tq]


def _in_proj(x, mod, g0, win, qg, kvg, wuq, wuk, cck, ssk, ccq, ssq, tm, tq):
    b, t, _ = x.shape
    grid = (b, t // tm)
    row = lambda w: pl.BlockSpec((1, tm, w), lambda i, j: (i, j, 0))
    tab = lambda w: pl.BlockSpec((tm, w), lambda i, j: (j, 0))
    tab_t = pl.BlockSpec((A_HEADS * ROPE_DIM, tm), lambda i, j: (0, j))
    nqb = tm // tq
    return pl.pallas_call(
        functools.partial(_in_proj_kernel, tq=tq),
        grid=grid,
        in_specs=[
            row(D_MODEL),
            pl.BlockSpec((1, 1, 6 * D_MODEL), lambda i, j: (i, 0, 0)),
            _const_spec((1, D_MODEL)),
            _const_spec((D_MODEL, IN_PACKED)),
            _const_spec((1, Q_LORA)),
            _const_spec((1, KV_LORA)),
            _const_spec((wuq.shape[0], Q_LORA)),
            _const_spec((A_HEADS, KV_LORA, NOPE_DIM)),
            tab(ROPE_DIM), tab(ROPE_DIM), tab_t, tab_t,
        ],
        out_specs=[
            row(KV_LORA), row(ROPE_DIM), row(QK_WIDTH),
            pl.BlockSpec((1, KV_LORA, tm), lambda i, j: (i, 0, j)),
            pl.BlockSpec((1, nqb, QK_WIDTH, A_HEADS * tq), lambda i, j: (i, j, 0, 0)),
            row(RW_COLS),
        ],
        out_shape=[
            jax.ShapeDtypeStruct((b, t, KV_LORA), F32),
            jax.ShapeDtypeStruct((b, t, ROPE_DIM), F32),
            jax.ShapeDtypeStruct((b, t, QK_WIDTH), BF16),
            jax.ShapeDtypeStruct((b, KV_LORA, t), BF16),
            jax.ShapeDtypeStruct((b, t // tq, QK_WIDTH, A_HEADS * tq), BF16),
            jax.ShapeDtypeStruct((b, t, RW_COLS), F32),
        ],
        compiler_params=_cparams(("parallel", "parallel")),
        name="in_proj",
    )(x, mod, g0, win, qg, kvg, wuq, wuk, cck, ssk, ccq, ssq)


COLS = 2 * LANE
SCORE_LOOKAHEAD = 4
SAMPLE_SUB_KEYS = 512


def _softmax_update(s, vt, c, m_ref, l_ref, acc_ref):
    m_prev = m_ref[c]
    m_new = jnp.maximum(m_prev, jnp.max(s, axis=0, keepdims=True))
    alpha = jnp.exp(m_prev - m_new)
    p = jnp.exp(s - m_new)
    l_ref[c] = alpha * l_ref[c] + jnp.sum(p, axis=0, keepdims=True)
    acc_ref[c] = alpha * acc_ref[c] + jnp.dot(vt, p.astype(BF16), preferred_element_type=F32)
    m_ref[c] = m_new


def _pipelined_updates(units, m_ref, l_ref, acc_ref):
    n = len(units)
    pending = [units[u][0]() for u in range(min(SCORE_LOOKAHEAD, n))]
    for u in range(n):
        if u + SCORE_LOOKAHEAD < n:
            pending.append(units[u + SCORE_LOOKAHEAD][0]())
        _softmax_update(pending.pop(0), units[u][1], units[u][2], m_ref, l_ref, acc_ref)


def _pipelined_chunks(scores, vt, nc, m_ref, l_ref, acc_ref):
    _pipelined_updates([(functools.partial(scores, c), vt, c) for c in range(nc)], m_ref, l_ref, acc_ref)


def _attn_init(m_ref, l_ref, acc_ref):
    m_ref[...] = jnp.full(m_ref.shape, -jnp.inf, F32)
    l_ref[...] = jnp.zeros(l_ref.shape, F32)
    acc_ref[...] = jnp.zeros(acc_ref.shape, F32)


def _attn_finish(l_ref, acc_ref, wuv_ref, o_ref, tq):
    outs = []
    for hd in range(A_HEADS):
        c, off = divmod(hd * tq, COLS)
        o_lat = acc_ref[c, :, off:off + tq] / l_ref[c, :, off:off + tq]
        outs.append(jnp.dot(wuv_ref[hd], o_lat.astype(BF16), preferred_element_type=F32))
    o_ref[0] = jnp.concatenate(outs, axis=0).T.astype(o_ref.dtype)


def _attn_scratch(tq):
    nc = A_HEADS * tq // COLS
    return [pltpu.VMEM((nc, 1, COLS), F32), pltpu.VMEM((nc, 1, COLS), F32),
            pltpu.VMEM((nc, KV_LORA, COLS), F32)]


def _prompt_attn_kernel(q_ref, k_ref, vt_ref, wuv_ref, o_ref, m_ref, l_ref, acc_ref, *, tq):
    i = pl.program_id(1)
    nc = A_HEADS * tq // COLS
    _attn_init(m_ref, l_ref, acc_ref)

    def tile(j, bias):
        off = pl.multiple_of(j * tq, tq)
        k = k_ref[0, pl.ds(off, tq), :]
        vt = vt_ref[0, :, pl.ds(off, tq)]

        def scores(c):
            s = jnp.dot(k, q_ref[0, 0, :, c * COLS:(c + 1) * COLS], preferred_element_type=F32)
            return s if bias is None else s + bias

        _pipelined_chunks(scores, vt, nc, m_ref, l_ref, acc_ref)

    def full_tile(j, carry):
        tile(j, None)
        return carry

    lax.fori_loop(0, i, full_tile, 0)

    k_chunk = lax.broadcasted_iota(jnp.int32, (tq, COLS), 0) // CHUNK
    q_chunk = (lax.broadcasted_iota(jnp.int32, (tq, COLS), 1) % tq) // CHUNK
    tile(i, jnp.where(k_chunk <= q_chunk, 0.0, -jnp.inf).astype(F32))
    _attn_finish(l_ref, acc_ref, wuv_ref, o_ref, tq)


def _prompt_attention(qt, kcat, vt, wuv, tq):
    b, t, _ = kcat.shape
    return pl.pallas_call(
        functools.partial(_prompt_attn_kernel, tq=tq),
        grid=(b, t // tq),
        in_specs=[
            pl.BlockSpec((1, 1, QK_WIDTH, A_HEADS * tq), lambda i, j: (i, j, 0, 0)),
            pl.BlockSpec((1, t, QK_WIDTH), lambda i, j: (i, 0, 0)),
            pl.BlockSpec((1, KV_LORA, t), lambda i, j: (i, 0, 0)),
            _const_spec((A_HEADS, V_DIM, KV_LORA)),
        ],
        out_specs=pl.BlockSpec((1, tq, A_HEADS * V_DIM), lambda i, j: (i, j, 0)),
        out_shape=jax.ShapeDtypeStruct((b, t, A_HEADS * V_DIM), BF16),
        scratch_shapes=_attn_scratch(tq),
        compiler_params=_cparams(("parallel", "arbitrary")),
        name="prompt_attn",
    )(qt, kcat, vt, wuv)


def _sample_attn_kernel(q_ref, kn_ref, vtn_ref, cckv_ref, ckr_ref, wuv_ref, o_ref,
                        m_ref, l_ref, acc_ref, *, tq):
    j = pl.program_id(1)
    nc = A_HEADS * tq // COLS

    @pl.when(j == 0)
    def _():
        _attn_init(m_ref, l_ref, acc_ref)

    def cache_scores(ckv_b, kr, c):
        cols = slice(c * COLS, (c + 1) * COLS)
        return (jnp.dot(ckv_b, q_ref[0, 0, 0:KV_LORA, cols], preferred_element_type=F32)
                + jnp.dot(kr, q_ref[0, 0, KV_LORA:KV_LORA + ROPE_DIM, cols], preferred_element_type=F32))

    units = []
    for sub in range(cckv_ref.shape[2] // SAMPLE_SUB_KEYS):
        keys = slice(sub * SAMPLE_SUB_KEYS, (sub + 1) * SAMPLE_SUB_KEYS)
        ckv = cckv_ref[0, 0, keys, :]
        ckv_b = ckv.astype(BF16)
        vt = ckv.T.astype(BF16)
        kr = ckr_ref[0, 0, keys, :].astype(BF16)
        units += [(functools.partial(cache_scores, ckv_b, kr, c), vt, c) for c in range(nc)]
    _pipelined_updates(units, m_ref, l_ref, acc_ref)

    @pl.when(j == pl.num_programs(1) - 1)
    def _():
        kn = kn_ref[0]
        new_scores = lambda c: jnp.dot(kn, q_ref[0, 0, :, c * COLS:(c + 1) * COLS],
                                       preferred_element_type=F32)
        _pipelined_chunks(new_scores, vtn_ref[0], nc, m_ref, l_ref, acc_ref)
        _attn_finish(l_ref, acc_ref, wuv_ref, o_ref, tq)


def _sample_attention(qt, kcat, vt, cache_ckv, cache_krope, layer, wuv, tk):
    b, tq, _ = kcat.shape
    past = cache_ckv.shape[2]
    return pl.pallas_call(
        functools.partial(_sample_attn_kernel, tq=tq),
        grid=(b, past // tk),
        in_specs=[
            pl.BlockSpec((1, 1, QK_WIDTH, A_HEADS * tq), lambda i, j: (i, 0, 0, 0)),
            pl.BlockSpec((1, tq, QK_WIDTH), lambda i, j: (i, 0, 0)),
            pl.BlockSpec((1, KV_LORA, tq), lambda i, j: (i, 0, 0)),
            pl.BlockSpec((1, 1, tk, KV_LORA), lambda i, j: (layer, i, j, 0)),
            pl.BlockSpec((1, 1, tk, ROPE_DIM), lambda i, j: (layer, i, j, 0)),
            _const_spec((A_HEADS, V_DIM, KV_LORA)),
        ],
        out_specs=pl.BlockSpec((1, tq, A_HEADS * V_DIM), lambda i, j: (i, 0, 0)),
        out_shape=jax.ShapeDtypeStruct((b, tq, A_HEADS * V_DIM), BF16),
        scratch_shapes=_attn_scratch(tq),
        compiler_params=_cparams(("parallel", "arbitrary")),
        name="sample_attn",
    )(qt, kcat, vt, cache_ckv, cache_krope, wuv)


def _rwkv_prep_kernel(rw_ref, prev_ref, shift_ref, mu_ref, w0_ref, w2_ref, a0_ref, a2_ref, g2_ref,
                      kkw_ref, ka_ref, rk_ref, bd_ref,
                      r_ref, w_ref, k_ref, v_ref, al_ref, be_ref, g_ref, bonus_ref):
    j = pl.program_id(1)
    rw = rw_ref[0]
    tt = rw.shape[0]
    first = jnp.where(j == 0, shift_ref[0], prev_ref[0, 7:8, :])
    rolled = pltpu.roll(rw, 1, 0)
    row_id = lax.broadcasted_iota(jnp.int32, (tt, 1), 0)
    rw_prev = jnp.where(row_id == 0, first, rolled)
    mix = rw + (rw_prev - rw) * mu_ref[...]
    r = mix[:, 0:R_WIDTH]
    k = mix[:, R_WIDTH:2 * R_WIDTH]
    v = mix[:, 2 * R_WIDTH:3 * R_WIDTH]
    o = 3 * R_WIDTH
    w_lo = mix[:, o:o + DECAY_LORA]
    a_lo = mix[:, o + DECAY_LORA:o + DECAY_LORA + AAA_LORA]
    g_lo = mix[:, o + DECAY_LORA + AAA_LORA:RW_COLS]
    w_raw = w0_ref[...] + _bdot(jnp.tanh(w_lo), w2_ref[...])
    nw = -w_raw
    softplus = jnp.maximum(nw, 0.0) + jnp.log(1.0 + jnp.exp(-jnp.abs(nw)))
    log_decay = -jnp.exp(-softplus - 0.5)
    a = _sigmoid(a0_ref[...] + _bdot(a_lo, a2_ref[...]))
    g = _bdot(_sigmoid(g_lo), g2_ref[...])
    bd = bd_ref[...]
    kk = k * kkw_ref[...]
    kk = kk / jnp.maximum(jnp.sqrt(_split_dot(kk * kk, bd)), 1e-12)
    k2 = k * (1.0 + (a - 1.0) * ka_ref[...])
    bonus = _split_dot(r * k2 * rk_ref[...], bd) * v
    r_ref[0] = r
    w_ref[0] = log_decay
    k_ref[0] = k2
    v_ref[0] = v
    al_ref[0] = -kk
    be_ref[0] = kk * a
    g_ref[0] = g
    bonus_ref[0] = bonus


def _rwkv_prep(rw, shift0, mu, w0, w2, a0, a2, g2, kkw, ka, rk, bd, tt):
    b, t, _ = rw.shape
    sub = tt // 8
    row = pl.BlockSpec((1, tt, R_WIDTH), lambda i, j: (i, j, 0))
    vec = lambda w: _const_spec((1, w))
    out = jax.ShapeDtypeStruct((b, t, R_WIDTH), F32)
    return pl.pallas_call(
        _rwkv_prep_kernel,
        grid=(b, t // tt),
        in_specs=[
            pl.BlockSpec((1, tt, RW_COLS), lambda i, j: (i, j, 0)),
            pl.BlockSpec((1, 8, RW_COLS), lambda i, j: (i, jnp.maximum(j * sub - 1, 0), 0)),
            pl.BlockSpec((1, 1, RW_COLS), lambda i, j: (i, 0, 0)),
            vec(RW_COLS), vec(R_WIDTH), _const_spec((DECAY_LORA, R_WIDTH)),
            vec(R_WIDTH), _const_spec((AAA_LORA, R_WIDTH)), _const_spec((GATE_LORA, R_WIDTH)),
            vec(R_WIDTH), vec(R_WIDTH), vec(R_WIDTH), _const_spec((R_WIDTH, R_WIDTH)),
        ],
        out_specs=[row] * 8,
        out_shape=[out] * 8,
        compiler_params=_cparams(("parallel", "parallel")),
        name="rwkv_prep",
    )(rw, rw, shift0, mu, w0, w2, a0, a2, g2, kkw, ka, rk, bd)


SCAN_CHUNK = R_HEAD_DIM
TILE = 4 * R_HEAD_DIM
SUB_BLOCK = 16
BATCH_UNROLL = 4


def _contract_last(a, b):
    return lax.dot_general(a, b, (((1,), (1,)), ((), ())), preferred_element_type=F32)


def _rwkv_scan_kernel(r_ref, lw_ref, k_ref, v_ref, al_ref, be_ref, g_ref, bonus_ref, s0_ref,
                      lg_ref, lb_ref, bd_ref,
                      o_ref, sout_ref,
                      p_ref, at_ref, bh_ref, u_ref, z_ref, y_ref, aab_ref, arb_ref, bk_ref, ec_ref,
                      coef_ref, *, nb, tt):
    j = pl.program_id(1)
    n = R_HEAD_DIM
    c = SCAN_CHUNK
    nt = R_WIDTH // TILE
    bd = bd_ref[...]
    bd_half = bd[0:TILE, 0:TILE]
    block_mask = (lax.broadcasted_iota(jnp.int32, (TILE, TILE), 0) // n
                  == lax.broadcasted_iota(jnp.int32, (TILE, TILE), 1) // n)
    t_id = lax.broadcasted_iota(jnp.int32, (c, TILE), 0)
    j_id = lax.broadcasted_iota(jnp.int32, (c, TILE), 1) % c
    strict = j_id < t_id
    incl = j_id <= t_id
    tri = (lax.broadcasted_iota(jnp.int32, (c, c), 1)
           <= lax.broadcasted_iota(jnp.int32, (c, c), 0)).astype(BF16)

    @pl.when(j == 0)
    def _():
        for b in range(nb):
            for i in range(nt):
                rows = []
                for hh in range(TILE // n):
                    parts = []
                    if hh:
                        parts.append(jnp.zeros((n, hh * n), F32))
                    parts.append(s0_ref[b, i * (TILE // n) + hh])
                    if TILE - (hh + 1) * n:
                        parts.append(jnp.zeros((n, TILE - (hh + 1) * n), F32))
                    rows.append(jnp.concatenate(parts, axis=-1))
                p_ref[b, i] = jnp.concatenate(rows, axis=0)

    def block_diag(x):
        return jnp.where(block_mask, jnp.concatenate([x] * (TILE // c), axis=0), 0.0).astype(BF16)

    def head_sums(x):
        xb = x.astype(BF16)
        return jnp.concatenate(
            [jnp.dot(xb[:, i * TILE:(i + 1) * TILE], bd_half, preferred_element_type=F32)
             for i in range(nt)], axis=-1)

    for ci in range(tt // c):
        rows = pl.ds(ci * c, c)

        def prepare(b, carry):
            lw = lw_ref[b, rows, :]
            hi = lw.astype(BF16)
            rem = lw - hi.astype(F32)
            mid = rem.astype(BF16)
            lo = (rem - mid.astype(F32)).astype(BF16)
            linc = (jnp.dot(tri, hi, preferred_element_type=F32)
                    + jnp.dot(tri, mid, preferred_element_type=F32)
                    + jnp.dot(tri, lo, preferred_element_type=F32))
            lc = linc[c - 1:c]
            e_neg = jnp.exp(-linc)
            e_rem = jnp.exp(lc - linc)
            al, be, r, k, v = (ref[b, rows, :] for ref in (al_ref, be_ref, r_ref, k_ref, v_ref))
            at = al * jnp.exp(linc - lw)
            rt = r * jnp.exp(linc)
            bh = be * e_neg
            kh = k * e_neg
            at_ref[b] = at
            bh_ref[b] = bh
            ec_ref[b] = jnp.exp(lc)
            bk_ref[b] = jnp.concatenate([be * e_rem, k * e_rem], axis=0).astype(BF16)
            for i in range(nt):
                ls = slice(i * TILE, (i + 1) * TILE)
                rt_b = rt[:, ls].astype(BF16)
                x = jnp.concatenate([at[:, ls].astype(BF16), rt_b], axis=0)
                a_k = _contract_last(x, block_diag(kh[:, ls]))
                a_b = _contract_last(x, block_diag(bh[:, ls]))
                x_p = _contract_last(x, p_ref[b, i].astype(BF16))
                w_v = block_diag(v[:, ls])
                u_ref[b, :, ls] = x_p[0:c] + jnp.dot(
                    jnp.where(strict, a_k[0:c], 0.0).astype(BF16), w_v, preferred_element_type=F32)
                y_ref[b, rows, ls] = x_p[c:2 * c] + jnp.dot(
                    jnp.where(incl, a_k[c:2 * c], 0.0).astype(BF16), w_v, preferred_element_type=F32)
                aab_ref[b, i] = jnp.where(strict, a_b[0:c], 0.0).astype(BF16)
                arb_ref[b, i] = jnp.where(incl, a_b[c:2 * c], 0.0).astype(BF16)
            return carry

        lax.fori_loop(0, nb, prepare, 0, unroll=BATCH_UNROLL)
        z_ref[...] = jnp.zeros(z_ref.shape, F32)

        for sb in range(c // SUB_BLOCK):
            srows = pl.ds(sb * SUB_BLOCK, SUB_BLOCK)

            def stage(b, carry):
                lhs = at_ref[b, srows, :][:, None, :] * bh_ref[b, srows, :][None, :, :]
                coef = head_sums(lhs.reshape(SUB_BLOCK * SUB_BLOCK, R_WIDTH))
                coef_ref[:, pl.ds(pl.multiple_of(b * SUB_BLOCK, SUB_BLOCK), SUB_BLOCK), :] = (
                    coef.reshape(SUB_BLOCK, SUB_BLOCK, R_WIDTH))
                if sb:
                    z = z_ref[b]
                    for i in range(nt):
                        ls = slice(i * TILE, (i + 1) * TILE)
                        u_ref[b, srows, ls] += jnp.dot(aab_ref[b, i, srows, :], block_diag(z[:, ls]),
                                                       preferred_element_type=F32)
                return carry

            lax.fori_loop(0, nb, stage, 0, unroll=BATCH_UNROLL)

            def substitute(t, zs):
                local = t - sb * SUB_BLOCK
                coef = coef_ref[local]
                here = lax.broadcasted_iota(jnp.int32, (SUB_BLOCK, R_WIDTH), 0) == local
                out = []
                for b in range(nb):
                    z = u_ref[b, pl.ds(t, 1), :] + jnp.sum(
                        coef[b * SUB_BLOCK:(b + 1) * SUB_BLOCK] * zs[b], axis=0, keepdims=True)
                    out.append(jnp.where(here, z, zs[b]))
                return tuple(out)

            zs = lax.fori_loop(sb * SUB_BLOCK, (sb + 1) * SUB_BLOCK, substitute,
                               tuple(jnp.zeros((SUB_BLOCK, R_WIDTH), F32) for _ in range(nb)))
            for b in range(nb):
                z_ref[b, srows, :] = zs[b]

        def finish(b, carry):
            z = z_ref[b]
            v = v_ref[b, rows, :]
            for i in range(nt):
                ls = slice(i * TILE, (i + 1) * TILE)
                y_ref[b, rows, ls] += jnp.dot(arb_ref[b, i], block_diag(z[:, ls]),
                                              preferred_element_type=F32)
                zv_t = jnp.concatenate([z[:, ls], v[:, ls]], axis=0).T.astype(BF16)
                upd = jnp.dot(zv_t, bk_ref[b, :, ls], preferred_element_type=F32)
                p_ref[b, i] = p_ref[b, i] * ec_ref[b, :, ls] + jnp.where(block_mask, upd, 0.0)
            return carry

        lax.fori_loop(0, nb, finish, 0, unroll=BATCH_UNROLL)

    inv_n = 1.0 / n
    for b in range(nb):
        y = y_ref[b]
        mu = _split_dot(y, bd) * inv_n
        d = y - mu
        var = _split_dot(d * d, bd) * inv_n
        yn = d * lax.rsqrt(var + LNX_EPS) * lg_ref[...] + lb_ref[...]
        o_ref[b] = ((yn + bonus_ref[b]) * g_ref[b]).astype(o_ref.dtype)

    @pl.when(j == pl.num_programs(1) - 1)
    def _():
        for b in range(nb):
            for i in range(nt):
                p = p_ref[b, i]
                for hh in range(TILE // n):
                    sout_ref[b, i * (TILE // n) + hh] = p[hh * n:(hh + 1) * n, hh * n:(hh + 1) * n]


def _rwkv_scan(r, lw, k, v, al, be, g, bonus, s0, lg, lb, bd, nb, tt):
    b, t, _ = r.shape
    c = SCAN_CHUNK
    row = pl.BlockSpec((nb, tt, R_WIDTH), lambda i, j: (i, j, 0))
    st = pl.BlockSpec((nb, R_HEADS, R_HEAD_DIM, R_HEAD_DIM), lambda i, j: (i, 0, 0, 0))
    nt = R_WIDTH // TILE
    return pl.pallas_call(
        functools.partial(_rwkv_scan_kernel, nb=nb, tt=tt),
        grid=(b // nb, t // tt),
        in_specs=[row] * 8 + [st, _const_spec((1, R_WIDTH)), _const_spec((1, R_WIDTH)),
                              _const_spec((R_WIDTH, R_WIDTH))],
        out_specs=[row, st],
        out_shape=[jax.ShapeDtypeStruct((b, t, R_WIDTH), BF16),
                   jax.ShapeDtypeStruct((b, R_HEADS, R_HEAD_DIM, R_HEAD_DIM), F32)],
        scratch_shapes=[pltpu.VMEM((nb, nt, TILE, TILE), F32),
                        pltpu.VMEM((nb, c, R_WIDTH), F32),
                        pltpu.VMEM((nb, c, R_WIDTH), F32),
                        pltpu.VMEM((nb, c, R_WIDTH), F32),
                        pltpu.VMEM((nb, c, R_WIDTH), F32),
                        pltpu.VMEM((nb, tt, R_WIDTH), F32),
                        pltpu.VMEM((nb, nt, c, TILE), BF16),
                        pltpu.VMEM((nb, nt, c, TILE), BF16),
                        pltpu.VMEM((nb, 2 * c, R_WIDTH), BF16),
                        pltpu.VMEM((nb, 1, R_WIDTH), F32),
                        pltpu.VMEM((SUB_BLOCK, nb * SUB_BLOCK, R_WIDTH), F32)],
        compiler_params=_cparams(("parallel", "arbitrary")),
        name="rwkv_scan",
    )(r, lw, k, v, al, be, g, bonus, s0, lg, lb, bd)


def _out_ffn_kernel(x_ref, oa_ref, orw_ref, mod_ref, ng_ref, wout_ref, wg_ref, wu_ref, wd_ref, o_ref):
    x = x_ref[0]
    mod = mod_ref[0]
    gt_a = mod[:, 2 * D_MODEL:3 * D_MODEL]
    sh_f = mod[:, 3 * D_MODEL:4 * D_MODEL]
    sc_f = mod[:, 4 * D_MODEL:5 * D_MODEL]
    gt_f = mod[:, 5 * D_MODEL:6 * D_MODEL]
    half = A_HEADS * V_DIM
    m = (jnp.dot(oa_ref[0], wout_ref[0:half, :], preferred_element_type=F32)
         + jnp.dot(orw_ref[0], wout_ref[half:D_MIX, :], preferred_element_type=F32))
    x = x + gt_a * _rms(m, ng_ref[1:2, :])
    h = (_rms(x, ng_ref[2:3, :]) * (1.0 + sc_f) + sh_f).astype(BF16)
    f = jnp.zeros_like(x)
    n_split = 2
    cw = D_FF // n_split
    for c in range(n_split):
        gate = jnp.dot(h, wg_ref[:, c * cw:(c + 1) * cw], preferred_element_type=F32)
        up = jnp.dot(h, wu_ref[:, c * cw:(c + 1) * cw], preferred_element_type=F32)
        act = (gate * _sigmoid(gate) * up).astype(BF16)
        f = f + jnp.dot(act, wd_ref[c * cw:(c + 1) * cw, :], preferred_element_type=F32)
    o_ref[0] = x + gt_f * _rms(f, ng_ref[3:4, :])


def _out_ffn(x, o_attn, o_rw, mod, ng, wout, wg, wu, wd, tm):
    b, t, _ = x.shape
    row = lambda w: pl.BlockSpec((1, tm, w), lambda i, j: (i, j, 0))
    single = dict(pipeline_mode=pl.Buffered(1))
    wspec = lambda shape: pl.BlockSpec(shape, lambda i, j: (0, 0), **single)
    return pl.pallas_call(
        _out_ffn_kernel,
        grid=(b, t // tm),
        in_specs=[
            row(D_MODEL), row(A_HEADS * V_DIM), row(R_WIDTH),
            pl.BlockSpec((1, 1, 6 * D_MODEL), lambda i, j: (i, 0, 0)),
            _const_spec((4, D_MODEL)),
            wspec((D_MIX, D_MODEL)), wspec((D_MODEL, D_FF)), wspec((D_MODEL, D_FF)),
            wspec((D_FF, D_MODEL)),
        ],
        out_specs=row(D_MODEL),
        out_shape=jax.ShapeDtypeStruct((b, t, D_MODEL), F32),
        compiler_params=_cparams(("parallel", "parallel")),
        name="out_ffn",
    )(x, o_attn, o_rw, mod, ng, wout, wg, wu, wd)


def _rope_tables(pos):
    inv = ROPE_THETA ** (-jnp.arange(0, ROPE_DIM, 2, dtype=F32) / ROPE_DIM)
    ang = pos.astype(F32)[:, None] * inv[None, :]
    cos, sin = jnp.cos(ang), jnp.sin(ang)
    cc = jnp.concatenate([cos, cos], axis=-1)
    ss = jnp.concatenate([-sin, sin], axis=-1)
    return cc, ss, jnp.tile(cc, (1, A_HEADS)).T, jnp.tile(ss, (1, A_HEADS)).T


def _pack_weights(w_in, w_uq, w_uk, w_uv):
    depth = w_in.shape[0]
    half = ROPE_DIM // 2
    o = Q_LORA + KV_LORA
    k_r = w_in[:, :, o:o + ROPE_DIM]
    k_rs = jnp.concatenate([k_r[..., half:], k_r[..., :half]], axis=-1)
    pad = jnp.zeros((depth, D_MODEL, LANE - 2 * ROPE_DIM), w_in.dtype)
    win = jnp.concatenate([w_in[:, :, :o], k_r, k_rs, pad, w_in[:, :, MLA_COLS:]], axis=-1)
    uq = w_uq.reshape(depth, Q_LORA, A_HEADS, NOPE_DIM + ROPE_DIM)
    nope = uq[..., :NOPE_DIM].reshape(depth, Q_LORA, A_HEADS * NOPE_DIM)
    rope = uq[..., NOPE_DIM:]
    rope_s = jnp.concatenate([rope[..., half:], rope[..., :half]], axis=-1)
    wuq = jnp.concatenate([nope, rope.reshape(depth, Q_LORA, -1), rope_s.reshape(depth, Q_LORA, -1)], axis=-1)
    wuq = jnp.transpose(wuq, (0, 2, 1))
    wuk = jnp.transpose(w_uk, (0, 2, 1, 3))
    wuv = jnp.transpose(w_uv, (0, 2, 3, 1))
    return win.astype(BF16), wuq.astype(BF16), wuk.astype(BF16), wuv.astype(BF16)


def _block_diag_ones():
    hid = jnp.arange(R_WIDTH) // R_HEAD_DIM
    bd = (hid[:, None] == hid[None, :]).astype(BF16)
    return bd


def _row_tile(t, pref):
    return pref if t % pref == 0 else t


def kernel(x_prompt, x_sample, c_prompt, c_sample, cache_ckv, cache_krope, state_wkv, state_shift,
           ada_w, ada_b, norm_g, w_in, q_norm_g, kv_norm_g, w_uq, w_uk, w_uv,
           rw_mu, rw_w0, rw_w2, rw_a0, rw_a2, rw_g2, rw_kk, rw_ka, rw_rk, lnx_g, lnx_b,
           w_out, w_gate, w_up, w_down):
    depth = w_in.shape[0]
    bp, tp, _ = x_prompt.shape
    bs, ts, _ = x_sample.shape
    past = cache_ckv.shape[2]

    mod_all = _modulation(jnp.concatenate([c_prompt, c_sample], axis=0), ada_w, ada_b)
    win, wuq, wuk, wuv = _pack_weights(w_in, w_uq, w_uk, w_uv)
    wout, wg, wu, wd = (w.astype(BF16) for w in (w_out, w_gate, w_up, w_down))
    w2, a2, g2 = (w.astype(BF16) for w in (rw_w2, rw_a2, rw_g2))
    bd = _block_diag_ones()
    tabs_p = _rope_tables(jnp.arange(tp))
    tabs_s = _rope_tables(past + jnp.arange(ts))
    vec = lambda a, l: a[l].reshape(1, -1)

    def layer(l, x, mod, tabs, s0, shift0, cache):
        b, t, _ = x.shape
        tm = _row_tile(t, 512)
        tq = _row_tile(t, COLS)
        ckv, krope, kcat, vt, qt, rw = _in_proj(
            x, mod, vec(norm_g[:, 0], l), win[l], vec(q_norm_g, l), vec(kv_norm_g, l),
            wuq[l], wuk[l], *tabs, tm, tq)
        if cache is None:
            o_attn = _prompt_attention(qt, kcat, vt, wuv[l], tq)
        else:
            o_attn = _sample_attention(qt, kcat, vt, cache[0], cache[1], l, wuv[l], min(2048, past))
        tt = _row_tile(t, 256)
        r, w, k, v, al, be, g, bonus = _rwkv_prep(
            rw, shift0, vec(rw_mu, l), vec(rw_w0, l), w2[l], vec(rw_a0, l), a2[l], g2[l],
            vec(rw_kk, l), vec(rw_ka, l), rw_rk[l].reshape(1, -1), bd, tt)
        ts_ = SCAN_CHUNK
        o_rw, s_new = _rwkv_scan(r, w, k, v, al, be, g, bonus, s0, vec(lnx_g, l), vec(lnx_b, l),
                                 bd, 8 if b % 8 == 0 else b, ts_)
        x_new = _out_ffn(x, o_attn, o_rw, mod, norm_g[l], wout[l], wg[l], wu[l], wd[l], tm)
        return x_new, ckv, krope, s_new, rw[:, -1]

    xp, xs = x_prompt, x_sample
    wkv_zero = jnp.zeros((bp, R_HEADS, R_HEAD_DIM, R_HEAD_DIM), F32)
    shift_zero = jnp.zeros((bp, 1, RW_COLS), F32)
    outs_p, outs_s = [], []
    for l in range(depth):
        mod_p = mod_all[l, :bp].reshape(bp, 1, 6 * D_MODEL)
        mod_s = mod_all[l, bp:].reshape(bs, 1, 6 * D_MODEL)
        xp, *rest_p = layer(l, xp, mod_p, tabs_p, wkv_zero, shift_zero, None)
        xs, *rest_s = layer(l, xs, mod_s, tabs_s, state_wkv[l],
                            state_shift[l].reshape(bs, 1, RW_COLS), (cache_ckv, cache_krope))
        outs_p.append(rest_p)
        outs_s.append(rest_s)
    stack = lambda outs, i: jnp.stack([o[i] for o in outs])
    return (xp, xs,
            stack(outs_p, 0), stack(outs_p, 1), stack(outs_p, 2), stack(outs_p, 3),
            stack(outs_s, 0), stack(outs_s, 1), stack(outs_s, 2), stack(outs_s, 3))
```

```python
import functools

import jax
import jax.numpy as jnp
from jax import lax
from jax.experimental import pallas as pl
from jax.experimental.pallas import tpu as pltpu

D_MODEL = 1024
CHUNK = 64
A_HEADS = 8
NOPE_DIM = 64
ROPE_DIM = 32
V_DIM = 64
Q_LORA = 384
KV_LORA = 256
ROPE_THETA = 10000.0
SM_SCALE = (NOPE_DIM + ROPE_DIM) ** -0.5
R_HEADS = 8
R_HEAD_DIM = 64
R_WIDTH = R_HEADS * R_HEAD_DIM
DECAY_LORA = 64
AAA_LORA = 64
GATE_LORA = 128
RW_COLS = 3 * R_WIDTH + DECAY_LORA + AAA_LORA + GATE_LORA
MLA_COLS = Q_LORA + KV_LORA + ROPE_DIM
D_MIX = A_HEADS * V_DIM + R_WIDTH
D_FF = 2816
RMS_EPS = 1e-6
LNX_EPS = 64e-5

LANE = 128
QK_WIDTH = KV_LORA + LANE
RW_OFF = Q_LORA + KV_LORA + LANE
IN_PACKED = RW_OFF + RW_COLS
VMEM_LIMIT = 56 * 1024 * 1024
FFN_ROWS = 512

BF16 = jnp.bfloat16
F32 = jnp.float32


def _cparams(sem):
    return pltpu.CompilerParams(dimension_semantics=sem, vmem_limit_bytes=VMEM_LIMIT)


def _const_spec(shape):
    n = len(shape)
    return pl.BlockSpec(shape, lambda *_: (0,) * n)


def _layer_spec(stacked, layer, **kwargs):
    shape = stacked.shape[1:]
    zeros = (0,) * len(shape)
    return pl.BlockSpec((None,) + shape, lambda *_: (layer,) + zeros, **kwargs)


def _bdot(a, b):
    return jnp.dot(a.astype(BF16), b.astype(BF16), preferred_element_type=F32)


def _split_dot(a, b):
    hi = a.astype(BF16)
    lo = (a - hi.astype(F32)).astype(BF16)
    return (jnp.dot(hi, b, preferred_element_type=F32)
            + jnp.dot(lo, b, preferred_element_type=F32))


def _rms(x, g):
    return x * lax.rsqrt(jnp.mean(x * x, axis=-1, keepdims=True) + RMS_EPS) * g


def _sigmoid(x):
    return 1.0 / (1.0 + jnp.exp(-x))


def _mod_kernel(c_ref, w_ref, b_ref, o_ref):
    c = c_ref[...]
    s = c * _sigmoid(c)
    o_ref[0] = _bdot(s, w_ref[0]) + b_ref[0]


def _modulation(c_all, ada_w, ada_b):
    depth = ada_w.shape[0]
    n = c_all.shape[0]
    tn = 1024
    return pl.pallas_call(
        _mod_kernel,
        grid=(depth, 6 * D_MODEL // tn),
        in_specs=[
            pl.BlockSpec((n, D_MODEL), lambda l, j: (0, 0)),
            pl.BlockSpec((1, D_MODEL, tn), lambda l, j: (l, 0, j)),
            pl.BlockSpec((1, 1, tn), lambda l, j: (l, 0, j)),
        ],
        out_specs=pl.BlockSpec((1, n, tn), lambda l, j: (l, 0, j)),
        out_shape=jax.ShapeDtypeStruct((depth, n, 6 * D_MODEL), F32),
        compiler_params=_cparams(("parallel", "parallel")),
        name="adaln_mod",
    )(c_all, ada_w, ada_b.reshape(depth, 1, 6 * D_MODEL))


def _in_proj_kernel(x_ref, mod_ref, g0_ref, win_ref, qg_ref, kvg_ref, wuq_ref, wuk_ref,
                    cck_ref, ssk_ref, ccq_ref, ssq_ref, shift_ref, *rest, tq):
    rw_params = rest[0:10]
    ckv_ref, krope_ref, kcat_ref, vt_ref, qt_ref = rest[10:15]
    feat_refs = rest[15:23]
    shift_out_ref, prev_ref = rest[23:25]
    x = x_ref[0]
    mod = mod_ref[0]
    sh = mod[:, 0:D_MODEL]
    sc = mod[:, D_MODEL:2 * D_MODEL]
    h = _rms(x, g0_ref[...]) * (1.0 + sc) + sh
    proj = _bdot(h, win_ref[...])
    c_q = proj[:, 0:Q_LORA]
    c_kv = proj[:, Q_LORA:Q_LORA + KV_LORA]
    k_r = proj[:, Q_LORA + KV_LORA:Q_LORA + KV_LORA + ROPE_DIM]
    k_rs = proj[:, Q_LORA + KV_LORA + ROPE_DIM:Q_LORA + KV_LORA + 2 * ROPE_DIM]
    tm = x.shape[0]
    rw = proj[:, RW_OFF:IN_PACKED]
    first = jnp.where(pl.program_id(1) == 0, shift_ref[0], prev_ref[...])
    for ref, val in zip(feat_refs, _rwkv_features(rw, first, *rw_params)):
        ref[0] = val
    prev_ref[...] = rw[tm - 1:tm]
    shift_out_ref[0] = rw[tm - 1:tm]

    ckv = _rms(c_kv, kvg_ref[...])
    krope = k_r * cck_ref[...] + k_rs * ssk_ref[...]
    ckv_ref[0] = ckv
    krope_ref[0] = krope
    zpad = jnp.zeros((tm, LANE - ROPE_DIM), F32)
    kcat_ref[0] = jnp.concatenate([ckv, krope, zpad], axis=-1).astype(BF16)
    vt_ref[0] = ckv.T.astype(BF16)

    cqn_t = _rms(c_q, qg_ref[...]).T.astype(BF16)
    q_t = jnp.dot(wuq_ref[...], cqn_t, preferred_element_type=F32)
    nope_w = A_HEADS * NOPE_DIM
    rope_w = A_HEADS * ROPE_DIM
    q_rope = (q_t[nope_w:nope_w + rope_w] * ccq_ref[...]
              + q_t[nope_w + rope_w:nope_w + 2 * rope_w] * ssq_ref[...])
    zrows = jnp.zeros((LANE - ROPE_DIM, tm), F32)
    for hd in range(A_HEADS):
        q_lat = jnp.dot(wuk_ref[hd], q_t[hd * NOPE_DIM:(hd + 1) * NOPE_DIM].astype(BF16),
                        preferred_element_type=F32)
        qr = q_rope[hd * ROPE_DIM:(hd + 1) * ROPE_DIM]
        qh = (jnp.concatenate([q_lat, qr, zrows], axis=0) * SM_SCALE).astype(BF16)
        for qb in range(tm // tq):
            qt_ref[0, qb, :, hd * tq:(hd + 1) * tq] = qh[:, qb * tq:(qb + 1) * tq]


def _in_proj(x, mod, g0, win, qg, kvg, wuq, wuk, cck, ssk, ccq, ssq, shift0, rw_params, layer, tm, tq):
    b, t, _ = x.shape
    grid = (b, t // tm)
    row = lambda w: pl.BlockSpec((1, tm, w), lambda i, j: (i, j, 0))
    tab = lambda w: pl.BlockSpec((tm, w), lambda i, j: (j, 0))
    tab_t = pl.BlockSpec((A_HEADS * ROPE_DIM, tm), lambda i, j: (0, j))
    nqb = tm // tq
    return pl.pallas_call(
        functools.partial(_in_proj_kernel, tq=tq),
        grid=grid,
        in_specs=[
            row(D_MODEL),
            pl.BlockSpec((1, 1, 6 * D_MODEL), lambda i, j: (i, 0, 0)),
            _const_spec((1, D_MODEL)),
            _layer_spec(win, layer),
            _const_spec((1, Q_LORA)),
            _const_spec((1, KV_LORA)),
            _layer_spec(wuq, layer),
            _layer_spec(wuk, layer),
            tab(ROPE_DIM), tab(ROPE_DIM), tab_t, tab_t,
            pl.BlockSpec((1, 1, RW_COLS), lambda i, j: (i, 0, 0)),
        ] + [_const_spec(p.shape) for p in rw_params],
        out_specs=[
            row(KV_LORA), row(ROPE_DIM), row(QK_WIDTH),
            pl.BlockSpec((1, KV_LORA, tm), lambda i, j: (i, 0, j)),
            pl.BlockSpec((1, nqb, QK_WIDTH, A_HEADS * tq), lambda i, j: (i, j, 0, 0)),
        ] + [row(R_WIDTH)] * 8 + [pl.BlockSpec((1, 1, RW_COLS), lambda i, j: (i, 0, 0))],
        out_shape=[
            jax.ShapeDtypeStruct((b, t, KV_LORA), F32),
            jax.ShapeDtypeStruct((b, t, ROPE_DIM), F32),
            jax.ShapeDtypeStruct((b, t, QK_WIDTH), BF16),
            jax.ShapeDtypeStruct((b, KV_LORA, t), BF16),
            jax.ShapeDtypeStruct((b, t // tq, QK_WIDTH, A_HEADS * tq), BF16),
        ] + [jax.ShapeDtypeStruct((b, t, R_WIDTH), F32)] * 8 + [jax.ShapeDtypeStruct((b, 1, RW_COLS), F32)],
        scratch_shapes=[pltpu.VMEM((1, RW_COLS), F32)],
        compiler_params=_cparams(("parallel", "arbitrary")),
        name="in_proj",
    )(x, mod, g0, win, qg, kvg, wuq, wuk, cck, ssk, ccq, ssq, shift0, *rw_params)


COLS = 2 * LANE
SCORE_LOOKAHEAD = 4
SAMPLE_SUB_KEYS = 512


def _softmax_update(s, vt, c, m_ref, l_ref, acc_ref):
    m_prev = m_ref[c]
    m_new = jnp.maximum(m_prev, jnp.max(s, axis=0, keepdims=True))
    alpha = jnp.exp(m_prev - m_new)
    p = jnp.exp(s - m_new)
    l_ref[c] = alpha * l_ref[c] + jnp.sum(p, axis=0, keepdims=True)
    acc_ref[c] = alpha * acc_ref[c] + jnp.dot(vt, p.astype(BF16), preferred_element_type=F32)
    m_ref[c] = m_new


def _pipelined_updates(units, m_ref, l_ref, acc_ref):
    n = len(units)
    pending = [units[u][0]() for u in range(min(SCORE_LOOKAHEAD, n))]
    for u in range(n):
        if u + SCORE_LOOKAHEAD < n:
            pending.append(units[u + SCORE_LOOKAHEAD][0]())
        _softmax_update(pending.pop(0), units[u][1], units[u][2], m_ref, l_ref, acc_ref)


def _pipelined_chunks(scores, vt, nc, m_ref, l_ref, acc_ref):
    _pipelined_updates([(functools.partial(scores, c), vt, c) for c in range(nc)], m_ref, l_ref, acc_ref)


def _attn_init(m_ref, l_ref, acc_ref):
    m_ref[...] = jnp.full(m_ref.shape, -jnp.inf, F32)
    l_ref[...] = jnp.zeros(l_ref.shape, F32)
    acc_ref[...] = jnp.zeros(acc_ref.shape, F32)


def _attn_finish(l_ref, acc_ref, wuv_ref, o_ref, tq):
    outs = []
    for hd in range(A_HEADS):
        c, off = divmod(hd * tq, COLS)
        o_lat = acc_ref[c, :, off:off + tq] / l_ref[c, :, off:off + tq]
        outs.append(jnp.dot(wuv_ref[hd], o_lat.astype(BF16), preferred_element_type=F32))
    o_ref[0] = jnp.concatenate(outs, axis=0).T.astype(o_ref.dtype)


def _attn_scratch(tq):
    nc = A_HEADS * tq // COLS
    return [pltpu.VMEM((nc, 1, COLS), F32), pltpu.VMEM((nc, 1, COLS), F32),
            pltpu.VMEM((nc, KV_LORA, COLS), F32)]


def _prompt_attn_kernel(q_ref, k_ref, vt_ref, wuv_ref, o_ref, m_ref, l_ref, acc_ref, *, tq):
    i = pl.program_id(1)
    nc = A_HEADS * tq // COLS
    _attn_init(m_ref, l_ref, acc_ref)

    def tile(j, bias):
        off = pl.multiple_of(j * tq, tq)
        k = k_ref[0, pl.ds(off, tq), :]
        vt = vt_ref[0, :, pl.ds(off, tq)]

        def scores(c):
            s = jnp.dot(k, q_ref[0, 0, :, c * COLS:(c + 1) * COLS], preferred_element_type=F32)
            return s if bias is None else s + bias

        _pipelined_chunks(scores, vt, nc, m_ref, l_ref, acc_ref)

    def full_tile(j, carry):
        tile(j, None)
        return carry

    lax.fori_loop(0, i, full_tile, 0)

    k_chunk = lax.broadcasted_iota(jnp.int32, (tq, COLS), 0) // CHUNK
    q_chunk = (lax.broadcasted_iota(jnp.int32, (tq, COLS), 1) % tq) // CHUNK
    tile(i, jnp.where(k_chunk <= q_chunk, 0.0, -jnp.inf).astype(F32))
    _attn_finish(l_ref, acc_ref, wuv_ref, o_ref, tq)


def _prompt_attention(qt, kcat, vt, wuv, layer, tq):
    b, t, _ = kcat.shape
    return pl.pallas_call(
        functools.partial(_prompt_attn_kernel, tq=tq),
        grid=(b, t // tq),
        in_specs=[
            pl.BlockSpec((1, 1, QK_WIDTH, A_HEADS * tq), lambda i, j: (i, j, 0, 0)),
            pl.BlockSpec((1, t, QK_WIDTH), lambda i, j: (i, 0, 0)),
            pl.BlockSpec((1, KV_LORA, t), lambda i, j: (i, 0, 0)),
            _layer_spec(wuv, layer),
        ],
        out_specs=pl.BlockSpec((1, tq, A_HEADS * V_DIM), lambda i, j: (i, j, 0)),
        out_shape=jax.ShapeDtypeStruct((b, t, A_HEADS * V_DIM), BF16),
        scratch_shapes=_attn_scratch(tq),
        compiler_params=_cparams(("parallel", "arbitrary")),
        name="prompt_attn",
    )(qt, kcat, vt, wuv)


def _sample_attn_kernel(q_ref, kn_ref, vtn_ref, cckv_ref, ckr_ref, wuv_ref, o_ref,
                        m_ref, l_ref, acc_ref, *, tq):
    j = pl.program_id(1)
    nc = A_HEADS * tq // COLS

    @pl.when(j == 0)
    def _():
        _attn_init(m_ref, l_ref, acc_ref)

    def cache_scores(ckv_b, kr, c):
        cols = slice(c * COLS, (c + 1) * COLS)
        return (jnp.dot(ckv_b, q_ref[0, 0, 0:KV_LORA, cols], preferred_element_type=F32)
                + jnp.dot(kr, q_ref[0, 0, KV_LORA:KV_LORA + ROPE_DIM, cols], preferred_element_type=F32))

    units = []
    for sub in range(cckv_ref.shape[2] // SAMPLE_SUB_KEYS):
        keys = slice(sub * SAMPLE_SUB_KEYS, (sub + 1) * SAMPLE_SUB_KEYS)
        ckv = cckv_ref[0, 0, keys, :]
        ckv_b = ckv.astype(BF16)
        vt = ckv.T.astype(BF16)
        kr = ckr_ref[0, 0, :, keys].T.astype(BF16)
        units += [(functools.partial(cache_scores, ckv_b, kr, c), vt, c) for c in range(nc)]
    _pipelined_updates(units, m_ref, l_ref, acc_ref)

    @pl.when(j == pl.num_programs(1) - 1)
    def _():
        kn = kn_ref[0]
        new_scores = lambda c: jnp.dot(kn, q_ref[0, 0, :, c * COLS:(c + 1) * COLS],
                                       preferred_element_type=F32)
        _pipelined_chunks(new_scores, vtn_ref[0], nc, m_ref, l_ref, acc_ref)
        _attn_finish(l_ref, acc_ref, wuv_ref, o_ref, tq)


def _sample_attention(qt, kcat, vt, cache_ckv, cache_krope_t, layer, wuv, tk):
    b, tq, _ = kcat.shape
    past = cache_ckv.shape[2]
    return pl.pallas_call(
        functools.partial(_sample_attn_kernel, tq=tq),
        grid=(b, past // tk),
        in_specs=[
            pl.BlockSpec((1, 1, QK_WIDTH, A_HEADS * tq), lambda i, j: (i, 0, 0, 0)),
            pl.BlockSpec((1, tq, QK_WIDTH), lambda i, j: (i, 0, 0)),
            pl.BlockSpec((1, KV_LORA, tq), lambda i, j: (i, 0, 0)),
            pl.BlockSpec((1, 1, tk, KV_LORA), lambda i, j: (layer, i, j, 0)),
            pl.BlockSpec((1, 1, ROPE_DIM, tk), lambda i, j: (layer, i, 0, j)),
            _layer_spec(wuv, layer),
        ],
        out_specs=pl.BlockSpec((1, tq, A_HEADS * V_DIM), lambda i, j: (i, 0, 0)),
        out_shape=jax.ShapeDtypeStruct((b, tq, A_HEADS * V_DIM), BF16),
        scratch_shapes=_attn_scratch(tq),
        compiler_params=_cparams(("parallel", "arbitrary")),
        name="sample_attn",
    )(qt, kcat, vt, cache_ckv, cache_krope_t, wuv)


def _rwkv_features(rw, first, mu_ref, w0_ref, w2_ref, a0_ref, a2_ref, g2_ref, kkw_ref, ka_ref, rk_ref, bd_ref):
    rows = rw.shape[0]
    rolled = pltpu.roll(rw, 1, 0)
    row_id = lax.broadcasted_iota(jnp.int32, (rows, 1), 0)
    rw_prev = jnp.where(row_id == 0, first, rolled)
    mix = rw + (rw_prev - rw) * mu_ref[...]
    r = mix[:, 0:R_WIDTH]
    k = mix[:, R_WIDTH:2 * R_WIDTH]
    v = mix[:, 2 * R_WIDTH:3 * R_WIDTH]
    o = 3 * R_WIDTH
    w_lo = mix[:, o:o + DECAY_LORA]
    a_lo = mix[:, o + DECAY_LORA:o + DECAY_LORA + AAA_LORA]
    g_lo = mix[:, o + DECAY_LORA + AAA_LORA:RW_COLS]
    w_raw = w0_ref[...] + _bdot(jnp.tanh(w_lo), w2_ref[...])
    nw = -w_raw
    softplus = jnp.maximum(nw, 0.0) + jnp.log(1.0 + jnp.exp(-jnp.abs(nw)))
    log_decay = -jnp.exp(-softplus - 0.5)
    a = _sigmoid(a0_ref[...] + _bdot(a_lo, a2_ref[...]))
    g = _bdot(_sigmoid(g_lo), g2_ref[...])
    bd = bd_ref[...]
    kk = k * kkw_ref[...]
    kk = kk / jnp.maximum(jnp.sqrt(_split_dot(kk * kk, bd)), 1e-12)
    k2 = k * (1.0 + (a - 1.0) * ka_ref[...])
    bonus = _split_dot(r * k2 * rk_ref[...], bd) * v
    return r, log_decay, k2, v, -kk, kk * a, g, bonus


SCAN_CHUNK = R_HEAD_DIM
TILE = 4 * R_HEAD_DIM
SUB_BLOCK = 16
BATCH_UNROLL = 4


def _contract_last(a, b):
    return lax.dot_general(a, b, (((1,), (1,)), ((), ())), preferred_element_type=F32)


def _rwkv_scan_kernel(r_ref, lw_ref, k_ref, v_ref, al_ref, be_ref, g_ref, bonus_ref, s0_ref,
                      lg_ref, lb_ref, bd_ref,
                      o_ref, sout_ref,
                      p_ref, at_ref, bh_ref, u_ref, z_ref, y_ref, aab_ref, arb_ref, bk_ref, ec_ref,
                      coef_ref, *us_refs, nb, tt):
    j = pl.program_id(1)
    n = R_HEAD_DIM
    c = SCAN_CHUNK
    nt = R_WIDTH // TILE
    bd = bd_ref[...]
    bd_half = bd[0:TILE, 0:TILE]
    block_mask = (lax.broadcasted_iota(jnp.int32, (TILE, TILE), 0) // n
                  == lax.broadcasted_iota(jnp.int32, (TILE, TILE), 1) // n)
    t_id = lax.broadcasted_iota(jnp.int32, (c, TILE), 0)
    j_id = lax.broadcasted_iota(jnp.int32, (c, TILE), 1) % c
    strict = j_id < t_id
    incl = j_id <= t_id
    later = (lax.broadcasted_iota(jnp.int32, (SUB_BLOCK, SUB_BLOCK, R_WIDTH), 1)
             > lax.broadcasted_iota(jnp.int32, (SUB_BLOCK, SUB_BLOCK, R_WIDTH), 0))
    tri = (lax.broadcasted_iota(jnp.int32, (c, c), 1)
           <= lax.broadcasted_iota(jnp.int32, (c, c), 0)).astype(BF16)

    @pl.when(j == 0)
    def _():
        for b in range(nb):
            for i in range(nt):
                rows = []
                for hh in range(TILE // n):
                    parts = []
                    if hh:
                        parts.append(jnp.zeros((n, hh * n), F32))
                    parts.append(s0_ref[b, i * (TILE // n) + hh])
                    if TILE - (hh + 1) * n:
                        parts.append(jnp.zeros((n, TILE - (hh + 1) * n), F32))
                    rows.append(jnp.concatenate(parts, axis=-1))
                p_ref[b, i] = jnp.concatenate(rows, axis=0)

    def block_diag(x):
        return jnp.where(block_mask, jnp.concatenate([x] * (TILE // c), axis=0), 0.0).astype(BF16)

    def head_sums(x):
        xb = x.astype(BF16)
        return jnp.concatenate(
            [jnp.dot(xb[:, i * TILE:(i + 1) * TILE], bd_half, preferred_element_type=F32)
             for i in range(nt)], axis=-1)

    for ci in range(tt // c):
        rows = pl.ds(ci * c, c)

        def prepare(b, carry):
            lw = lw_ref[b, rows, :]
            hi = lw.astype(BF16)
            rem = lw - hi.astype(F32)
            mid = rem.astype(BF16)
            lo = (rem - mid.astype(F32)).astype(BF16)
            linc = (jnp.dot(tri, hi, preferred_element_type=F32)
                    + jnp.dot(tri, mid, preferred_element_type=F32)
                    + jnp.dot(tri, lo, preferred_element_type=F32))
            lc = linc[c - 1:c]
            e_neg = jnp.exp(-linc)
            e_rem = jnp.exp(lc - linc)
            al, be, r, k, v = (ref[b, rows, :] for ref in (al_ref, be_ref, r_ref, k_ref, v_ref))
            at = al * jnp.exp(linc - lw)
            rt = r * jnp.exp(linc)
            bh = be * e_neg
            kh = k * e_neg
            at_ref[b] = at
            bh_ref[b] = bh
            ec_ref[b] = jnp.exp(lc)
            bk_ref[b] = jnp.concatenate([be * e_rem, k * e_rem], axis=0).astype(BF16)
            for i in range(nt):
                ls = slice(i * TILE, (i + 1) * TILE)
                rt_b = rt[:, ls].astype(BF16)
                x = jnp.concatenate([at[:, ls].astype(BF16), rt_b], axis=0)
                a_k = _contract_last(x, block_diag(kh[:, ls]))
                a_b = _contract_last(x, block_diag(bh[:, ls]))
                x_p = _contract_last(x, p_ref[b, i].astype(BF16))
                w_v = block_diag(v[:, ls])
                u_ref[b, :, ls] = x_p[0:c] + jnp.dot(
                    jnp.where(strict, a_k[0:c], 0.0).astype(BF16), w_v, preferred_element_type=F32)
                y_ref[b, rows, ls] = x_p[c:2 * c] + jnp.dot(
                    jnp.where(incl, a_k[c:2 * c], 0.0).astype(BF16), w_v, preferred_element_type=F32)
                aab_ref[b, i] = jnp.where(strict, a_b[0:c], 0.0).astype(BF16)
                arb_ref[b, i] = jnp.where(incl, a_b[c:2 * c], 0.0).astype(BF16)
            return carry

        lax.fori_loop(0, nb, prepare, 0, unroll=BATCH_UNROLL)
        z_ref[...] = jnp.zeros(z_ref.shape, F32)

        for sb in range(c // SUB_BLOCK):
            srows = pl.ds(sb * SUB_BLOCK, SUB_BLOCK)

            def stage(b, carry):
                lhs = bh_ref[b, srows, :][:, None, :] * at_ref[b, srows, :][None, :, :]
                coef = head_sums(lhs.reshape(SUB_BLOCK * SUB_BLOCK, R_WIDTH))
                coef_ref[:, pl.ds(pl.multiple_of(b * SUB_BLOCK, SUB_BLOCK), SUB_BLOCK), :] = jnp.where(
                    later, coef.reshape(SUB_BLOCK, SUB_BLOCK, R_WIDTH), 0.0)
                if sb:
                    z = z_ref[b]
                    for i in range(nt):
                        ls = slice(i * TILE, (i + 1) * TILE)
                        u_ref[b, srows, ls] += jnp.dot(aab_ref[b, i, srows, :], block_diag(z[:, ls]),
                                                       preferred_element_type=F32)
                return carry

            lax.fori_loop(0, nb, stage, 0, unroll=BATCH_UNROLL)

            for b in range(nb):
                us_refs[b][...] = u_ref[b, srows, :]

            def substitute(j, carry):
                coef = coef_ref[j]
                for b in range(nb):
                    z_j = us_refs[b][pl.ds(j, 1), :]
                    us_refs[b][...] += coef[b * SUB_BLOCK:(b + 1) * SUB_BLOCK] * z_j
                return carry

            lax.fori_loop(0, SUB_BLOCK - 1, substitute, 0)
            for b in range(nb):
                z_ref[b, srows, :] = us_refs[b][...]

        def finish(b, carry):
            z = z_ref[b]
            v = v_ref[b, rows, :]
            for i in range(nt):
                ls = slice(i * TILE, (i + 1) * TILE)
                y_ref[b, rows, ls] += jnp.dot(arb_ref[b, i], block_diag(z[:, ls]),
                                              preferred_element_type=F32)
                zv_t = jnp.concatenate([z[:, ls], v[:, ls]], axis=0).T.astype(BF16)
                upd = jnp.dot(zv_t, bk_ref[b, :, ls], preferred_element_type=F32)
                p_ref[b, i] = p_ref[b, i] * ec_ref[b, :, ls] + jnp.where(block_mask, upd, 0.0)
            return carry

        lax.fori_loop(0, nb, finish, 0, unroll=BATCH_UNROLL)

    def head_means(x):
        return jnp.concatenate(
            [_split_dot(x[:, i * TILE:(i + 1) * TILE], bd_half) for i in range(nt)], axis=-1) * (1.0 / n)

    y = y_ref[...].reshape(nb * tt, R_WIDTH)
    d = y - head_means(y)
    yn = d * lax.rsqrt(head_means(d * d) + LNX_EPS) * lg_ref[...] + lb_ref[...]
    out = (yn + bonus_ref[...].reshape(nb * tt, R_WIDTH)) * g_ref[...].reshape(nb * tt, R_WIDTH)
    o_ref[...] = out.reshape(nb, tt, R_WIDTH).astype(o_ref.dtype)

    @pl.when(j == pl.num_programs(1) - 1)
    def _():
        for b in range(nb):
            for i in range(nt):
                p = p_ref[b, i]
                for hh in range(TILE // n):
                    sout_ref[b, i * (TILE // n) + hh] = p[hh * n:(hh + 1) * n, hh * n:(hh + 1) * n]


def _rwkv_scan(r, lw, k, v, al, be, g, bonus, s0, lg, lb, bd, nb, tt):
    b, t, _ = r.shape
    c = SCAN_CHUNK
    row = pl.BlockSpec((nb, tt, R_WIDTH), lambda i, j: (i, j, 0))
    st = pl.BlockSpec((nb, R_HEADS, R_HEAD_DIM, R_HEAD_DIM), lambda i, j: (i, 0, 0, 0))
    nt = R_WIDTH // TILE
    return pl.pallas_call(
        functools.partial(_rwkv_scan_kernel, nb=nb, tt=tt),
        grid=(b // nb, t // tt),
        in_specs=[row] * 8 + [st, _const_spec((1, R_WIDTH)), _const_spec((1, R_WIDTH)),
                              _const_spec((R_WIDTH, R_WIDTH))],
        out_specs=[row, st],
        out_shape=[jax.ShapeDtypeStruct((b, t, R_WIDTH), BF16),
                   jax.ShapeDtypeStruct((b, R_HEADS, R_HEAD_DIM, R_HEAD_DIM), F32)],
        scratch_shapes=[pltpu.VMEM((nb, nt, TILE, TILE), F32),
                        pltpu.VMEM((nb, c, R_WIDTH), F32),
                        pltpu.VMEM((nb, c, R_WIDTH), F32),
                        pltpu.VMEM((nb, c, R_WIDTH), F32),
                        pltpu.VMEM((nb, c, R_WIDTH), F32),
                        pltpu.VMEM((nb, tt, R_WIDTH), F32),
                        pltpu.VMEM((nb, nt, c, TILE), BF16),
                        pltpu.VMEM((nb, nt, c, TILE), BF16),
                        pltpu.VMEM((nb, 2 * c, R_WIDTH), BF16),
                        pltpu.VMEM((nb, 1, R_WIDTH), F32),
                        pltpu.VMEM((SUB_BLOCK, nb * SUB_BLOCK, R_WIDTH), F32)]
                       + [pltpu.VMEM((SUB_BLOCK, R_WIDTH), F32)] * nb,
        compiler_params=_cparams(("parallel", "arbitrary")),
        name="rwkv_scan",
    )(r, lw, k, v, al, be, g, bonus, s0, lg, lb, bd)


def _out_ffn_kernel(x_ref, oa_ref, orw_ref, mod_ref, ng_ref, wout_ref, wg_ref, wu_ref, wd_ref, o_ref):
    gb, tr, _ = x_ref.shape
    rows = gb * tr
    x = x_ref[...]
    mod = mod_ref[...]
    gt_a = mod[:, :, 2 * D_MODEL:3 * D_MODEL]
    sh_f = mod[:, :, 3 * D_MODEL:4 * D_MODEL]
    sc_f = mod[:, :, 4 * D_MODEL:5 * D_MODEL]
    gt_f = mod[:, :, 5 * D_MODEL:6 * D_MODEL]
    half = A_HEADS * V_DIM
    m = (jnp.dot(oa_ref[...].reshape(rows, half), wout_ref[0:half, :], preferred_element_type=F32)
         + jnp.dot(orw_ref[...].reshape(rows, R_WIDTH), wout_ref[half:D_MIX, :],
                   preferred_element_type=F32))
    x = x + gt_a * _rms(m.reshape(gb, tr, D_MODEL), ng_ref[1:2, :])
    h = (_rms(x, ng_ref[2:3, :]) * (1.0 + sc_f) + sh_f).reshape(rows, D_MODEL).astype(BF16)
    f = jnp.zeros((rows, D_MODEL), F32)
    n_split = 2
    cw = D_FF // n_split
    for c in range(n_split):
        gate = jnp.dot(h, wg_ref[:, c * cw:(c + 1) * cw], preferred_element_type=F32)
        up = jnp.dot(h, wu_ref[:, c * cw:(c + 1) * cw], preferred_element_type=F32)
        act = (gate * _sigmoid(gate) * up).astype(BF16)
        f = f + jnp.dot(act, wd_ref[c * cw:(c + 1) * cw, :], preferred_element_type=F32)
    o_ref[...] = x + gt_f * _rms(f.reshape(gb, tr, D_MODEL), ng_ref[3:4, :])


def _out_ffn(x, o_attn, o_rw, mod, ng, wout, wg, wu, wd, layer, gb, tr):
    b, t, _ = x.shape
    row = lambda w: pl.BlockSpec((gb, tr, w), lambda i, j: (i, j, 0))
    wspec = lambda w: _layer_spec(w, layer, pipeline_mode=pl.Buffered(1))
    return pl.pallas_call(
        _out_ffn_kernel,
        grid=(b // gb, t // tr),
        in_specs=[
            row(D_MODEL), row(A_HEADS * V_DIM), row(R_WIDTH),
            pl.BlockSpec((gb, 1, 6 * D_MODEL), lambda i, j: (i, 0, 0)),
            _const_spec((4, D_MODEL)),
            wspec(wout), wspec(wg), wspec(wu), wspec(wd),
        ],
        out_specs=row(D_MODEL),
        out_shape=jax.ShapeDtypeStruct((b, t, D_MODEL), F32),
        compiler_params=_cparams(("parallel", "parallel")),
        name="out_ffn",
    )(x, o_attn, o_rw, mod, ng, wout, wg, wu, wd)


def _rope_tables(pos):
    inv = ROPE_THETA ** (-jnp.arange(0, ROPE_DIM, 2, dtype=F32) / ROPE_DIM)
    ang = pos.astype(F32)[:, None] * inv[None, :]
    cos, sin = jnp.cos(ang), jnp.sin(ang)
    cc = jnp.concatenate([cos, cos], axis=-1)
    ss = jnp.concatenate([-sin, sin], axis=-1)
    return cc, ss, jnp.tile(cc, (1, A_HEADS)).T, jnp.tile(ss, (1, A_HEADS)).T


def _pack_weights(w_in, w_uq, w_uk, w_uv):
    depth = w_in.shape[0]
    half = ROPE_DIM // 2
    o = Q_LORA + KV_LORA
    k_r = w_in[:, :, o:o + ROPE_DIM]
    k_rs = jnp.concatenate([k_r[..., half:], k_r[..., :half]], axis=-1)
    pad = jnp.zeros((depth, D_MODEL, LANE - 2 * ROPE_DIM), w_in.dtype)
    win = jnp.concatenate([w_in[:, :, :o], k_r, k_rs, pad, w_in[:, :, MLA_COLS:]], axis=-1)
    uq = w_uq.reshape(depth, Q_LORA, A_HEADS, NOPE_DIM + ROPE_DIM)
    nope = uq[..., :NOPE_DIM].reshape(depth, Q_LORA, A_HEADS * NOPE_DIM)
    rope = uq[..., NOPE_DIM:]
    rope_s = jnp.concatenate([rope[..., half:], rope[..., :half]], axis=-1)
    wuq = jnp.concatenate([nope, rope.reshape(depth, Q_LORA, -1), rope_s.reshape(depth, Q_LORA, -1)], axis=-1)
    wuq = jnp.transpose(wuq, (0, 2, 1))
    wuk = jnp.transpose(w_uk, (0, 2, 1, 3))
    wuv = jnp.transpose(w_uv, (0, 2, 3, 1))
    return win.astype(BF16), wuq.astype(BF16), wuk.astype(BF16), wuv.astype(BF16)


def _block_diag_ones():
    hid = jnp.arange(R_WIDTH) // R_HEAD_DIM
    bd = (hid[:, None] == hid[None, :]).astype(BF16)
    return bd


def _row_tile(t, pref):
    return pref if t % pref == 0 else t


def kernel(x_prompt, x_sample, c_prompt, c_sample, cache_ckv, cache_krope, state_wkv, state_shift,
           ada_w, ada_b, norm_g, w_in, q_norm_g, kv_norm_g, w_uq, w_uk, w_uv,
           rw_mu, rw_w0, rw_w2, rw_a0, rw_a2, rw_g2, rw_kk, rw_ka, rw_rk, lnx_g, lnx_b,
           w_out, w_gate, w_up, w_down):
    depth = w_in.shape[0]
    bp, tp, _ = x_prompt.shape
    bs, ts, _ = x_sample.shape
    past = cache_ckv.shape[2]

    mod_all = _modulation(jnp.concatenate([c_prompt, c_sample], axis=0), ada_w, ada_b)
    win, wuq, wuk, wuv = _pack_weights(w_in, w_uq, w_uk, w_uv)
    wout, wg, wu, wd = (w.astype(BF16) for w in (w_out, w_gate, w_up, w_down))
    w2, a2, g2 = (w.astype(BF16) for w in (rw_w2, rw_a2, rw_g2))
    bd = _block_diag_ones()
    tabs_p = _rope_tables(jnp.arange(tp))
    tabs_s = _rope_tables(past + jnp.arange(ts))
    cache_krope_t = jnp.swapaxes(cache_krope, 2, 3)
    vec = lambda a, l: a[l].reshape(1, -1)

    def layer(l, x, mod, tabs, s0, shift0, cache):
        b, t, _ = x.shape
        tm = _row_tile(t, 512)
        tq = _row_tile(t, COLS)
        rw_params = (vec(rw_mu, l), vec(rw_w0, l), w2[l], vec(rw_a0, l), a2[l], g2[l],
                     vec(rw_kk, l), vec(rw_ka, l), rw_rk[l].reshape(1, -1), bd)
        ckv, krope, kcat, vt, qt, r, w, k, v, al, be, g, bonus, shift_new = _in_proj(
            x, mod, vec(norm_g[:, 0], l), win, vec(q_norm_g, l), vec(kv_norm_g, l),
            wuq, wuk, *tabs, shift0, rw_params, l, tm, tq)
        if cache is None:
            o_attn = _prompt_attention(qt, kcat, vt, wuv, l, tq)
        else:
            o_attn = _sample_attention(qt, kcat, vt, cache[0], cache[1], l, wuv, min(2048, past))
        ts_ = SCAN_CHUNK
        o_rw, s_new = _rwkv_scan(r, w, k, v, al, be, g, bonus, s0, vec(lnx_g, l), vec(lnx_b, l),
                                 bd, 8 if b % 8 == 0 else b, ts_)
        gb = max(1, min(b, FFN_ROWS // t))
        x_new = _out_ffn(x, o_attn, o_rw, mod, norm_g[l], wout, wg, wu, wd, l,
                         gb if b % gb == 0 else 1, min(t, FFN_ROWS))
        return x_new, ckv, krope, s_new, shift_new[:, 0]

    xp, xs = x_prompt, x_sample
    wkv_zero = jnp.zeros((bp, R_HEADS, R_HEAD_DIM, R_HEAD_DIM), F32)
    shift_zero = jnp.zeros((bp, 1, RW_COLS), F32)
    outs_p, outs_s = [], []
    for l in range(depth):
        mod_p = mod_all[l, :bp].reshape(bp, 1, 6 * D_MODEL)
        mod_s = mod_all[l, bp:].reshape(bs, 1, 6 * D_MODEL)
        xp, *rest_p = layer(l, xp, mod_p, tabs_p, wkv_zero, shift_zero, None)
        xs, *rest_s = layer(l, xs, mod_s, tabs_s, state_wkv[l],
                            state_shift[l].reshape(bs, 1, RW_COLS), (cache_ckv, cache_krope_t))
        outs_p.append(rest_p)
        outs_s.append(rest_s)
    stack = lambda outs, i: jnp.stack([o[i] for o in outs])
    return (xp, xs,
            stack(outs_p, 0), stack(outs_p, 1), stack(outs_p, 2), stack(outs_p, 3),
            stack(outs_s, 0), stack(outs_s, 1), stack(outs_s, 2), stack(outs_s, 3))
```

```python
import functools

import jax
import jax.numpy as jnp
from jax import lax
from jax.experimental import pallas as pl
from jax.experimental.pallas import tpu as pltpu

D_MODEL = 1024
CHUNK = 64
A_HEADS = 8
NOPE_DIM = 64
ROPE_DIM = 32
V_DIM = 64
Q_LORA = 384
KV_LORA = 256
ROPE_THETA = 10000.0
SM_SCALE = (NOPE_DIM + ROPE_DIM) ** -0.5
Q_SCALE = SM_SCALE * 1.4426950408889634
R_HEADS = 8
R_HEAD_DIM = 64
R_WIDTH = R_HEADS * R_HEAD_DIM
DECAY_LORA = 64
AAA_LORA = 64
GATE_LORA = 128
RW_COLS = 3 * R_WIDTH + DECAY_LORA + AAA_LORA + GATE_LORA
MLA_COLS = Q_LORA + KV_LORA + ROPE_DIM
D_MIX = A_HEADS * V_DIM + R_WIDTH
D_FF = 2816
RMS_EPS = 1e-6
LNX_EPS = 64e-5

LANE = 128
QK_WIDTH = KV_LORA + LANE
RW_OFF = Q_LORA + KV_LORA + LANE
IN_PACKED = RW_OFF + RW_COLS
VMEM_LIMIT = 56 * 1024 * 1024
IN_ROWS = 512
FFN_ROWS = 512
FFN_CHUNK = 256
FFN_LOOKAHEAD = 2

BF16 = jnp.bfloat16
F32 = jnp.float32


def _cparams(sem):
    return pltpu.CompilerParams(dimension_semantics=sem, vmem_limit_bytes=VMEM_LIMIT)


def _const_spec(shape):
    n = len(shape)
    return pl.BlockSpec(shape, lambda *_: (0,) * n)


def _layer_spec(stacked, layer, **kwargs):
    shape = stacked.shape[1:]
    zeros = (0,) * len(shape)
    return pl.BlockSpec((None,) + shape, lambda *_: (layer,) + zeros, **kwargs)


def _bdot(a, b):
    return jnp.dot(a.astype(BF16), b.astype(BF16), preferred_element_type=F32)


def _split_dot(a, b):
    hi = a.astype(BF16)
    lo = (a - hi.astype(F32)).astype(BF16)
    return (jnp.dot(hi, b, preferred_element_type=F32)
            + jnp.dot(lo, b, preferred_element_type=F32))


def _rms(x, g):
    return x * lax.rsqrt(jnp.mean(x * x, axis=-1, keepdims=True) + RMS_EPS) * g


def _sigmoid(x):
    return 1.0 / (1.0 + jnp.exp(-x))


def _mod_kernel(c_ref, w_ref, b_ref, o_ref):
    c = c_ref[...]
    s = c * _sigmoid(c)
    o_ref[0] = _bdot(s, w_ref[0]) + b_ref[0]


def _modulation(c_all, ada_w, ada_b):
    depth = ada_w.shape[0]
    n = c_all.shape[0]
    tn = 1024
    return pl.pallas_call(
        _mod_kernel,
        grid=(depth, 6 * D_MODEL // tn),
        in_specs=[
            pl.BlockSpec((n, D_MODEL), lambda l, j: (0, 0)),
            pl.BlockSpec((1, D_MODEL, tn), lambda l, j: (l, 0, j)),
            pl.BlockSpec((1, 1, tn), lambda l, j: (l, 0, j)),
        ],
        out_specs=pl.BlockSpec((1, n, tn), lambda l, j: (l, 0, j)),
        out_shape=jax.ShapeDtypeStruct((depth, n, 6 * D_MODEL), F32),
        compiler_params=_cparams(("parallel", "parallel")),
        name="adaln_mod",
    )(c_all, ada_w, ada_b.reshape(depth, 1, 6 * D_MODEL))


def _in_proj_kernel(x_ref, mod_ref, g0_ref, win_ref, qg_ref, kvg_ref, wuq_ref, wuk_ref,
                    cck_ref, ssk_ref, ccq_ref, ssq_ref, shift_ref, *rest, tq):
    rw_params = rest[0:10]
    ckv_ref, krope_ref, kcat_ref, vt_ref, qt_ref = rest[10:15]
    feat_refs = rest[15:23]
    shift_out_ref, prev_ref = rest[23:25]
    gb, tr, _ = x_ref.shape
    tm = gb * tr
    mod = mod_ref[...]
    sh = mod[:, :, 0:D_MODEL]
    sc = mod[:, :, D_MODEL:2 * D_MODEL]
    h = (_rms(x_ref[...], g0_ref[...]) * (1.0 + sc) + sh).reshape(tm, D_MODEL).astype(BF16)
    proj = jnp.dot(h, win_ref[:, 0:RW_OFF], preferred_element_type=F32)
    rw = jnp.dot(h, win_ref[:, RW_OFF:IN_PACKED], preferred_element_type=F32)
    c_q = proj[:, 0:Q_LORA]
    c_kv = proj[:, Q_LORA:Q_LORA + KV_LORA]
    k_r = proj[:, Q_LORA + KV_LORA:Q_LORA + KV_LORA + ROPE_DIM]
    k_rs = proj[:, Q_LORA + KV_LORA + ROPE_DIM:Q_LORA + KV_LORA + 2 * ROPE_DIM]

    ckv = _rms(c_kv, kvg_ref[...])
    krope = k_r * cck_ref[...] + k_rs * ssk_ref[...]
    ckv_ref[...] = ckv.reshape(gb, tr, KV_LORA)
    krope_ref[...] = krope.reshape(gb, tr, ROPE_DIM)
    zpad = jnp.zeros((tm, LANE - ROPE_DIM), F32)
    kcat_ref[...] = jnp.concatenate([ckv, krope, zpad], axis=-1).astype(BF16).reshape(gb, tr, QK_WIDTH)
    ckv_t = ckv.T.astype(BF16)
    for s in range(gb):
        vt_ref[s] = ckv_t[:, s * tr:(s + 1) * tr]

    cqn_t = _rms(c_q, qg_ref[...]).T.astype(BF16)
    q_t = jnp.dot(wuq_ref[...], cqn_t, preferred_element_type=F32)
    nope_w = A_HEADS * NOPE_DIM
    rope_w = A_HEADS * ROPE_DIM
    q_rope = (q_t[nope_w:nope_w + rope_w] * ccq_ref[...]
              + q_t[nope_w + rope_w:nope_w + 2 * rope_w] * ssq_ref[...])
    zrows = jnp.zeros((LANE - ROPE_DIM, tm), F32)
    for hd in range(A_HEADS):
        q_lat = jnp.dot(wuk_ref[hd], q_t[hd * NOPE_DIM:(hd + 1) * NOPE_DIM].astype(BF16),
                        preferred_element_type=F32)
        qr = q_rope[hd * ROPE_DIM:(hd + 1) * ROPE_DIM]
        qh = (jnp.concatenate([q_lat, qr, zrows], axis=0) * Q_SCALE).astype(BF16)
        nqb = tr // tq
        for s in range(gb):
            for qb in range(nqb):
                col = (s * nqb + qb) * tq
                qt_ref[s, qb, :, hd * tq:(hd + 1) * tq] = qh[:, col:col + tq]

    first = jnp.where(pl.program_id(1) == 0, shift_ref[...], prev_ref[...][None])
    rw3 = rw.reshape(gb, tr, RW_COLS)
    row_id = lax.broadcasted_iota(jnp.int32, (gb, tr, 1), 1)
    rw_prev = jnp.where(row_id == 0, first, pltpu.roll(rw3, 1, 1)).reshape(tm, RW_COLS)
    for ref, val in zip(feat_refs, _rwkv_features(rw, rw_prev, *rw_params)):
        ref[...] = val.reshape(gb, tr, R_WIDTH)
    prev_ref[...] = rw[tm - 1:tm]
    shift_out_ref[...] = rw3[:, tr - 1:tr, :]


def _in_proj(x, mod, g0, win, qg, kvg, wuq, wuk, cck, ssk, ccq, ssq, shift0, rw_params, layer, gb, tr, tq):
    b, t, _ = x.shape
    tm = gb * tr
    grid = (b // gb, t // tr)
    row = lambda w: pl.BlockSpec((gb, tr, w), lambda i, j: (i, j, 0))
    tab = lambda w: pl.BlockSpec((tm, w), lambda i, j: (j, 0))
    tab_t = pl.BlockSpec((A_HEADS * ROPE_DIM, tm), lambda i, j: (0, j))
    nqb = tr // tq
    return pl.pallas_call(
        functools.partial(_in_proj_kernel, tq=tq),
        grid=grid,
        in_specs=[
            row(D_MODEL),
            pl.BlockSpec((gb, 1, 6 * D_MODEL), lambda i, j: (i, 0, 0)),
            _const_spec((1, D_MODEL)),
            _layer_spec(win, layer),
            _const_spec((1, Q_LORA)),
            _const_spec((1, KV_LORA)),
            _layer_spec(wuq, layer),
            _layer_spec(wuk, layer),
            tab(ROPE_DIM), tab(ROPE_DIM), tab_t, tab_t,
            pl.BlockSpec((gb, 1, RW_COLS), lambda i, j: (i, 0, 0)),
        ] + [_const_spec(p.shape) for p in rw_params],
        out_specs=[
            row(KV_LORA), row(ROPE_DIM), row(QK_WIDTH),
            pl.BlockSpec((gb, KV_LORA, tr), lambda i, j: (i, 0, j)),
            pl.BlockSpec((gb, nqb, QK_WIDTH, A_HEADS * tq), lambda i, j: (i, j, 0, 0)),
        ] + [row(R_WIDTH)] * 8 + [pl.BlockSpec((gb, 1, RW_COLS), lambda i, j: (i, 0, 0))],
        out_shape=[
            jax.ShapeDtypeStruct((b, t, KV_LORA), F32),
            jax.ShapeDtypeStruct((b, t, ROPE_DIM), F32),
            jax.ShapeDtypeStruct((b, t, QK_WIDTH), BF16),
            jax.ShapeDtypeStruct((b, KV_LORA, t), BF16),
            jax.ShapeDtypeStruct((b, t // tq, QK_WIDTH, A_HEADS * tq), BF16),
        ] + [jax.ShapeDtypeStruct((b, t, R_WIDTH), F32)] * 8 + [jax.ShapeDtypeStruct((b, 1, RW_COLS), F32)],
        scratch_shapes=[pltpu.VMEM((1, RW_COLS), F32)],
        compiler_params=_cparams(("parallel", "arbitrary")),
        name="in_proj",
    )(x, mod, g0, win, qg, kvg, wuq, wuk, cck, ssk, ccq, ssq, shift0, *rw_params)


COLS = 2 * LANE
SCORE_LOOKAHEAD = 4
SAMPLE_SUB_KEYS = 512


def _softmax_update(s, vt, c, m_ref, l_ref, acc_ref):
    m_prev = m_ref[c]
    m_new = jnp.maximum(m_prev, jnp.max(s, axis=0, keepdims=True))
    alpha = jnp.exp2(m_prev - m_new)
    p = jnp.exp2(s - m_new)
    l_ref[c] = alpha * l_ref[c] + jnp.sum(p, axis=0, keepdims=True)
    acc_ref[c] = alpha * acc_ref[c] + jnp.dot(vt, p.astype(BF16), preferred_element_type=F32)
    m_ref[c] = m_new


def _pipelined_updates(units, m_ref, l_ref, acc_ref):
    n = len(units)
    pending = [units[u][0]() for u in range(min(SCORE_LOOKAHEAD, n))]
    for u in range(n):
        if u + SCORE_LOOKAHEAD < n:
            pending.append(units[u + SCORE_LOOKAHEAD][0]())
        _softmax_update(pending.pop(0), units[u][1], units[u][2], m_ref, l_ref, acc_ref)


def _pipelined_chunks(scores, vt, nc, m_ref, l_ref, acc_ref):
    _pipelined_updates([(functools.partial(scores, c), vt, c) for c in range(nc)], m_ref, l_ref, acc_ref)


def _attn_init(m_ref, l_ref, acc_ref):
    m_ref[...] = jnp.full(m_ref.shape, -jnp.inf, F32)
    l_ref[...] = jnp.zeros(l_ref.shape, F32)
    acc_ref[...] = jnp.zeros(acc_ref.shape, F32)


def _attn_finish(l_ref, acc_ref, wuv_ref, o_ref, tq):
    outs = []
    for hd in range(A_HEADS):
        c, off = divmod(hd * tq, COLS)
        o_lat = acc_ref[c, :, off:off + tq] / l_ref[c, :, off:off + tq]
        outs.append(jnp.dot(wuv_ref[hd], o_lat.astype(BF16), preferred_element_type=F32))
    o_ref[0] = jnp.concatenate(outs, axis=0).T.astype(o_ref.dtype)


def _attn_scratch(tq):
    nc = A_HEADS * tq // COLS
    return [pltpu.VMEM((nc, 1, COLS), F32), pltpu.VMEM((nc, 1, COLS), F32),
            pltpu.VMEM((nc, KV_LORA, COLS), F32)]


def _prompt_attn_kernel(q_ref, k_ref, vt_ref, wuv_ref, o_ref, m_ref, l_ref, acc_ref, *, tq):
    i = pl.program_id(1)
    nc = A_HEADS * tq // COLS
    _attn_init(m_ref, l_ref, acc_ref)

    def tile(j, bias):
        off = pl.multiple_of(j * tq, tq)
        k = k_ref[0, pl.ds(off, tq), :]
        vt = vt_ref[0, :, pl.ds(off, tq)]

        def scores(c):
            s = jnp.dot(k, q_ref[0, 0, :, c * COLS:(c + 1) * COLS], preferred_element_type=F32)
            return s if bias is None else s + bias

        _pipelined_chunks(scores, vt, nc, m_ref, l_ref, acc_ref)

    def full_tile(j, carry):
        tile(j, None)
        return carry

    lax.fori_loop(0, i, full_tile, 0)

    k_chunk = lax.broadcasted_iota(jnp.int32, (tq, COLS), 0) // CHUNK
    q_chunk = (lax.broadcasted_iota(jnp.int32, (tq, COLS), 1) % tq) // CHUNK
    tile(i, jnp.where(k_chunk <= q_chunk, 0.0, -jnp.inf).astype(F32))
    _attn_finish(l_ref, acc_ref, wuv_ref, o_ref, tq)


def _prompt_attention(qt, kcat, vt, wuv, layer, tq):
    b, t, _ = kcat.shape
    return pl.pallas_call(
        functools.partial(_prompt_attn_kernel, tq=tq),
        grid=(b, t // tq),
        in_specs=[
            pl.BlockSpec((1, 1, QK_WIDTH, A_HEADS * tq), lambda i, j: (i, j, 0, 0)),
            pl.BlockSpec((1, t, QK_WIDTH), lambda i, j: (i, 0, 0)),
            pl.BlockSpec((1, KV_LORA, t), lambda i, j: (i, 0, 0)),
            _layer_spec(wuv, layer),
        ],
        out_specs=pl.BlockSpec((1, tq, A_HEADS * V_DIM), lambda i, j: (i, j, 0)),
        out_shape=jax.ShapeDtypeStruct((b, t, A_HEADS * V_DIM), BF16),
        scratch_shapes=_attn_scratch(tq),
        compiler_params=_cparams(("parallel", "arbitrary")),
        name="prompt_attn",
    )(qt, kcat, vt, wuv)


def _sample_attn_kernel(q_ref, kn_ref, vtn_ref, cckv_ref, ckr_ref, wuv_ref, o_ref,
                        m_ref, l_ref, acc_ref, *, tq):
    j = pl.program_id(1)
    nc = A_HEADS * tq // COLS

    @pl.when(j == 0)
    def _():
        _attn_init(m_ref, l_ref, acc_ref)

    def cache_scores(ckv_b, kr, c):
        cols = slice(c * COLS, (c + 1) * COLS)
        return (jnp.dot(ckv_b, q_ref[0, 0, 0:KV_LORA, cols], preferred_element_type=F32)
                + jnp.dot(kr, q_ref[0, 0, KV_LORA:KV_LORA + ROPE_DIM, cols], preferred_element_type=F32))

    units = []
    for sub in range(cckv_ref.shape[2] // SAMPLE_SUB_KEYS):
        keys = slice(sub * SAMPLE_SUB_KEYS, (sub + 1) * SAMPLE_SUB_KEYS)
        ckv = cckv_ref[0, 0, keys, :]
        ckv_b = ckv.astype(BF16)
        vt = ckv.T.astype(BF16)
        kr = ckr_ref[0, 0, :, keys].T.astype(BF16)
        units += [(functools.partial(cache_scores, ckv_b, kr, c), vt, c) for c in range(nc)]
    _pipelined_updates(units, m_ref, l_ref, acc_ref)

    @pl.when(j == pl.num_programs(1) - 1)
    def _():
        kn = kn_ref[0]
        new_scores = lambda c: jnp.dot(kn, q_ref[0, 0, :, c * COLS:(c + 1) * COLS],
                                       preferred_element_type=F32)
        _pipelined_chunks(new_scores, vtn_ref[0], nc, m_ref, l_ref, acc_ref)
        _attn_finish(l_ref, acc_ref, wuv_ref, o_ref, tq)


def _sample_attention(qt, kcat, vt, cache_ckv, cache_krope_t, layer, wuv, tk):
    b, tq, _ = kcat.shape
    past = cache_ckv.shape[2]
    return pl.pallas_call(
        functools.partial(_sample_attn_kernel, tq=tq),
        grid=(b, past // tk),
        in_specs=[
            pl.BlockSpec((1, 1, QK_WIDTH, A_HEADS * tq), lambda i, j: (i, 0, 0, 0)),
            pl.BlockSpec((1, tq, QK_WIDTH), lambda i, j: (i, 0, 0)),
            pl.BlockSpec((1, KV_LORA, tq), lambda i, j: (i, 0, 0)),
            pl.BlockSpec((1, 1, tk, KV_LORA), lambda i, j: (layer, i, j, 0)),
            pl.BlockSpec((1, 1, ROPE_DIM, tk), lambda i, j: (layer, i, 0, j)),
            _layer_spec(wuv, layer),
        ],
        out_specs=pl.BlockSpec((1, tq, A_HEADS * V_DIM), lambda i, j: (i, 0, 0)),
        out_shape=jax.ShapeDtypeStruct((b, tq, A_HEADS * V_DIM), BF16),
        scratch_shapes=_attn_scratch(tq),
        compiler_params=_cparams(("parallel", "arbitrary")),
        name="sample_attn",
    )(qt, kcat, vt, cache_ckv, cache_krope_t, wuv)


def _rwkv_features(rw, rw_prev, mu_ref, w0_ref, w2_ref, a0_ref, a2_ref, g2_ref, kkw_ref, ka_ref, rk_ref,
                   bd_ref):
    mix = rw + (rw_prev - rw) * mu_ref[...]
    r = mix[:, 0:R_WIDTH]
    k = mix[:, R_WIDTH:2 * R_WIDTH]
    v = mix[:, 2 * R_WIDTH:3 * R_WIDTH]
    o = 3 * R_WIDTH
    w_lo = mix[:, o:o + DECAY_LORA]
    a_lo = mix[:, o + DECAY_LORA:o + DECAY_LORA + AAA_LORA]
    g_lo = mix[:, o + DECAY_LORA + AAA_LORA:RW_COLS]
    w_raw = w0_ref[...] + _bdot(jnp.tanh(w_lo), w2_ref[...])
    nw = -w_raw
    softplus = jnp.maximum(nw, 0.0) + jnp.log(1.0 + jnp.exp(-jnp.abs(nw)))
    log_decay = -jnp.exp(-softplus - 0.5)
    a = _sigmoid(a0_ref[...] + _bdot(a_lo, a2_ref[...]))
    g = _bdot(_sigmoid(g_lo), g2_ref[...])
    bd_half = bd_ref[0:TILE, 0:TILE]

    def head_sums(x):
        return jnp.concatenate([_split_dot(x[:, i * TILE:(i + 1) * TILE], bd_half)
                                for i in range(R_WIDTH // TILE)], axis=-1)

    kk = k * kkw_ref[...]
    kk = kk / jnp.maximum(jnp.sqrt(head_sums(kk * kk)), 1e-12)
    k2 = k * (1.0 + (a - 1.0) * ka_ref[...])
    bonus = head_sums(r * k2 * rk_ref[...]) * v
    return r, log_decay, k2, v, -kk, kk * a, g, bonus


SCAN_CHUNK = R_HEAD_DIM
TILE = 4 * R_HEAD_DIM
SUB_BLOCK = 16
BATCH_UNROLL = 4


def _contract_last(a, b):
    return lax.dot_general(a, b, (((1,), (1,)), ((), ())), preferred_element_type=F32)


def _rwkv_scan_kernel(r_ref, lw_ref, k_ref, v_ref, al_ref, be_ref, g_ref, bonus_ref, s0_ref,
                      lg_ref, lb_ref, bd_ref,
                      o_ref, sout_ref,
                      p_ref, at_ref, bh_ref, u_ref, z_ref, y_ref, aab_ref, arb_ref, bk_ref, ec_ref,
                      coef_ref, *us_refs, nb, tt):
    j = pl.program_id(1)
    n = R_HEAD_DIM
    c = SCAN_CHUNK
    nt = R_WIDTH // TILE
    bd = bd_ref[...]
    bd_half = bd[0:TILE, 0:TILE]
    block_mask = (lax.broadcasted_iota(jnp.int32, (TILE, TILE), 0) // n
                  == lax.broadcasted_iota(jnp.int32, (TILE, TILE), 1) // n)
    t_id = lax.broadcasted_iota(jnp.int32, (c, TILE), 0)
    j_id = lax.broadcasted_iota(jnp.int32, (c, TILE), 1) % c
    strict = j_id < t_id
    incl = j_id <= t_id
    later = (lax.broadcasted_iota(jnp.int32, (SUB_BLOCK, SUB_BLOCK, R_WIDTH), 1)
             > lax.broadcasted_iota(jnp.int32, (SUB_BLOCK, SUB_BLOCK, R_WIDTH), 0))
    tri = (lax.broadcasted_iota(jnp.int32, (c, c), 1)
           <= lax.broadcasted_iota(jnp.int32, (c, c), 0)).astype(BF16)

    @pl.when(j == 0)
    def _():
        for b in range(nb):
            for i in range(nt):
                rows = []
                for hh in range(TILE // n):
                    parts = []
                    if hh:
                        parts.append(jnp.zeros((n, hh * n), F32))
                    parts.append(s0_ref[b, i * (TILE // n) + hh])
                    if TILE - (hh + 1) * n:
                        parts.append(jnp.zeros((n, TILE - (hh + 1) * n), F32))
                    rows.append(jnp.concatenate(parts, axis=-1))
                p_ref[b, i] = jnp.concatenate(rows, axis=0)

    def block_diag(x):
        return jnp.where(block_mask, jnp.concatenate([x] * (TILE // c), axis=0), 0.0).astype(BF16)

    def head_sums(x):
        xb = x.astype(BF16)
        return jnp.concatenate(
            [jnp.dot(xb[:, i * TILE:(i + 1) * TILE], bd_half, preferred_element_type=F32)
             for i in range(nt)], axis=-1)

    for ci in range(tt // c):
        rows = pl.ds(ci * c, c)

        def prepare(b, carry):
            lw = lw_ref[b, rows, :]
            hi = lw.astype(BF16)
            rem = lw - hi.astype(F32)
            mid = rem.astype(BF16)
            lo = (rem - mid.astype(F32)).astype(BF16)
            linc = (jnp.dot(tri, hi, preferred_element_type=F32)
                    + jnp.dot(tri, mid, preferred_element_type=F32)
                    + jnp.dot(tri, lo, preferred_element_type=F32))
            lc = linc[c - 1:c]
            e_neg = jnp.exp(-linc)
            e_rem = jnp.exp(lc - linc)
            al, be, r, k, v = (ref[b, rows, :] for ref in (al_ref, be_ref, r_ref, k_ref, v_ref))
            at = al * jnp.exp(linc - lw)
            rt = r * jnp.exp(linc)
            bh = be * e_neg
            kh = k * e_neg
            at_ref[b] = at
            bh_ref[b] = bh
            ec_ref[b] = jnp.exp(lc)
            bk_ref[b] = jnp.concatenate([be * e_rem, k * e_rem], axis=0).astype(BF16)
            for i in range(nt):
                ls = slice(i * TILE, (i + 1) * TILE)
                rt_b = rt[:, ls].astype(BF16)
                x = jnp.concatenate([at[:, ls].astype(BF16), rt_b], axis=0)
                a_k = _contract_last(x, block_diag(kh[:, ls]))
                a_b = _contract_last(x, block_diag(bh[:, ls]))
                x_p = _contract_last(x, p_ref[b, i].astype(BF16))
                w_v = block_diag(v[:, ls])
                u_ref[b, :, ls] = x_p[0:c] + jnp.dot(
                    jnp.where(strict, a_k[0:c], 0.0).astype(BF16), w_v, preferred_element_type=F32)
                y_ref[b, rows, ls] = x_p[c:2 * c] + jnp.dot(
                    jnp.where(incl, a_k[c:2 * c], 0.0).astype(BF16), w_v, preferred_element_type=F32)
                aab_ref[b, i] = jnp.where(strict, a_b[0:c], 0.0).astype(BF16)
                arb_ref[b, i] = jnp.where(incl, a_b[c:2 * c], 0.0).astype(BF16)
            return carry

        lax.fori_loop(0, nb, prepare, 0, unroll=BATCH_UNROLL)
        z_ref[...] = jnp.zeros(z_ref.shape, F32)

        for sb in range(c // SUB_BLOCK):
            srows = pl.ds(sb * SUB_BLOCK, SUB_BLOCK)

            def stage(b, carry):
                lhs = bh_ref[b, srows, :][:, None, :] * at_ref[b, srows, :][None, :, :]
                coef = head_sums(lhs.reshape(SUB_BLOCK * SUB_BLOCK, R_WIDTH))
                coef_ref[:, pl.ds(pl.multiple_of(b * SUB_BLOCK, SUB_BLOCK), SUB_BLOCK), :] = jnp.where(
                    later, coef.reshape(SUB_BLOCK, SUB_BLOCK, R_WIDTH), 0.0)
                if sb:
                    z = z_ref[b]
                    for i in range(nt):
                        ls = slice(i * TILE, (i + 1) * TILE)
                        u_ref[b, srows, ls] += jnp.dot(aab_ref[b, i, srows, :], block_diag(z[:, ls]),
                                                       preferred_element_type=F32)
                return carry

            lax.fori_loop(0, nb, stage, 0, unroll=BATCH_UNROLL)

            for b in range(nb):
                us_refs[b][...] = u_ref[b, srows, :]

            def substitute(j, carry):
                coef = coef_ref[j]
                for b in range(nb):
                    z_j = us_refs[b][pl.ds(j, 1), :]
                    us_refs[b][...] += coef[b * SUB_BLOCK:(b + 1) * SUB_BLOCK] * z_j
                return carry

            lax.fori_loop(0, SUB_BLOCK - 1, substitute, 0)
            for b in range(nb):
                z_ref[b, srows, :] = us_refs[b][...]

        def finish(b, carry):
            z = z_ref[b]
            v = v_ref[b, rows, :]
            for i in range(nt):
                ls = slice(i * TILE, (i + 1) * TILE)
                y_ref[b, rows, ls] += jnp.dot(arb_ref[b, i], block_diag(z[:, ls]),
                                              preferred_element_type=F32)
                zv_t = jnp.concatenate([z[:, ls], v[:, ls]], axis=0).T.astype(BF16)
                upd = jnp.dot(zv_t, bk_ref[b, :, ls], preferred_element_type=F32)
                p_ref[b, i] = p_ref[b, i] * ec_ref[b, :, ls] + jnp.where(block_mask, upd, 0.0)
            return carry

        lax.fori_loop(0, nb, finish, 0, unroll=BATCH_UNROLL)

    def head_means(x):
        return jnp.concatenate(
            [_split_dot(x[:, i * TILE:(i + 1) * TILE], bd_half) for i in range(nt)], axis=-1) * (1.0 / n)

    y = y_ref[...].reshape(nb * tt, R_WIDTH)
    d = y - head_means(y)
    yn = d * lax.rsqrt(head_means(d * d) + LNX_EPS) * lg_ref[...] + lb_ref[...]
    out = (yn + bonus_ref[...].reshape(nb * tt, R_WIDTH)) * g_ref[...].reshape(nb * tt, R_WIDTH)
    o_ref[...] = out.reshape(nb, tt, R_WIDTH).astype(o_ref.dtype)

    @pl.when(j == pl.num_programs(1) - 1)
    def _():
        for b in range(nb):
            for i in range(nt):
                p = p_ref[b, i]
                for hh in range(TILE // n):
                    sout_ref[b, i * (TILE // n) + hh] = p[hh * n:(hh + 1) * n, hh * n:(hh + 1) * n]


def _rwkv_scan(r, lw, k, v, al, be, g, bonus, s0, lg, lb, bd, nb, tt):
    b, t, _ = r.shape
    c = SCAN_CHUNK
    row = pl.BlockSpec((nb, tt, R_WIDTH), lambda i, j: (i, j, 0))
    st = pl.BlockSpec((nb, R_HEADS, R_HEAD_DIM, R_HEAD_DIM), lambda i, j: (i, 0, 0, 0))
    nt = R_WIDTH // TILE
    return pl.pallas_call(
        functools.partial(_rwkv_scan_kernel, nb=nb, tt=tt),
        grid=(b // nb, t // tt),
        in_specs=[row] * 8 + [st, _const_spec((1, R_WIDTH)), _const_spec((1, R_WIDTH)),
                              _const_spec((R_WIDTH, R_WIDTH))],
        out_specs=[row, st],
        out_shape=[jax.ShapeDtypeStruct((b, t, R_WIDTH), BF16),
                   jax.ShapeDtypeStruct((b, R_HEADS, R_HEAD_DIM, R_HEAD_DIM), F32)],
        scratch_shapes=[pltpu.VMEM((nb, nt, TILE, TILE), F32),
                        pltpu.VMEM((nb, c, R_WIDTH), F32),
                        pltpu.VMEM((nb, c, R_WIDTH), F32),
                        pltpu.VMEM((nb, c, R_WIDTH), F32),
                        pltpu.VMEM((nb, c, R_WIDTH), F32),
                        pltpu.VMEM((nb, tt, R_WIDTH), F32),
                        pltpu.VMEM((nb, nt, c, TILE), BF16),
                        pltpu.VMEM((nb, nt, c, TILE), BF16),
                        pltpu.VMEM((nb, 2 * c, R_WIDTH), BF16),
                        pltpu.VMEM((nb, 1, R_WIDTH), F32),
                        pltpu.VMEM((SUB_BLOCK, nb * SUB_BLOCK, R_WIDTH), F32)]
                       + [pltpu.VMEM((SUB_BLOCK, R_WIDTH), F32)] * nb,
        compiler_params=_cparams(("parallel", "arbitrary")),
        name="rwkv_scan",
    )(r, lw, k, v, al, be, g, bonus, s0, lg, lb, bd)


def _out_ffn_kernel(x_ref, oa_ref, orw_ref, mod_ref, ng_ref, wout_ref, wg_ref, wu_ref, wd_ref, o_ref):
    gb, tr, _ = x_ref.shape
    rows = gb * tr
    x = x_ref[...]
    mod = mod_ref[...]
    gt_a = mod[:, :, 2 * D_MODEL:3 * D_MODEL]
    sh_f = mod[:, :, 3 * D_MODEL:4 * D_MODEL]
    sc_f = mod[:, :, 4 * D_MODEL:5 * D_MODEL]
    gt_f = mod[:, :, 5 * D_MODEL:6 * D_MODEL]
    half = A_HEADS * V_DIM
    m = (jnp.dot(oa_ref[...].reshape(rows, half), wout_ref[0:half, :], preferred_element_type=F32)
         + jnp.dot(orw_ref[...].reshape(rows, R_WIDTH), wout_ref[half:D_MIX, :],
                   preferred_element_type=F32))
    x = x + gt_a * _rms(m.reshape(gb, tr, D_MODEL), ng_ref[1:2, :])
    h = (_rms(x, ng_ref[2:3, :]) * (1.0 + sc_f) + sh_f).reshape(rows, D_MODEL).astype(BF16)

    def gate_up(c):
        cols = slice(c * FFN_CHUNK, (c + 1) * FFN_CHUNK)
        return (jnp.dot(h, wg_ref[:, cols], preferred_element_type=F32),
                jnp.dot(h, wu_ref[:, cols], preferred_element_type=F32))

    f = jnp.zeros((rows, D_MODEL), F32)
    n_chunks = D_FF // FFN_CHUNK
    pending = [gate_up(c) for c in range(min(FFN_LOOKAHEAD, n_chunks))]
    for c in range(n_chunks):
        if c + FFN_LOOKAHEAD < n_chunks:
            pending.append(gate_up(c + FFN_LOOKAHEAD))
        gate, up = pending.pop(0)
        act = (gate * _sigmoid(gate) * up).astype(BF16)
        f = f + jnp.dot(act, wd_ref[c * FFN_CHUNK:(c + 1) * FFN_CHUNK, :], preferred_element_type=F32)
    o_ref[...] = x + gt_f * _rms(f.reshape(gb, tr, D_MODEL), ng_ref[3:4, :])


def _out_ffn(x, o_attn, o_rw, mod, ng, wout, wg, wu, wd, layer, gb, tr):
    b, t, _ = x.shape
    row = lambda w: pl.BlockSpec((gb, tr, w), lambda i, j: (i, j, 0))
    wspec = lambda w: _layer_spec(w, layer, pipeline_mode=pl.Buffered(1))
    return pl.pallas_call(
        _out_ffn_kernel,
        grid=(b // gb, t // tr),
        in_specs=[
            row(D_MODEL), row(A_HEADS * V_DIM), row(R_WIDTH),
            pl.BlockSpec((gb, 1, 6 * D_MODEL), lambda i, j: (i, 0, 0)),
            _const_spec((4, D_MODEL)),
            wspec(wout), wspec(wg), wspec(wu), wspec(wd),
        ],
        out_specs=row(D_MODEL),
        out_shape=jax.ShapeDtypeStruct((b, t, D_MODEL), F32),
        compiler_params=_cparams(("parallel", "parallel")),
        name="out_ffn",
    )(x, o_attn, o_rw, mod, ng, wout, wg, wu, wd)


def _rope_tables(pos):
    inv = ROPE_THETA ** (-jnp.arange(0, ROPE_DIM, 2, dtype=F32) / ROPE_DIM)
    ang = pos.astype(F32)[:, None] * inv[None, :]
    cos, sin = jnp.cos(ang), jnp.sin(ang)
    cc = jnp.concatenate([cos, cos], axis=-1)
    ss = jnp.concatenate([-sin, sin], axis=-1)
    return cc, ss, jnp.tile(cc, (1, A_HEADS)).T, jnp.tile(ss, (1, A_HEADS)).T


def _pack_weights(w_in, w_uq, w_uk, w_uv):
    depth = w_in.shape[0]
    half = ROPE_DIM // 2
    o = Q_LORA + KV_LORA
    k_r = w_in[:, :, o:o + ROPE_DIM]
    k_rs = jnp.concatenate([k_r[..., half:], k_r[..., :half]], axis=-1)
    pad = jnp.zeros((depth, D_MODEL, LANE - 2 * ROPE_DIM), w_in.dtype)
    win = jnp.concatenate([w_in[:, :, :o], k_r, k_rs, pad, w_in[:, :, MLA_COLS:]], axis=-1)
    uq = w_uq.reshape(depth, Q_LORA, A_HEADS, NOPE_DIM + ROPE_DIM)
    nope = uq[..., :NOPE_DIM].reshape(depth, Q_LORA, A_HEADS * NOPE_DIM)
    rope = uq[..., NOPE_DIM:]
    rope_s = jnp.concatenate([rope[..., half:], rope[..., :half]], axis=-1)
    wuq = jnp.concatenate([nope, rope.reshape(depth, Q_LORA, -1), rope_s.reshape(depth, Q_LORA, -1)], axis=-1)
    wuq = jnp.transpose(wuq, (0, 2, 1))
    wuk = jnp.transpose(w_uk, (0, 2, 1, 3))
    wuv = jnp.transpose(w_uv, (0, 2, 3, 1))
    return win.astype(BF16), wuq.astype(BF16), wuk.astype(BF16), wuv.astype(BF16)


def _block_diag_ones():
    hid = jnp.arange(R_WIDTH) // R_HEAD_DIM
    bd = (hid[:, None] == hid[None, :]).astype(BF16)
    return bd


def _row_tile(t, pref):
    return pref if t % pref == 0 else t


def kernel(x_prompt, x_sample, c_prompt, c_sample, cache_ckv, cache_krope, state_wkv, state_shift,
           ada_w, ada_b, norm_g, w_in, q_norm_g, kv_norm_g, w_uq, w_uk, w_uv,
           rw_mu, rw_w0, rw_w2, rw_a0, rw_a2, rw_g2, rw_kk, rw_ka, rw_rk, lnx_g, lnx_b,
           w_out, w_gate, w_up, w_down):
    depth = w_in.shape[0]
    bp, tp, _ = x_prompt.shape
    bs, ts, _ = x_sample.shape
    past = cache_ckv.shape[2]

    mod_all = _modulation(jnp.concatenate([c_prompt, c_sample], axis=0), ada_w, ada_b)
    win, wuq, wuk, wuv = _pack_weights(w_in, w_uq, w_uk, w_uv)
    wout, wg, wu, wd = (w.astype(BF16) for w in (w_out, w_gate, w_up, w_down))
    w2, a2, g2 = (w.astype(BF16) for w in (rw_w2, rw_a2, rw_g2))
    bd = _block_diag_ones()
    tabs_p = _rope_tables(jnp.arange(tp))
    tabs_s = _rope_tables(past + jnp.arange(ts))
    cache_krope_t = jnp.swapaxes(cache_krope, 2, 3)
    vec = lambda a, l: a[l].reshape(1, -1)

    def layer(l, x, mod, tabs, s0, shift0, cache):
        b, t, _ = x.shape
        tr = _row_tile(t, IN_ROWS)
        gb = max(1, min(b, IN_ROWS // t))
        gb = gb if b % gb == 0 else 1
        tabs = tuple(jnp.tile(tb, (gb, 1) if tb.shape[0] == t else (1, gb)) for tb in tabs)
        tq = _row_tile(t, COLS)
        rw_params = (vec(rw_mu, l), vec(rw_w0, l), w2[l], vec(rw_a0, l), a2[l], g2[l],
                     vec(rw_kk, l), vec(rw_ka, l), rw_rk[l].reshape(1, -1), bd)
        ckv, krope, kcat, vt, qt, r, w, k, v, al, be, g, bonus, shift_new = _in_proj(
            x, mod, vec(norm_g[:, 0], l), win, vec(q_norm_g, l), vec(kv_norm_g, l),
            wuq, wuk, *tabs, shift0, rw_params, l, gb, tr, tq)
        if cache is None:
            o_attn = _prompt_attention(qt, kcat, vt, wuv, l, tq)
        else:
            o_attn = _sample_attention(qt, kcat, vt, cache[0], cache[1], l, wuv, min(2048, past))
        ts_ = SCAN_CHUNK
        o_rw, s_new = _rwkv_scan(r, w, k, v, al, be, g, bonus, s0, vec(lnx_g, l), vec(lnx_b, l),
                                 bd, 8 if b % 8 == 0 else b, ts_)
        gb = max(1, min(b, FFN_ROWS // t))
        x_new = _out_ffn(x, o_attn, o_rw, mod, norm_g[l], wout, wg, wu, wd, l,
                         gb if b % gb == 0 else 1, min(t, FFN_ROWS))
        return x_new, ckv, krope, s_new, shift_new[:, 0]

    xp, xs = x_prompt, x_sample
    wkv_zero = jnp.zeros((bp, R_HEADS, R_HEAD_DIM, R_HEAD_DIM), F32)
    shift_zero = jnp.zeros((bp, 1, RW_COLS), F32)
    outs_p, outs_s = [], []
    for l in range(depth):
        mod_p = mod_all[l, :bp].reshape(bp, 1, 6 * D_MODEL)
        mod_s = mod_all[l, bp:].reshape(bs, 1, 6 * D_MODEL)
        xp, *rest_p = layer(l, xp, mod_p, tabs_p, wkv_zero, shift_zero, None)
        xs, *rest_s = layer(l, xs, mod_s, tabs_s, state_wkv[l],
                            state_shift[l].reshape(bs, 1, RW_COLS), (cache_ckv, cache_krope_t))
        outs_p.append(rest_p)
        outs_s.append(rest_s)
    stack = lambda outs, i: jnp.stack([o[i] for o in outs])
    return (xp, xs,
            stack(outs_p, 0), stack(outs_p, 1), stack(outs_p, 2), stack(outs_p, 3),
            stack(outs_s, 0), stack(outs_s, 1), stack(outs_s, 2), stack(outs_s, 3))
```

```python
import functools

import jax
import jax.numpy as jnp
from jax import lax
from jax.experimental import pallas as pl
from jax.experimental.pallas import tpu as pltpu

D_MODEL = 1024
CHUNK = 64
A_HEADS = 8
NOPE_DIM = 64
ROPE_DIM = 32
V_DIM = 64
Q_LORA = 384
KV_LORA = 256
ROPE_THETA = 10000.0
SM_SCALE = (NOPE_DIM + ROPE_DIM) ** -0.5
Q_SCALE = SM_SCALE * 1.4426950408889634
R_HEADS = 8
R_HEAD_DIM = 64
R_WIDTH = R_HEADS * R_HEAD_DIM
DECAY_LORA = 64
AAA_LORA = 64
GATE_LORA = 128
RW_COLS = 3 * R_WIDTH + DECAY_LORA + AAA_LORA + GATE_LORA
MLA_COLS = Q_LORA + KV_LORA + ROPE_DIM
D_MIX = A_HEADS * V_DIM + R_WIDTH
D_FF = 2816
RMS_EPS = 1e-6
LNX_EPS = 64e-5

LANE = 128
QK_WIDTH = KV_LORA + LANE
RW_OFF = Q_LORA + KV_LORA + LANE
IN_PACKED = RW_OFF + RW_COLS
VMEM_LIMIT = 56 * 1024 * 1024
IN_ROWS = 512
FFN_ROWS = 512
FFN_CHUNK = 256
FFN_LOOKAHEAD = 2

BF16 = jnp.bfloat16
F32 = jnp.float32


def _cparams(sem):
    return pltpu.CompilerParams(dimension_semantics=sem, vmem_limit_bytes=VMEM_LIMIT)


def _const_spec(shape):
    n = len(shape)
    return pl.BlockSpec(shape, lambda *_: (0,) * n)


def _layer_spec(stacked, layer, **kwargs):
    shape = stacked.shape[1:]
    zeros = (0,) * len(shape)
    return pl.BlockSpec((None,) + shape, lambda *_: (layer,) + zeros, **kwargs)


def _bdot(a, b):
    return jnp.dot(a.astype(BF16), b.astype(BF16), preferred_element_type=F32)


def _split_dot(a, b):
    hi = a.astype(BF16)
    lo = (a - hi.astype(F32)).astype(BF16)
    return (jnp.dot(hi, b, preferred_element_type=F32)
            + jnp.dot(lo, b, preferred_element_type=F32))


def _rms(x, g):
    return x * lax.rsqrt(jnp.mean(x * x, axis=-1, keepdims=True) + RMS_EPS) * g


def _sigmoid(x):
    return 1.0 / (1.0 + jnp.exp(-x))


def _mod_kernel(c_ref, w_ref, b_ref, o_ref):
    c = c_ref[...]
    s = c * _sigmoid(c)
    o_ref[0] = _bdot(s, w_ref[0]) + b_ref[0]


def _modulation(c_all, ada_w, ada_b):
    depth = ada_w.shape[0]
    n = c_all.shape[0]
    tn = 1024
    return pl.pallas_call(
        _mod_kernel,
        grid=(depth, 6 * D_MODEL // tn),
        in_specs=[
            pl.BlockSpec((n, D_MODEL), lambda l, j: (0, 0)),
            pl.BlockSpec((1, D_MODEL, tn), lambda l, j: (l, 0, j)),
            pl.BlockSpec((1, 1, tn), lambda l, j: (l, 0, j)),
        ],
        out_specs=pl.BlockSpec((1, n, tn), lambda l, j: (l, 0, j)),
        out_shape=jax.ShapeDtypeStruct((depth, n, 6 * D_MODEL), F32),
        compiler_params=_cparams(("parallel", "parallel")),
        name="adaln_mod",
    )(c_all, ada_w, ada_b.reshape(depth, 1, 6 * D_MODEL))


def _in_proj_kernel(x_ref, mod_ref, g0_ref, win_ref, qg_ref, kvg_ref, wuq_ref, wuk_ref,
                    cck_ref, ssk_ref, ccq_ref, ssq_ref, shift_ref, *rest, tq):
    rw_params = rest[0:10]
    ckv_ref, krope_ref, kcat_ref, vt_ref, qt_ref = rest[10:15]
    feat_refs = rest[15:23]
    shift_out_ref, prev_ref = rest[23:25]
    gb, tr, _ = x_ref.shape
    tm = gb * tr
    mod = mod_ref[...]
    sh = mod[:, :, 0:D_MODEL]
    sc = mod[:, :, D_MODEL:2 * D_MODEL]
    h = (_rms(x_ref[...], g0_ref[...]) * (1.0 + sc) + sh).reshape(tm, D_MODEL).astype(BF16)
    proj = jnp.dot(h, win_ref[:, 0:RW_OFF], preferred_element_type=F32)
    rw = jnp.dot(h, win_ref[:, RW_OFF:IN_PACKED], preferred_element_type=F32)
    c_q = proj[:, 0:Q_LORA]
    c_kv = proj[:, Q_LORA:Q_LORA + KV_LORA]
    k_r = proj[:, Q_LORA + KV_LORA:Q_LORA + KV_LORA + ROPE_DIM]
    k_rs = proj[:, Q_LORA + KV_LORA + ROPE_DIM:Q_LORA + KV_LORA + 2 * ROPE_DIM]

    ckv = _rms(c_kv, kvg_ref[...])
    krope = k_r * cck_ref[...] + k_rs * ssk_ref[...]
    ckv_ref[...] = ckv.reshape(gb, tr, KV_LORA)
    krope_ref[...] = krope.reshape(gb, tr, ROPE_DIM)
    zpad = jnp.zeros((tm, LANE - ROPE_DIM), F32)
    kcat_ref[...] = jnp.concatenate([ckv, krope, zpad], axis=-1).astype(BF16).reshape(gb, tr, QK_WIDTH)
    ckv_t = ckv.T.astype(BF16)
    for s in range(gb):
        vt_ref[s] = ckv_t[:, s * tr:(s + 1) * tr]

    cqn_t = _rms(c_q, qg_ref[...]).T.astype(BF16)
    q_t = jnp.dot(wuq_ref[...], cqn_t, preferred_element_type=F32)
    nope_w = A_HEADS * NOPE_DIM
    rope_w = A_HEADS * ROPE_DIM
    q_rope = (q_t[nope_w:nope_w + rope_w] * ccq_ref[...]
              + q_t[nope_w + rope_w:nope_w + 2 * rope_w] * ssq_ref[...])
    zrows = jnp.zeros((LANE - ROPE_DIM, tm), F32)
    for hd in range(A_HEADS):
        q_lat = jnp.dot(wuk_ref[hd], q_t[hd * NOPE_DIM:(hd + 1) * NOPE_DIM].astype(BF16),
                        preferred_element_type=F32)
        qr = q_rope[hd * ROPE_DIM:(hd + 1) * ROPE_DIM]
        qh = (jnp.concatenate([q_lat, qr, zrows], axis=0) * Q_SCALE).astype(BF16)
        nqb = tr // tq
        for s in range(gb):
            for qb in range(nqb):
                col = (s * nqb + qb) * tq
                qt_ref[s, qb, :, hd * tq:(hd + 1) * tq] = qh[:, col:col + tq]

    first = jnp.where(pl.program_id(1) == 0, shift_ref[...], prev_ref[...][None])
    rw3 = rw.reshape(gb, tr, RW_COLS)
    row_id = lax.broadcasted_iota(jnp.int32, (gb, tr, 1), 1)
    rw_prev = jnp.where(row_id == 0, first, pltpu.roll(rw3, 1, 1)).reshape(tm, RW_COLS)
    for ref, val in zip(feat_refs, _rwkv_features(rw, rw_prev, *rw_params)):
        ref[...] = val.reshape(gb, tr, R_WIDTH)
    prev_ref[...] = rw[tm - 1:tm]
    shift_out_ref[...] = rw3[:, tr - 1:tr, :]


def _in_proj(x, mod, g0, win, qg, kvg, wuq, wuk, cck, ssk, ccq, ssq, shift0, rw_params, layer, gb, tr, tq):
    b, t, _ = x.shape
    tm = gb * tr
    grid = (b // gb, t // tr)
    row = lambda w: pl.BlockSpec((gb, tr, w), lambda i, j: (i, j, 0))
    tab = lambda w: pl.BlockSpec((tm, w), lambda i, j: (j, 0))
    tab_t = pl.BlockSpec((A_HEADS * ROPE_DIM, tm), lambda i, j: (0, j))
    nqb = tr // tq
    return pl.pallas_call(
        functools.partial(_in_proj_kernel, tq=tq),
        grid=grid,
        in_specs=[
            row(D_MODEL),
            pl.BlockSpec((gb, 1, 6 * D_MODEL), lambda i, j: (i, 0, 0)),
            _const_spec((1, D_MODEL)),
            _layer_spec(win, layer),
            _const_spec((1, Q_LORA)),
            _const_spec((1, KV_LORA)),
            _layer_spec(wuq, layer),
            _layer_spec(wuk, layer),
            tab(ROPE_DIM), tab(ROPE_DIM), tab_t, tab_t,
            pl.BlockSpec((gb, 1, RW_COLS), lambda i, j: (i, 0, 0)),
        ] + [_const_spec(p.shape) for p in rw_params],
        out_specs=[
            row(KV_LORA), row(ROPE_DIM), row(QK_WIDTH),
            pl.BlockSpec((gb, KV_LORA, tr), lambda i, j: (i, 0, j)),
            pl.BlockSpec((gb, nqb, QK_WIDTH, A_HEADS * tq), lambda i, j: (i, j, 0, 0)),
        ] + [row(R_WIDTH)] * 8 + [pl.BlockSpec((gb, 1, RW_COLS), lambda i, j: (i, 0, 0))],
        out_shape=[
            jax.ShapeDtypeStruct((b, t, KV_LORA), F32),
            jax.ShapeDtypeStruct((b, t, ROPE_DIM), F32),
            jax.ShapeDtypeStruct((b, t, QK_WIDTH), BF16),
            jax.ShapeDtypeStruct((b, KV_LORA, t), BF16),
            jax.ShapeDtypeStruct((b, t // tq, QK_WIDTH, A_HEADS * tq), BF16),
        ] + [jax.ShapeDtypeStruct((b, t, R_WIDTH), F32)] * 8 + [jax.ShapeDtypeStruct((b, 1, RW_COLS), F32)],
        scratch_shapes=[pltpu.VMEM((1, RW_COLS), F32)],
        compiler_params=_cparams(("parallel", "arbitrary")),
        name="in_proj",
    )(x, mod, g0, win, qg, kvg, wuq, wuk, cck, ssk, ccq, ssq, shift0, *rw_params)


COLS = 2 * LANE
SCORE_LOOKAHEAD = 4
SAMPLE_SUB_KEYS = 512


def _softmax_update(s, vt, c, m_ref, l_ref, acc_ref):
    m_prev = m_ref[c]
    m_new = jnp.maximum(m_prev, jnp.max(s, axis=0, keepdims=True))
    alpha = jnp.exp2(m_prev - m_new)
    p = jnp.exp2(s - m_new)
    l_ref[c] = alpha * l_ref[c] + jnp.sum(p, axis=0, keepdims=True)
    acc_ref[c] = alpha * acc_ref[c] + jnp.dot(vt, p.astype(BF16), preferred_element_type=F32)
    m_ref[c] = m_new


def _pipelined_updates(units, m_ref, l_ref, acc_ref):
    n = len(units)
    pending = [units[u][0]() for u in range(min(SCORE_LOOKAHEAD, n))]
    for u in range(n):
        if u + SCORE_LOOKAHEAD < n:
            pending.append(units[u + SCORE_LOOKAHEAD][0]())
        _softmax_update(pending.pop(0), units[u][1], units[u][2], m_ref, l_ref, acc_ref)


def _pipelined_chunks(scores, vt, nc, m_ref, l_ref, acc_ref):
    _pipelined_updates([(functools.partial(scores, c), vt, c) for c in range(nc)], m_ref, l_ref, acc_ref)


def _attn_init(m_ref, l_ref, acc_ref):
    m_ref[...] = jnp.full(m_ref.shape, -jnp.inf, F32)
    l_ref[...] = jnp.zeros(l_ref.shape, F32)
    acc_ref[...] = jnp.zeros(acc_ref.shape, F32)


def _attn_finish(l_ref, acc_ref, wuv_ref, o_ref, tq):
    outs = []
    for hd in range(A_HEADS):
        c, off = divmod(hd * tq, COLS)
        o_lat = acc_ref[c, :, off:off + tq] / l_ref[c, :, off:off + tq]
        outs.append(jnp.dot(wuv_ref[hd], o_lat.astype(BF16), preferred_element_type=F32))
    o_ref[0] = jnp.concatenate(outs, axis=0).T.astype(o_ref.dtype)


def _attn_scratch(tq):
    nc = A_HEADS * tq // COLS
    return [pltpu.VMEM((nc, 1, COLS), F32), pltpu.VMEM((nc, 1, COLS), F32),
            pltpu.VMEM((nc, KV_LORA, COLS), F32)]


def _prompt_attn_kernel(q_ref, k_ref, vt_ref, wuv_ref, o_ref, m_ref, l_ref, acc_ref, *, tq):
    i = pl.program_id(1)
    nc = A_HEADS * tq // COLS
    _attn_init(m_ref, l_ref, acc_ref)

    def tile(j, n_tiles, bias):
        off = pl.multiple_of(j * tq, tq)
        k = k_ref[0, pl.ds(off, n_tiles * tq), :]
        vt = vt_ref[0, :, pl.ds(off, n_tiles * tq)]

        def scores(c):
            s = jnp.dot(k, q_ref[0, 0, :, c * COLS:(c + 1) * COLS], preferred_element_type=F32)
            return s if bias is None else s + bias

        _pipelined_chunks(scores, vt, nc, m_ref, l_ref, acc_ref)

    def tile_pair(p, carry):
        tile(2 * p, 2, None)
        return carry

    lax.fori_loop(0, i // 2, tile_pair, 0)

    @pl.when(i % 2 == 1)
    def _():
        tile(i - 1, 1, None)

    k_chunk = lax.broadcasted_iota(jnp.int32, (tq, COLS), 0) // CHUNK
    q_chunk = (lax.broadcasted_iota(jnp.int32, (tq, COLS), 1) % tq) // CHUNK
    tile(i, 1, jnp.where(k_chunk <= q_chunk, 0.0, -jnp.inf).astype(F32))
    _attn_finish(l_ref, acc_ref, wuv_ref, o_ref, tq)


def _prompt_attention(qt, kcat, vt, wuv, layer, tq):
    b, t, _ = kcat.shape
    return pl.pallas_call(
        functools.partial(_prompt_attn_kernel, tq=tq),
        grid=(b, t // tq),
        in_specs=[
            pl.BlockSpec((1, 1, QK_WIDTH, A_HEADS * tq), lambda i, j: (i, j, 0, 0)),
            pl.BlockSpec((1, t, QK_WIDTH), lambda i, j: (i, 0, 0)),
            pl.BlockSpec((1, KV_LORA, t), lambda i, j: (i, 0, 0)),
            _layer_spec(wuv, layer),
        ],
        out_specs=pl.BlockSpec((1, tq, A_HEADS * V_DIM), lambda i, j: (i, j, 0)),
        out_shape=jax.ShapeDtypeStruct((b, t, A_HEADS * V_DIM), BF16),
        scratch_shapes=_attn_scratch(tq),
        compiler_params=_cparams(("parallel", "arbitrary")),
        name="prompt_attn",
    )(qt, kcat, vt, wuv)


def _sample_attn_kernel(q_ref, kn_ref, vtn_ref, cckv_ref, ckr_ref, wuv_ref, o_ref,
                        m_ref, l_ref, acc_ref, *, tq):
    j = pl.program_id(1)
    nc = A_HEADS * tq // COLS

    @pl.when(j == 0)
    def _():
        _attn_init(m_ref, l_ref, acc_ref)

    def cache_scores(ckv_b, kr, c):
        cols = slice(c * COLS, (c + 1) * COLS)
        return (jnp.dot(ckv_b, q_ref[0, 0, 0:KV_LORA, cols], preferred_element_type=F32)
                + jnp.dot(kr, q_ref[0, 0, KV_LORA:KV_LORA + ROPE_DIM, cols], preferred_element_type=F32))

    units = []
    for sub in range(cckv_ref.shape[2] // SAMPLE_SUB_KEYS):
        keys = slice(sub * SAMPLE_SUB_KEYS, (sub + 1) * SAMPLE_SUB_KEYS)
        ckv = cckv_ref[0, 0, keys, :]
        ckv_b = ckv.astype(BF16)
        vt = ckv.T.astype(BF16)
        kr = ckr_ref[0, 0, :, keys].T.astype(BF16)
        units += [(functools.partial(cache_scores, ckv_b, kr, c), vt, c) for c in range(nc)]
    _pipelined_updates(units, m_ref, l_ref, acc_ref)

    @pl.when(j == pl.num_programs(1) - 1)
    def _():
        kn = kn_ref[0]
        new_scores = lambda c: jnp.dot(kn, q_ref[0, 0, :, c * COLS:(c + 1) * COLS],
                                       preferred_element_type=F32)
        _pipelined_chunks(new_scores, vtn_ref[0], nc, m_ref, l_ref, acc_ref)
        _attn_finish(l_ref, acc_ref, wuv_ref, o_ref, tq)


def _sample_attention(qt, kcat, vt, cache_ckv, cache_krope_t, layer, wuv, tk):
    b, tq, _ = kcat.shape
    past = cache_ckv.shape[2]
    return pl.pallas_call(
        functools.partial(_sample_attn_kernel, tq=tq),
        grid=(b, past // tk),
        in_specs=[
            pl.BlockSpec((1, 1, QK_WIDTH, A_HEADS * tq), lambda i, j: (i, 0, 0, 0)),
            pl.BlockSpec((1, tq, QK_WIDTH), lambda i, j: (i, 0, 0)),
            pl.BlockSpec((1, KV_LORA, tq), lambda i, j: (i, 0, 0)),
            pl.BlockSpec((1, 1, tk, KV_LORA), lambda i, j: (layer, i, j, 0)),
            pl.BlockSpec((1, 1, ROPE_DIM, tk), lambda i, j: (layer, i, 0, j)),
            _layer_spec(wuv, layer),
        ],
        out_specs=pl.BlockSpec((1, tq, A_HEADS * V_DIM), lambda i, j: (i, 0, 0)),
        out_shape=jax.ShapeDtypeStruct((b, tq, A_HEADS * V_DIM), BF16),
        scratch_shapes=_attn_scratch(tq),
        compiler_params=_cparams(("parallel", "arbitrary")),
        name="sample_attn",
    )(qt, kcat, vt, cache_ckv, cache_krope_t, wuv)


def _rwkv_features(rw, rw_prev, mu_ref, w0_ref, w2_ref, a0_ref, a2_ref, g2_ref, kkw_ref, ka_ref, rk_ref,
                   bd_ref):
    mix = rw + (rw_prev - rw) * mu_ref[...]
    r = mix[:, 0:R_WIDTH]
    k = mix[:, R_WIDTH:2 * R_WIDTH]
    v = mix[:, 2 * R_WIDTH:3 * R_WIDTH]
    o = 3 * R_WIDTH
    w_lo = mix[:, o:o + DECAY_LORA]
    a_lo = mix[:, o + DECAY_LORA:o + DECAY_LORA + AAA_LORA]
    g_lo = mix[:, o + DECAY_LORA + AAA_LORA:RW_COLS]
    w_raw = w0_ref[...] + _bdot(jnp.tanh(w_lo), w2_ref[...])
    nw = -w_raw
    softplus = jnp.maximum(nw, 0.0) + jnp.log(1.0 + jnp.exp(-jnp.abs(nw)))
    log_decay = -jnp.exp(-softplus - 0.5)
    a = _sigmoid(a0_ref[...] + _bdot(a_lo, a2_ref[...]))
    g = _bdot(_sigmoid(g_lo), g2_ref[...])
    bd_half = bd_ref[0:TILE, 0:TILE]

    def head_sums(x):
        return jnp.concatenate([_split_dot(x[:, i * TILE:(i + 1) * TILE], bd_half)
                                for i in range(R_WIDTH // TILE)], axis=-1)

    kk = k * kkw_ref[...]
    kk = kk / jnp.maximum(jnp.sqrt(head_sums(kk * kk)), 1e-12)
    k2 = k * (1.0 + (a - 1.0) * ka_ref[...])
    bonus = head_sums(r * k2 * rk_ref[...]) * v
    return r, log_decay, k2, v, -kk, kk * a, g, bonus


SCAN_CHUNK = R_HEAD_DIM
TILE = 4 * R_HEAD_DIM
SUB_BLOCK = 16
BATCH_UNROLL = 8


def _contract_last(a, b):
    return lax.dot_general(a, b, (((1,), (1,)), ((), ())), preferred_element_type=F32)


def _rwkv_scan_kernel(r_ref, lw_ref, k_ref, v_ref, al_ref, be_ref, g_ref, bonus_ref, s0_ref,
                      lg_ref, lb_ref, bd_ref,
                      o_ref, sout_ref,
                      p_ref, at_ref, bh_ref, u_ref, z_ref, y_ref, aab_ref, arb_ref, bk_ref, ec_ref,
                      coef_ref, *us_refs, nb, tt):
    j = pl.program_id(1)
    n = R_HEAD_DIM
    c = SCAN_CHUNK
    nt = R_WIDTH // TILE
    group = BATCH_UNROLL if nb % BATCH_UNROLL == 0 else 1
    bd = bd_ref[...]
    bd_half = bd[0:TILE, 0:TILE]
    block_mask = (lax.broadcasted_iota(jnp.int32, (TILE, TILE), 0) // n
                  == lax.broadcasted_iota(jnp.int32, (TILE, TILE), 1) // n)
    t_id = lax.broadcasted_iota(jnp.int32, (c, TILE), 0)
    j_id = lax.broadcasted_iota(jnp.int32, (c, TILE), 1) % c
    strict = j_id < t_id
    incl = j_id <= t_id
    later = (lax.broadcasted_iota(jnp.int32, (SUB_BLOCK, SUB_BLOCK, R_WIDTH), 1)
             > lax.broadcasted_iota(jnp.int32, (SUB_BLOCK, SUB_BLOCK, R_WIDTH), 0))
    tri = (lax.broadcasted_iota(jnp.int32, (c, c), 1)
           <= lax.broadcasted_iota(jnp.int32, (c, c), 0)).astype(BF16)

    @pl.when(j == 0)
    def _():
        for b in range(nb):
            for i in range(nt):
                rows = []
                for hh in range(TILE // n):
                    parts = []
                    if hh:
                        parts.append(jnp.zeros((n, hh * n), F32))
                    parts.append(s0_ref[b, i * (TILE // n) + hh])
                    if TILE - (hh + 1) * n:
                        parts.append(jnp.zeros((n, TILE - (hh + 1) * n), F32))
                    rows.append(jnp.concatenate(parts, axis=-1))
                p_ref[b, i] = jnp.concatenate(rows, axis=0)

    def block_diag(x):
        return jnp.where(block_mask, jnp.concatenate([x] * (TILE // c), axis=0), 0.0).astype(BF16)

    def head_sums(x):
        xb = x.astype(BF16)
        return jnp.concatenate(
            [jnp.dot(xb[:, i * TILE:(i + 1) * TILE], bd_half, preferred_element_type=F32)
             for i in range(nt)], axis=-1)

    for ci in range(tt // c):
        rows = pl.ds(ci * c, c)

        def prepare(g, carry):
            seqs = [g * group + s for s in range(group)]
            lws = [lw_ref[b, rows, :] for b in seqs]
            lincs = []
            for lw in lws:
                hi = lw.astype(BF16)
                rem = lw - hi.astype(F32)
                mid = rem.astype(BF16)
                lo = (rem - mid.astype(F32)).astype(BF16)
                lincs.append(jnp.dot(tri, hi, preferred_element_type=F32)
                             + jnp.dot(tri, mid, preferred_element_type=F32)
                             + jnp.dot(tri, lo, preferred_element_type=F32))
            scaled = []
            for b, lw, linc in zip(seqs, lws, lincs):
                lc = linc[c - 1:c]
                e_neg = jnp.exp(-linc)
                e_rem = jnp.exp(lc - linc)
                al, be, r, k, v = (ref[b, rows, :] for ref in (al_ref, be_ref, r_ref, k_ref, v_ref))
                at = al * jnp.exp(linc - lw)
                rt = r * jnp.exp(linc)
                bh = be * e_neg
                kh = k * e_neg
                at_ref[b] = at
                bh_ref[b] = bh
                ec_ref[b] = jnp.exp(lc)
                bk_ref[b] = jnp.concatenate([be * e_rem, k * e_rem], axis=0).astype(BF16)
                scaled.append((at, rt, bh, kh, v))
            products = []
            for b, (at, rt, bh, kh, v) in zip(seqs, scaled):
                for i in range(nt):
                    ls = slice(i * TILE, (i + 1) * TILE)
                    x = jnp.concatenate([at[:, ls], rt[:, ls]], axis=0).astype(BF16)
                    a_k = _contract_last(x, block_diag(kh[:, ls]))
                    a_b = _contract_last(x, block_diag(bh[:, ls]))
                    x_p = _contract_last(x, p_ref[b, i].astype(BF16))
                    products.append((b, i, ls, a_k, a_b, x_p, block_diag(v[:, ls])))
            for b, i, ls, a_k, a_b, x_p, w_v in products:
                u_ref[b, :, ls] = x_p[0:c] + jnp.dot(
                    jnp.where(strict, a_k[0:c], 0.0).astype(BF16), w_v, preferred_element_type=F32)
                y_ref[b, rows, ls] = x_p[c:2 * c] + jnp.dot(
                    jnp.where(incl, a_k[c:2 * c], 0.0).astype(BF16), w_v, preferred_element_type=F32)
                aab_ref[b, i] = jnp.where(strict, a_b[0:c], 0.0).astype(BF16)
                arb_ref[b, i] = jnp.where(incl, a_b[c:2 * c], 0.0).astype(BF16)
            return carry

        lax.fori_loop(0, nb // group, prepare, 0)
        z_ref[...] = jnp.zeros(z_ref.shape, F32)

        for sb in range(c // SUB_BLOCK):
            srows = pl.ds(sb * SUB_BLOCK, SUB_BLOCK)

            sq = SUB_BLOCK * SUB_BLOCK
            lhs = [(bh_ref[b, srows, :][:, None, :] * at_ref[b, srows, :][None, :, :]).reshape(sq, R_WIDTH)
                   for b in range(nb)]
            coef = head_sums(jnp.concatenate(lhs, axis=0))
            for b in range(nb):
                coef_ref[:, b * SUB_BLOCK:(b + 1) * SUB_BLOCK, :] = jnp.where(
                    later, coef[b * sq:(b + 1) * sq].reshape(SUB_BLOCK, SUB_BLOCK, R_WIDTH), 0.0)
            if sb:
                for b in range(nb):
                    z = z_ref[b]
                    for i in range(nt):
                        ls = slice(i * TILE, (i + 1) * TILE)
                        u_ref[b, srows, ls] += jnp.dot(aab_ref[b, i, srows, :], block_diag(z[:, ls]),
                                                       preferred_element_type=F32)

            for b in range(nb):
                us_refs[b][...] = u_ref[b, srows, :]

            def substitute(j, carry):
                coef = coef_ref[j]
                for b in range(nb):
                    z_j = us_refs[b][pl.ds(j, 1), :]
                    us_refs[b][...] += coef[b * SUB_BLOCK:(b + 1) * SUB_BLOCK] * z_j
                return carry

            lax.fori_loop(0, SUB_BLOCK - 1, substitute, 0)
            for b in range(nb):
                z_ref[b, srows, :] = us_refs[b][...]

        tiles = [(b, i, slice(i * TILE, (i + 1) * TILE)) for b in range(nb) for i in range(nt)]
        for b, i, ls in tiles:
            y_ref[b, rows, ls] += jnp.dot(arb_ref[b, i], block_diag(z_ref[b, :, ls]),
                                          preferred_element_type=F32)
        for b, i, ls in tiles:
            zv_t = jnp.concatenate([z_ref[b, :, ls], v_ref[b, rows, ls]], axis=0).T.astype(BF16)
            upd = jnp.dot(zv_t, bk_ref[b, :, ls], preferred_element_type=F32)
            p_ref[b, i] = p_ref[b, i] * ec_ref[b, :, ls] + jnp.where(block_mask, upd, 0.0)

    def head_means(x):
        return jnp.concatenate(
            [_split_dot(x[:, i * TILE:(i + 1) * TILE], bd_half) for i in range(nt)], axis=-1) * (1.0 / n)

    y = y_ref[...].reshape(nb * tt, R_WIDTH)
    d = y - head_means(y)
    yn = d * lax.rsqrt(head_means(d * d) + LNX_EPS) * lg_ref[...] + lb_ref[...]
    out = (yn + bonus_ref[...].reshape(nb * tt, R_WIDTH)) * g_ref[...].reshape(nb * tt, R_WIDTH)
    o_ref[...] = out.reshape(nb, tt, R_WIDTH).astype(o_ref.dtype)

    @pl.when(j == pl.num_programs(1) - 1)
    def _():
        for b in range(nb):
            for i in range(nt):
                p = p_ref[b, i]
                for hh in range(TILE // n):
                    sout_ref[b, i * (TILE // n) + hh] = p[hh * n:(hh + 1) * n, hh * n:(hh + 1) * n]


def _rwkv_scan(r, lw, k, v, al, be, g, bonus, s0, lg, lb, bd, nb, tt):
    b, t, _ = r.shape
    c = SCAN_CHUNK
    row = pl.BlockSpec((nb, tt, R_WIDTH), lambda i, j: (i, j, 0))
    st = pl.BlockSpec((nb, R_HEADS, R_HEAD_DIM, R_HEAD_DIM), lambda i, j: (i, 0, 0, 0))
    nt = R_WIDTH // TILE
    return pl.pallas_call(
        functools.partial(_rwkv_scan_kernel, nb=nb, tt=tt),
        grid=(b // nb, t // tt),
        in_specs=[row] * 8 + [st, _const_spec((1, R_WIDTH)), _const_spec((1, R_WIDTH)),
                              _const_spec((R_WIDTH, R_WIDTH))],
        out_specs=[row, st],
        out_shape=[jax.ShapeDtypeStruct((b, t, R_WIDTH), BF16),
                   jax.ShapeDtypeStruct((b, R_HEADS, R_HEAD_DIM, R_HEAD_DIM), F32)],
        scratch_shapes=[pltpu.VMEM((nb, nt, TILE, TILE), F32),
                        pltpu.VMEM((nb, c, R_WIDTH), F32),
                        pltpu.VMEM((nb, c, R_WIDTH), F32),
                        pltpu.VMEM((nb, c, R_WIDTH), F32),
                        pltpu.VMEM((nb, c, R_WIDTH), F32),
                        pltpu.VMEM((nb, tt, R_WIDTH), F32),
                        pltpu.VMEM((nb, nt, c, TILE), BF16),
                        pltpu.VMEM((nb, nt, c, TILE), BF16),
                        pltpu.VMEM((nb, 2 * c, R_WIDTH), BF16),
                        pltpu.VMEM((nb, 1, R_WIDTH), F32),
                        pltpu.VMEM((SUB_BLOCK, nb * SUB_BLOCK, R_WIDTH), F32)]
                       + [pltpu.VMEM((SUB_BLOCK, R_WIDTH), F32)] * nb,
        compiler_params=_cparams(("parallel", "arbitrary")),
        name="rwkv_scan",
    )(r, lw, k, v, al, be, g, bonus, s0, lg, lb, bd)


def _out_ffn_kernel(x_ref, oa_ref, orw_ref, mod_ref, ng_ref, wout_ref, wg_ref, wu_ref, wd_ref, o_ref):
    gb, tr, _ = x_ref.shape
    rows = gb * tr
    x = x_ref[...]
    mod = mod_ref[...]
    gt_a = mod[:, :, 2 * D_MODEL:3 * D_MODEL]
    sh_f = mod[:, :, 3 * D_MODEL:4 * D_MODEL]
    sc_f = mod[:, :, 4 * D_MODEL:5 * D_MODEL]
    gt_f = mod[:, :, 5 * D_MODEL:6 * D_MODEL]
    half = A_HEADS * V_DIM
    m = (jnp.dot(oa_ref[...].reshape(rows, half), wout_ref[0:half, :], preferred_element_type=F32)
         + jnp.dot(orw_ref[...].reshape(rows, R_WIDTH), wout_ref[half:D_MIX, :],
                   preferred_element_type=F32))
    x = x + gt_a * _rms(m.reshape(gb, tr, D_MODEL), ng_ref[1:2, :])
    h = (_rms(x, ng_ref[2:3, :]) * (1.0 + sc_f) + sh_f).reshape(rows, D_MODEL).astype(BF16)

    def gate_up(c):
        cols = slice(c * FFN_CHUNK, (c + 1) * FFN_CHUNK)
        return (jnp.dot(h, wg_ref[:, cols], preferred_element_type=F32),
                jnp.dot(h, wu_ref[:, cols], preferred_element_type=F32))

    f = jnp.zeros((rows, D_MODEL), F32)
    n_chunks = D_FF // FFN_CHUNK
    pending = [gate_up(c) for c in range(min(FFN_LOOKAHEAD, n_chunks))]
    for c in range(n_chunks):
        if c + FFN_LOOKAHEAD < n_chunks:
            pending.append(gate_up(c + FFN_LOOKAHEAD))
        gate, up = pending.pop(0)
        act = (gate * _sigmoid(gate) * up).astype(BF16)
        f = f + jnp.dot(act, wd_ref[c * FFN_CHUNK:(c + 1) * FFN_CHUNK, :], preferred_element_type=F32)
    o_ref[...] = x + gt_f * _rms(f.reshape(gb, tr, D_MODEL), ng_ref[3:4, :])


def _out_ffn(x, o_attn, o_rw, mod, ng, wout, wg, wu, wd, layer, gb, tr):
    b, t, _ = x.shape
    row = lambda w: pl.BlockSpec((gb, tr, w), lambda i, j: (i, j, 0))
    wspec = lambda w: _layer_spec(w, layer, pipeline_mode=pl.Buffered(1))
    return pl.pallas_call(
        _out_ffn_kernel,
        grid=(b // gb, t // tr),
        in_specs=[
            row(D_MODEL), row(A_HEADS * V_DIM), row(R_WIDTH),
            pl.BlockSpec((gb, 1, 6 * D_MODEL), lambda i, j: (i, 0, 0)),
            _const_spec((4, D_MODEL)),
            wspec(wout), wspec(wg), wspec(wu), wspec(wd),
        ],
        out_specs=row(D_MODEL),
        out_shape=jax.ShapeDtypeStruct((b, t, D_MODEL), F32),
        compiler_params=_cparams(("parallel", "parallel")),
        name="out_ffn",
    )(x, o_attn, o_rw, mod, ng, wout, wg, wu, wd)


def _rope_tables(pos):
    inv = ROPE_THETA ** (-jnp.arange(0, ROPE_DIM, 2, dtype=F32) / ROPE_DIM)
    ang = pos.astype(F32)[:, None] * inv[None, :]
    cos, sin = jnp.cos(ang), jnp.sin(ang)
    cc = jnp.concatenate([cos, cos], axis=-1)
    ss = jnp.concatenate([-sin, sin], axis=-1)
    return cc, ss, jnp.tile(cc, (1, A_HEADS)).T, jnp.tile(ss, (1, A_HEADS)).T


def _pack_weights(w_in, w_uq, w_uk, w_uv):
    depth = w_in.shape[0]
    half = ROPE_DIM // 2
    o = Q_LORA + KV_LORA
    k_r = w_in[:, :, o:o + ROPE_DIM]
    k_rs = jnp.concatenate([k_r[..., half:], k_r[..., :half]], axis=-1)
    pad = jnp.zeros((depth, D_MODEL, LANE - 2 * ROPE_DIM), w_in.dtype)
    win = jnp.concatenate([w_in[:, :, :o], k_r, k_rs, pad, w_in[:, :, MLA_COLS:]], axis=-1)
    uq = w_uq.reshape(depth, Q_LORA, A_HEADS, NOPE_DIM + ROPE_DIM)
    nope = uq[..., :NOPE_DIM].reshape(depth, Q_LORA, A_HEADS * NOPE_DIM)
    rope = uq[..., NOPE_DIM:]
    rope_s = jnp.concatenate([rope[..., half:], rope[..., :half]], axis=-1)
    wuq = jnp.concatenate([nope, rope.reshape(depth, Q_LORA, -1), rope_s.reshape(depth, Q_LORA, -1)], axis=-1)
    wuq = jnp.transpose(wuq, (0, 2, 1))
    wuk = jnp.transpose(w_uk, (0, 2, 1, 3))
    wuv = jnp.transpose(w_uv, (0, 2, 3, 1))
    return win.astype(BF16), wuq.astype(BF16), wuk.astype(BF16), wuv.astype(BF16)


def _block_diag_ones():
    hid = jnp.arange(R_WIDTH) // R_HEAD_DIM
    bd = (hid[:, None] == hid[None, :]).astype(BF16)
    return bd


def _row_tile(t, pref):
    return pref if t % pref == 0 else t


def kernel(x_prompt, x_sample, c_prompt, c_sample, cache_ckv, cache_krope, state_wkv, state_shift,
           ada_w, ada_b, norm_g, w_in, q_norm_g, kv_norm_g, w_uq, w_uk, w_uv,
           rw_mu, rw_w0, rw_w2, rw_a0, rw_a2, rw_g2, rw_kk, rw_ka, rw_rk, lnx_g, lnx_b,
           w_out, w_gate, w_up, w_down):
    depth = w_in.shape[0]
    bp, tp, _ = x_prompt.shape
    bs, ts, _ = x_sample.shape
    past = cache_ckv.shape[2]

    mod_all = _modulation(jnp.concatenate([c_prompt, c_sample], axis=0), ada_w, ada_b)
    win, wuq, wuk, wuv = _pack_weights(w_in, w_uq, w_uk, w_uv)
    wout, wg, wu, wd = (w.astype(BF16) for w in (w_out, w_gate, w_up, w_down))
    w2, a2, g2 = (w.astype(BF16) for w in (rw_w2, rw_a2, rw_g2))
    bd = _block_diag_ones()
    tabs_p = _rope_tables(jnp.arange(tp))
    tabs_s = _rope_tables(past + jnp.arange(ts))
    cache_krope_t = jnp.swapaxes(cache_krope, 2, 3)
    vec = lambda a, l: a[l].reshape(1, -1)

    def layer(l, x, mod, tabs, s0, shift0, cache):
        b, t, _ = x.shape
        tr = _row_tile(t, IN_ROWS)
        gb = max(1, min(b, IN_ROWS // t))
        gb = gb if b % gb == 0 else 1
        tabs = tuple(jnp.tile(tb, (gb, 1) if tb.shape[0] == t else (1, gb)) for tb in tabs)
        tq = _row_tile(t, COLS)
        rw_params = (vec(rw_mu, l), vec(rw_w0, l), w2[l], vec(rw_a0, l), a2[l], g2[l],
                     vec(rw_kk, l), vec(rw_ka, l), rw_rk[l].reshape(1, -1), bd)
        ckv, krope, kcat, vt, qt, r, w, k, v, al, be, g, bonus, shift_new = _in_proj(
            x, mod, vec(norm_g[:, 0], l), win, vec(q_norm_g, l), vec(kv_norm_g, l),
            wuq, wuk, *tabs, shift0, rw_params, l, gb, tr, tq)
        if cache is None:
            o_attn = _prompt_attention(qt, kcat, vt, wuv, l, tq)
        else:
            o_attn = _sample_attention(qt, kcat, vt, cache[0], cache[1], l, wuv, min(2048, past))
        ts_ = SCAN_CHUNK
        o_rw, s_new = _rwkv_scan(r, w, k, v, al, be, g, bonus, s0, vec(lnx_g, l), vec(lnx_b, l),
                                 bd, 8 if b % 8 == 0 else b, ts_)
        gb = max(1, min(b, FFN_ROWS // t))
        x_new = _out_ffn(x, o_attn, o_rw, mod, norm_g[l], wout, wg, wu, wd, l,
                         gb if b % gb == 0 else 1, min(t, FFN_ROWS))
        return x_new, ckv, krope, s_new, shift_new[:, 0]

    xp, xs = x_prompt, x_sample
    wkv_zero = jnp.zeros((bp, R_HEADS, R_HEAD_DIM, R_HEAD_DIM), F32)
    shift_zero = jnp.zeros((bp, 1, RW_COLS), F32)
    outs_p, outs_s = [], []
    for l in range(depth):
        mod_p = mod_all[l, :bp].reshape(bp, 1, 6 * D_MODEL)
        mod_s = mod_all[l, bp:].reshape(bs, 1, 6 * D_MODEL)
        xp, *rest_p = layer(l, xp, mod_p, tabs_p, wkv_zero, shift_zero, None)
        xs, *rest_s = layer(l, xs, mod_s, tabs_s, state_wkv[l],
                            state_shift[l].reshape(bs, 1, RW_COLS), (cache_ckv, cache_krope_t))
        outs_p.append(rest_p)
        outs_s.append(rest_s)
    stack = lambda outs, i: jnp.stack([o[i] for o in outs])
    return (xp, xs,
            stack(outs_p, 0), stack(outs_p, 1), stack(outs_p, 2), stack(outs_p, 3),
            stack(outs_s, 0), stack(outs_s, 1), stack(outs_s, 2), stack(outs_s, 3))
```

```python
import functools

import jax
import jax.numpy as jnp
from jax import lax
from jax.experimental import pallas as pl
from jax.experimental.pallas import tpu as pltpu

D_MODEL = 1024
CHUNK = 64
A_HEADS = 8
NOPE_DIM = 64
ROPE_DIM = 32
V_DIM = 64
Q_LORA = 384
KV_LORA = 256
ROPE_THETA = 10000.0
SM_SCALE = (NOPE_DIM + ROPE_DIM) ** -0.5
Q_SCALE = SM_SCALE * 1.4426950408889634
R_HEADS = 8
R_HEAD_DIM = 64
R_WIDTH = R_HEADS * R_HEAD_DIM
DECAY_LORA = 64
AAA_LORA = 64
GATE_LORA = 128
RW_COLS = 3 * R_WIDTH + DECAY_LORA + AAA_LORA + GATE_LORA
MLA_COLS = Q_LORA + KV_LORA + ROPE_DIM
D_MIX = A_HEADS * V_DIM + R_WIDTH
D_FF = 2816
RMS_EPS = 1e-6
LNX_EPS = 64e-5

LANE = 128
QK_WIDTH = KV_LORA + LANE
RW_OFF = Q_LORA + KV_LORA + LANE
IN_PACKED = RW_OFF + RW_COLS
VMEM_LIMIT = 56 * 1024 * 1024
IN_ROWS = 512
FFN_ROWS = 512
FFN_CHUNK = 256
FFN_LOOKAHEAD = 2

BF16 = jnp.bfloat16
F32 = jnp.float32


def _cparams(sem):
    return pltpu.CompilerParams(dimension_semantics=sem, vmem_limit_bytes=VMEM_LIMIT)


def _const_spec(shape):
    n = len(shape)
    return pl.BlockSpec(shape, lambda *_: (0,) * n)


def _layer_spec(stacked, layer, **kwargs):
    shape = stacked.shape[1:]
    zeros = (0,) * len(shape)
    return pl.BlockSpec((None,) + shape, lambda *_: (layer,) + zeros, **kwargs)


def _bdot(a, b):
    return jnp.dot(a.astype(BF16), b.astype(BF16), preferred_element_type=F32)


def _split_dot(a, b):
    hi = a.astype(BF16)
    lo = (a - hi.astype(F32)).astype(BF16)
    return (jnp.dot(hi, b, preferred_element_type=F32)
            + jnp.dot(lo, b, preferred_element_type=F32))


def _rms(x, g):
    return x * lax.rsqrt(jnp.mean(x * x, axis=-1, keepdims=True) + RMS_EPS) * g


def _sigmoid(x):
    return 1.0 / (1.0 + jnp.exp(-x))


def _mod_kernel(c_ref, w_ref, b_ref, o_ref):
    c = c_ref[...]
    s = c * _sigmoid(c)
    o_ref[0] = _bdot(s, w_ref[0]) + b_ref[0]


def _modulation(c_all, ada_w, ada_b):
    depth = ada_w.shape[0]
    n = c_all.shape[0]
    tn = 1024
    return pl.pallas_call(
        _mod_kernel,
        grid=(depth, 6 * D_MODEL // tn),
        in_specs=[
            pl.BlockSpec((n, D_MODEL), lambda l, j: (0, 0)),
            pl.BlockSpec((1, D_MODEL, tn), lambda l, j: (l, 0, j)),
            pl.BlockSpec((1, 1, tn), lambda l, j: (l, 0, j)),
        ],
        out_specs=pl.BlockSpec((1, n, tn), lambda l, j: (l, 0, j)),
        out_shape=jax.ShapeDtypeStruct((depth, n, 6 * D_MODEL), F32),
        compiler_params=_cparams(("parallel", "parallel")),
        name="adaln_mod",
    )(c_all, ada_w, ada_b.reshape(depth, 1, 6 * D_MODEL))


def _in_proj_kernel(x_ref, mod_ref, g0_ref, win_ref, qg_ref, kvg_ref, wuq_ref, wuk_ref,
                    wkx_ref, sel_ref, cck_ref, ssk_ref, ccq_ref, ssq_ref, shift_ref, *rest, tq, per_head_keys):
    rw_params = rest[0:10]
    ckv_ref, krope_ref, kmat_ref, vt_ref, qt_ref = rest[10:15]
    feat_refs = rest[15:23]
    shift_out_ref, prev_ref = rest[23:25]
    gb, tr, _ = x_ref.shape
    tm = gb * tr
    mod = mod_ref[...]
    sh = mod[:, :, 0:D_MODEL]
    sc = mod[:, :, D_MODEL:2 * D_MODEL]
    h = (_rms(x_ref[...], g0_ref[...]) * (1.0 + sc) + sh).reshape(tm, D_MODEL).astype(BF16)
    proj = jnp.dot(h, win_ref[:, 0:RW_OFF], preferred_element_type=F32)
    rw = jnp.dot(h, win_ref[:, RW_OFF:IN_PACKED], preferred_element_type=F32)
    c_q = proj[:, 0:Q_LORA]
    c_kv = proj[:, Q_LORA:Q_LORA + KV_LORA]
    k_r = proj[:, Q_LORA + KV_LORA:Q_LORA + KV_LORA + ROPE_DIM]
    k_rs = proj[:, Q_LORA + KV_LORA + ROPE_DIM:Q_LORA + KV_LORA + 2 * ROPE_DIM]

    ckv = _rms(c_kv, kvg_ref[...])
    krope = k_r * cck_ref[...] + k_rs * ssk_ref[...]
    ckv_ref[...] = ckv.reshape(gb, tr, KV_LORA)
    krope_ref[...] = krope.reshape(gb, tr, ROPE_DIM)
    if per_head_keys:
        kmat = (jnp.dot(ckv.astype(BF16), wkx_ref[...], preferred_element_type=F32)
                + jnp.dot(krope.astype(BF16), sel_ref[...], preferred_element_type=F32))
        kmat_ref[...] = kmat.astype(BF16).reshape(gb, tr, A_HEADS * LANE)
    else:
        zpad = jnp.zeros((tm, LANE - ROPE_DIM), F32)
        kmat_ref[...] = jnp.concatenate([ckv, krope, zpad], axis=-1).astype(BF16).reshape(gb, tr, QK_WIDTH)
    ckv_t = ckv.T.astype(BF16)
    for s in range(gb):
        vt_ref[s] = ckv_t[:, s * tr:(s + 1) * tr]

    cqn_t = _rms(c_q, qg_ref[...]).T.astype(BF16)
    q_t = jnp.dot(wuq_ref[...], cqn_t, preferred_element_type=F32)
    nope_w = A_HEADS * NOPE_DIM
    rope_w = A_HEADS * ROPE_DIM
    q_rope = (q_t[nope_w:nope_w + rope_w] * ccq_ref[...]
              + q_t[nope_w + rope_w:nope_w + 2 * rope_w] * ssq_ref[...])
    for hd in range(A_HEADS):
        q_nope = q_t[hd * NOPE_DIM:(hd + 1) * NOPE_DIM]
        qr = q_rope[hd * ROPE_DIM:(hd + 1) * ROPE_DIM]
        if per_head_keys:
            parts = [q_nope, qr, jnp.zeros((LANE - NOPE_DIM - ROPE_DIM, tm), F32)]
        else:
            q_lat = jnp.dot(wuk_ref[hd], q_nope.astype(BF16), preferred_element_type=F32)
            parts = [q_lat, qr, jnp.zeros((LANE - ROPE_DIM, tm), F32)]
        qh = (jnp.concatenate(parts, axis=0) * Q_SCALE).astype(BF16)
        nqb = tr // tq
        for s in range(gb):
            for qb in range(nqb):
                col = (s * nqb + qb) * tq
                qt_ref[s, qb, :, hd * tq:(hd + 1) * tq] = qh[:, col:col + tq]

    first = jnp.where(pl.program_id(1) == 0, shift_ref[...], prev_ref[...][None])
    rw3 = rw.reshape(gb, tr, RW_COLS)
    row_id = lax.broadcasted_iota(jnp.int32, (gb, tr, 1), 1)
    rw_prev = jnp.where(row_id == 0, first, pltpu.roll(rw3, 1, 1)).reshape(tm, RW_COLS)
    for ref, val in zip(feat_refs, _rwkv_features(rw, rw_prev, *rw_params)):
        ref[...] = val.reshape(gb, tr, R_WIDTH)
    prev_ref[...] = rw[tm - 1:tm]
    shift_out_ref[...] = rw3[:, tr - 1:tr, :]


def _in_proj(x, mod, g0, win, qg, kvg, wuq, wuk, wkx, sel, cck, ssk, ccq, ssq, shift0, rw_params,
             layer, gb, tr, tq, per_head_keys):
    kw = A_HEADS * LANE if per_head_keys else QK_WIDTH
    qw = LANE if per_head_keys else QK_WIDTH
    b, t, _ = x.shape
    tm = gb * tr
    grid = (b // gb, t // tr)
    row = lambda w: pl.BlockSpec((gb, tr, w), lambda i, j: (i, j, 0))
    tab = lambda w: pl.BlockSpec((tm, w), lambda i, j: (j, 0))
    tab_t = pl.BlockSpec((A_HEADS * ROPE_DIM, tm), lambda i, j: (0, j))
    nqb = tr // tq
    return pl.pallas_call(
        functools.partial(_in_proj_kernel, tq=tq, per_head_keys=per_head_keys),
        grid=grid,
        in_specs=[
            row(D_MODEL),
            pl.BlockSpec((gb, 1, 6 * D_MODEL), lambda i, j: (i, 0, 0)),
            _const_spec((1, D_MODEL)),
            _layer_spec(win, layer),
            _const_spec((1, Q_LORA)),
            _const_spec((1, KV_LORA)),
            _layer_spec(wuq, layer),
            _layer_spec(wuk, layer),
            _layer_spec(wkx, layer),
            _const_spec(sel.shape),
            tab(ROPE_DIM), tab(ROPE_DIM), tab_t, tab_t,
            pl.BlockSpec((gb, 1, RW_COLS), lambda i, j: (i, 0, 0)),
        ] + [_const_spec(p.shape) for p in rw_params],
        out_specs=[
            row(KV_LORA), row(ROPE_DIM), row(kw),
            pl.BlockSpec((gb, KV_LORA, tr), lambda i, j: (i, 0, j)),
            pl.BlockSpec((gb, nqb, qw, A_HEADS * tq), lambda i, j: (i, j, 0, 0)),
        ] + [row(R_WIDTH)] * 8 + [pl.BlockSpec((gb, 1, RW_COLS), lambda i, j: (i, 0, 0))],
        out_shape=[
            jax.ShapeDtypeStruct((b, t, KV_LORA), F32),
            jax.ShapeDtypeStruct((b, t, ROPE_DIM), F32),
            jax.ShapeDtypeStruct((b, t, kw), BF16),
            jax.ShapeDtypeStruct((b, KV_LORA, t), BF16),
            jax.ShapeDtypeStruct((b, t // tq, qw, A_HEADS * tq), BF16),
        ] + [jax.ShapeDtypeStruct((b, t, R_WIDTH), F32)] * 8 + [jax.ShapeDtypeStruct((b, 1, RW_COLS), F32)],
        scratch_shapes=[pltpu.VMEM((1, RW_COLS), F32)],
        compiler_params=_cparams(("parallel", "arbitrary")),
        name="in_proj",
    )(x, mod, g0, win, qg, kvg, wuq, wuk, wkx, sel, cck, ssk, ccq, ssq, shift0, *rw_params)


COLS = 2 * LANE
SCORE_LOOKAHEAD = 4
SAMPLE_SUB_KEYS = 512


def _softmax_update(s, vt, c, m_ref, l_ref, acc_ref):
    m_prev = m_ref[c]
    m_new = jnp.maximum(m_prev, jnp.max(s, axis=0, keepdims=True))
    alpha = jnp.exp2(m_prev - m_new)
    p = jnp.exp2(s - m_new)
    l_ref[c] = alpha * l_ref[c] + jnp.sum(p, axis=0, keepdims=True)
    acc_ref[c] = alpha * acc_ref[c] + jnp.dot(vt, p.astype(BF16), preferred_element_type=F32)
    m_ref[c] = m_new


def _pipelined_updates(units, m_ref, l_ref, acc_ref):
    n = len(units)
    pending = [units[u][0]() for u in range(min(SCORE_LOOKAHEAD, n))]
    for u in range(n):
        if u + SCORE_LOOKAHEAD < n:
            pending.append(units[u + SCORE_LOOKAHEAD][0]())
        _softmax_update(pending.pop(0), units[u][1], units[u][2], m_ref, l_ref, acc_ref)


def _pipelined_chunks(scores, vt, nc, m_ref, l_ref, acc_ref):
    _pipelined_updates([(functools.partial(scores, c), vt, c) for c in range(nc)], m_ref, l_ref, acc_ref)


def _attn_init(m_ref, l_ref, acc_ref):
    m_ref[...] = jnp.full(m_ref.shape, -jnp.inf, F32)
    l_ref[...] = jnp.zeros(l_ref.shape, F32)
    acc_ref[...] = jnp.zeros(acc_ref.shape, F32)


def _attn_finish(l_ref, acc_ref, wuv_ref, o_ref, tq):
    outs = []
    for hd in range(A_HEADS):
        c, off = divmod(hd * tq, COLS)
        o_lat = acc_ref[c, :, off:off + tq] / l_ref[c, :, off:off + tq]
        outs.append(jnp.dot(wuv_ref[hd], o_lat.astype(BF16), preferred_element_type=F32))
    o_ref[0] = jnp.concatenate(outs, axis=0).T.astype(o_ref.dtype)


def _attn_scratch(tq):
    nc = A_HEADS * tq // COLS
    return [pltpu.VMEM((nc, 1, COLS), F32), pltpu.VMEM((nc, 1, COLS), F32),
            pltpu.VMEM((nc, KV_LORA, COLS), F32)]


def _prompt_attn_kernel(q_ref, k_ref, vt_ref, wuv_ref, o_ref, m_ref, l_ref, acc_ref, *, tq):
    i = pl.program_id(1)
    nc = A_HEADS * tq // COLS
    _attn_init(m_ref, l_ref, acc_ref)

    def tile(j, n_tiles, bias):
        off = pl.multiple_of(j * tq, tq)
        vt = vt_ref[0, :, pl.ds(off, n_tiles * tq)]

        def scores(c):
            k = k_ref[0, pl.ds(off, n_tiles * tq), c * LANE:(c + 1) * LANE]
            s = jnp.dot(k, q_ref[0, 0, :, c * COLS:(c + 1) * COLS], preferred_element_type=F32)
            return s if bias is None else s + bias

        _pipelined_chunks(scores, vt, nc, m_ref, l_ref, acc_ref)

    def tile_pair(p, carry):
        tile(2 * p, 2, None)
        return carry

    lax.fori_loop(0, i // 2, tile_pair, 0)

    @pl.when(i % 2 == 1)
    def _():
        tile(i - 1, 1, None)

    k_chunk = lax.broadcasted_iota(jnp.int32, (tq, COLS), 0) // CHUNK
    q_chunk = (lax.broadcasted_iota(jnp.int32, (tq, COLS), 1) % tq) // CHUNK
    tile(i, 1, jnp.where(k_chunk <= q_chunk, 0.0, -jnp.inf).astype(F32))
    _attn_finish(l_ref, acc_ref, wuv_ref, o_ref, tq)


def _prompt_attention(qt, kmat, vt, wuv, layer, tq):
    b, t, _ = kmat.shape
    assert tq == COLS, "one softmax column chunk per head"
    return pl.pallas_call(
        functools.partial(_prompt_attn_kernel, tq=tq),
        grid=(b, t // tq),
        in_specs=[
            pl.BlockSpec((1, 1, LANE, A_HEADS * tq), lambda i, j: (i, j, 0, 0)),
            pl.BlockSpec((1, t, A_HEADS * LANE), lambda i, j: (i, 0, 0)),
            pl.BlockSpec((1, KV_LORA, t), lambda i, j: (i, 0, 0)),
            _layer_spec(wuv, layer),
        ],
        out_specs=pl.BlockSpec((1, tq, A_HEADS * V_DIM), lambda i, j: (i, j, 0)),
        out_shape=jax.ShapeDtypeStruct((b, t, A_HEADS * V_DIM), BF16),
        scratch_shapes=_attn_scratch(tq),
        compiler_params=_cparams(("parallel", "arbitrary")),
        name="prompt_attn",
    )(qt, kmat, vt, wuv)


def _sample_attn_kernel(q_ref, kn_ref, vtn_ref, cckv_ref, ckr_ref, wuv_ref, o_ref,
                        m_ref, l_ref, acc_ref, *, tq):
    j = pl.program_id(1)
    nc = A_HEADS * tq // COLS

    @pl.when(j == 0)
    def _():
        _attn_init(m_ref, l_ref, acc_ref)

    def cache_scores(ckv_b, kr, c):
        cols = slice(c * COLS, (c + 1) * COLS)
        return (jnp.dot(ckv_b, q_ref[0, 0, 0:KV_LORA, cols], preferred_element_type=F32)
                + jnp.dot(kr, q_ref[0, 0, KV_LORA:KV_LORA + ROPE_DIM, cols], preferred_element_type=F32))

    units = []
    for sub in range(cckv_ref.shape[2] // SAMPLE_SUB_KEYS):
        keys = slice(sub * SAMPLE_SUB_KEYS, (sub + 1) * SAMPLE_SUB_KEYS)
        ckv = cckv_ref[0, 0, keys, :]
        ckv_b = ckv.astype(BF16)
        vt = ckv.T.astype(BF16)
        kr = ckr_ref[0, 0, :, keys].T.astype(BF16)
        units += [(functools.partial(cache_scores, ckv_b, kr, c), vt, c) for c in range(nc)]
    _pipelined_updates(units, m_ref, l_ref, acc_ref)

    @pl.when(j == pl.num_programs(1) - 1)
    def _():
        kn = kn_ref[0]
        new_scores = lambda c: jnp.dot(kn, q_ref[0, 0, :, c * COLS:(c + 1) * COLS],
                                       preferred_element_type=F32)
        _pipelined_chunks(new_scores, vtn_ref[0], nc, m_ref, l_ref, acc_ref)
        _attn_finish(l_ref, acc_ref, wuv_ref, o_ref, tq)


def _sample_attention(qt, kcat, vt, cache_ckv, cache_krope_t, layer, wuv, tk):
    b, tq, _ = kcat.shape
    past = cache_ckv.shape[2]
    return pl.pallas_call(
        functools.partial(_sample_attn_kernel, tq=tq),
        grid=(b, past // tk),
        in_specs=[
            pl.BlockSpec((1, 1, QK_WIDTH, A_HEADS * tq), lambda i, j: (i, 0, 0, 0)),
            pl.BlockSpec((1, tq, QK_WIDTH), lambda i, j: (i, 0, 0)),
            pl.BlockSpec((1, KV_LORA, tq), lambda i, j: (i, 0, 0)),
            pl.BlockSpec((1, 1, tk, KV_LORA), lambda i, j: (layer, i, j, 0)),
            pl.BlockSpec((1, 1, ROPE_DIM, tk), lambda i, j: (layer, i, 0, j)),
            _layer_spec(wuv, layer),
        ],
        out_specs=pl.BlockSpec((1, tq, A_HEADS * V_DIM), lambda i, j: (i, 0, 0)),
        out_shape=jax.ShapeDtypeStruct((b, tq, A_HEADS * V_DIM), BF16),
        scratch_shapes=_attn_scratch(tq),
        compiler_params=_cparams(("parallel", "arbitrary")),
        name="sample_attn",
    )(qt, kcat, vt, cache_ckv, cache_krope_t, wuv)


def _rwkv_features(rw, rw_prev, mu_ref, w0_ref, w2_ref, a0_ref, a2_ref, g2_ref, kkw_ref, ka_ref, rk_ref,
                   bd_ref):
    mix = rw + (rw_prev - rw) * mu_ref[...]
    r = mix[:, 0:R_WIDTH]
    k = mix[:, R_WIDTH:2 * R_WIDTH]
    v = mix[:, 2 * R_WIDTH:3 * R_WIDTH]
    o = 3 * R_WIDTH
    w_lo = mix[:, o:o + DECAY_LORA]
    a_lo = mix[:, o + DECAY_LORA:o + DECAY_LORA + AAA_LORA]
    g_lo = mix[:, o + DECAY_LORA + AAA_LORA:RW_COLS]
    w_raw = w0_ref[...] + _bdot(jnp.tanh(w_lo), w2_ref[...])
    nw = -w_raw
    softplus = jnp.maximum(nw, 0.0) + jnp.log(1.0 + jnp.exp(-jnp.abs(nw)))
    log_decay = -jnp.exp(-softplus - 0.5)
    a = _sigmoid(a0_ref[...] + _bdot(a_lo, a2_ref[...]))
    g = _bdot(_sigmoid(g_lo), g2_ref[...])
    bd_half = bd_ref[0:TILE, 0:TILE]

    def head_sums(x):
        return jnp.concatenate([_split_dot(x[:, i * TILE:(i + 1) * TILE], bd_half)
                                for i in range(R_WIDTH // TILE)], axis=-1)

    kk = k * kkw_ref[...]
    kk = kk / jnp.maximum(jnp.sqrt(head_sums(kk * kk)), 1e-12)
    k2 = k * (1.0 + (a - 1.0) * ka_ref[...])
    bonus = head_sums(r * k2 * rk_ref[...]) * v
    return r, log_decay, k2, v, -kk, kk * a, g, bonus


SCAN_CHUNK = R_HEAD_DIM
TILE = 4 * R_HEAD_DIM
SUB_BLOCK = 16
BATCH_UNROLL = 8


def _contract_last(a, b):
    return lax.dot_general(a, b, (((1,), (1,)), ((), ())), preferred_element_type=F32)


def _rwkv_scan_kernel(r_ref, lw_ref, k_ref, v_ref, al_ref, be_ref, g_ref, bonus_ref, s0_ref,
                      lg_ref, lb_ref, bd_ref,
                      o_ref, sout_ref,
                      p_ref, at_ref, bh_ref, u_ref, z_ref, y_ref, aab_ref, arb_ref, bk_ref, ec_ref,
                      coef_ref, *us_refs, nb, tt):
    j = pl.program_id(1)
    n = R_HEAD_DIM
    c = SCAN_CHUNK
    nt = R_WIDTH // TILE
    group = BATCH_UNROLL if nb % BATCH_UNROLL == 0 else 1
    bd = bd_ref[...]
    bd_half = bd[0:TILE, 0:TILE]
    block_mask = (lax.broadcasted_iota(jnp.int32, (TILE, TILE), 0) // n
                  == lax.broadcasted_iota(jnp.int32, (TILE, TILE), 1) // n)
    t_id = lax.broadcasted_iota(jnp.int32, (c, TILE), 0)
    j_id = lax.broadcasted_iota(jnp.int32, (c, TILE), 1) % c
    strict = j_id < t_id
    incl = j_id <= t_id
    later = (lax.broadcasted_iota(jnp.int32, (SUB_BLOCK, SUB_BLOCK, R_WIDTH), 1)
             > lax.broadcasted_iota(jnp.int32, (SUB_BLOCK, SUB_BLOCK, R_WIDTH), 0))
    tri = (lax.broadcasted_iota(jnp.int32, (c, c), 1)
           <= lax.broadcasted_iota(jnp.int32, (c, c), 0)).astype(BF16)

    @pl.when(j == 0)
    def _():
        for b in range(nb):
            for i in range(nt):
                rows = []
                for hh in range(TILE // n):
                    parts = []
                    if hh:
                        parts.append(jnp.zeros((n, hh * n), F32))
                    parts.append(s0_ref[b, i * (TILE // n) + hh])
                    if TILE - (hh + 1) * n:
                        parts.append(jnp.zeros((n, TILE - (hh + 1) * n), F32))
                    rows.append(jnp.concatenate(parts, axis=-1))
                p_ref[b, i] = jnp.concatenate(rows, axis=0)

    def block_diag(x):
        return jnp.where(block_mask, jnp.concatenate([x] * (TILE // c), axis=0), 0.0).astype(BF16)

    def head_sums(x):
        xb = x.astype(BF16)
        return jnp.concatenate(
            [jnp.dot(xb[:, i * TILE:(i + 1) * TILE], bd_half, preferred_element_type=F32)
             for i in range(nt)], axis=-1)

    for ci in range(tt // c):
        rows = pl.ds(ci * c, c)

        def prepare(g, carry):
            seqs = [g * group + s for s in range(group)]
            lws = [lw_ref[b, rows, :] for b in seqs]
            lincs = []
            for lw in lws:
                hi = lw.astype(BF16)
                rem = lw - hi.astype(F32)
                mid = rem.astype(BF16)
                lo = (rem - mid.astype(F32)).astype(BF16)
                lincs.append(jnp.dot(tri, hi, preferred_element_type=F32)
                             + jnp.dot(tri, mid, preferred_element_type=F32)
                             + jnp.dot(tri, lo, preferred_element_type=F32))
            scaled = []
            for b, lw, linc in zip(seqs, lws, lincs):
                lc = linc[c - 1:c]
                e_neg = jnp.exp(-linc)
                e_rem = jnp.exp(lc - linc)
                al, be, r, k, v = (ref[b, rows, :] for ref in (al_ref, be_ref, r_ref, k_ref, v_ref))
                at = al * jnp.exp(linc - lw)
                rt = r * jnp.exp(linc)
                bh = be * e_neg
                kh = k * e_neg
                at_ref[b] = at
                bh_ref[b] = bh
                ec_ref[b] = jnp.exp(lc)
                bk_ref[b] = jnp.concatenate([be * e_rem, k * e_rem], axis=0).astype(BF16)
                scaled.append((at, rt, bh, kh, v))
            products = []
            for b, (at, rt, bh, kh, v) in zip(seqs, scaled):
                for i in range(nt):
                    ls = slice(i * TILE, (i + 1) * TILE)
                    x = jnp.concatenate([at[:, ls], rt[:, ls]], axis=0).astype(BF16)
                    a_k = _contract_last(x, block_diag(kh[:, ls]))
                    a_b = _contract_last(x, block_diag(bh[:, ls]))
                    x_p = _contract_last(x, p_ref[b, i].astype(BF16))
                    products.append((b, i, ls, a_k, a_b, x_p, block_diag(v[:, ls])))
            for b, i, ls, a_k, a_b, x_p, w_v in products:
                u_ref[b, :, ls] = x_p[0:c] + jnp.dot(
                    jnp.where(strict, a_k[0:c], 0.0).astype(BF16), w_v, preferred_element_type=F32)
                y_ref[b, rows, ls] = x_p[c:2 * c] + jnp.dot(
                    jnp.where(incl, a_k[c:2 * c], 0.0).astype(BF16), w_v, preferred_element_type=F32)
                aab_ref[b, i] = jnp.where(strict, a_b[0:c], 0.0).astype(BF16)
                arb_ref[b, i] = jnp.where(incl, a_b[c:2 * c], 0.0).astype(BF16)
            return carry

        lax.fori_loop(0, nb // group, prepare, 0)
        z_ref[...] = jnp.zeros(z_ref.shape, F32)

        for sb in range(c // SUB_BLOCK):
            srows = pl.ds(sb * SUB_BLOCK, SUB_BLOCK)

            sq = SUB_BLOCK * SUB_BLOCK
            lhs = [(bh_ref[b, srows, :][:, None, :] * at_ref[b, srows, :][None, :, :]).reshape(sq, R_WIDTH)
                   for b in range(nb)]
            coef = head_sums(jnp.concatenate(lhs, axis=0))
            for b in range(nb):
                coef_ref[:, b * SUB_BLOCK:(b + 1) * SUB_BLOCK, :] = jnp.where(
                    later, coef[b * sq:(b + 1) * sq].reshape(SUB_BLOCK, SUB_BLOCK, R_WIDTH), 0.0)
            if sb:
                for b in range(nb):
                    z = z_ref[b]
                    for i in range(nt):
                        ls = slice(i * TILE, (i + 1) * TILE)
                        u_ref[b, srows, ls] += jnp.dot(aab_ref[b, i, srows, :], block_diag(z[:, ls]),
                                                       preferred_element_type=F32)

            for b in range(nb):
                us_refs[b][...] = u_ref[b, srows, :]

            def substitute(j, carry):
                coef = coef_ref[j]
                for b in range(nb):
                    z_j = us_refs[b][pl.ds(j, 1), :]
                    us_refs[b][...] += coef[b * SUB_BLOCK:(b + 1) * SUB_BLOCK] * z_j
                return carry

            lax.fori_loop(0, SUB_BLOCK - 1, substitute, 0)
            for b in range(nb):
                z_ref[b, srows, :] = us_refs[b][...]

        tiles = [(b, i, slice(i * TILE, (i + 1) * TILE)) for b in range(nb) for i in range(nt)]
        for b, i, ls in tiles:
            y_ref[b, rows, ls] += jnp.dot(arb_ref[b, i], block_diag(z_ref[b, :, ls]),
                                          preferred_element_type=F32)
        for b, i, ls in tiles:
            zv_t = jnp.concatenate([z_ref[b, :, ls], v_ref[b, rows, ls]], axis=0).T.astype(BF16)
            upd = jnp.dot(zv_t, bk_ref[b, :, ls], preferred_element_type=F32)
            p_ref[b, i] = p_ref[b, i] * ec_ref[b, :, ls] + jnp.where(block_mask, upd, 0.0)

    def head_means(x):
        return jnp.concatenate(
            [_split_dot(x[:, i * TILE:(i + 1) * TILE], bd_half) for i in range(nt)], axis=-1) * (1.0 / n)

    y = y_ref[...].reshape(nb * tt, R_WIDTH)
    d = y - head_means(y)
    yn = d * lax.rsqrt(head_means(d * d) + LNX_EPS) * lg_ref[...] + lb_ref[...]
    out = (yn + bonus_ref[...].reshape(nb * tt, R_WIDTH)) * g_ref[...].reshape(nb * tt, R_WIDTH)
    o_ref[...] = out.reshape(nb, tt, R_WIDTH).astype(o_ref.dtype)

    @pl.when(j == pl.num_programs(1) - 1)
    def _():
        for b in range(nb):
            for i in range(nt):
                p = p_ref[b, i]
                for hh in range(TILE // n):
                    sout_ref[b, i * (TILE // n) + hh] = p[hh * n:(hh + 1) * n, hh * n:(hh + 1) * n]


def _rwkv_scan(r, lw, k, v, al, be, g, bonus, s0, lg, lb, bd, nb, tt):
    b, t, _ = r.shape
    c = SCAN_CHUNK
    row = pl.BlockSpec((nb, tt, R_WIDTH), lambda i, j: (i, j, 0))
    st = pl.BlockSpec((nb, R_HEADS, R_HEAD_DIM, R_HEAD_DIM), lambda i, j: (i, 0, 0, 0))
    nt = R_WIDTH // TILE
    return pl.pallas_call(
        functools.partial(_rwkv_scan_kernel, nb=nb, tt=tt),
        grid=(b // nb, t // tt),
        in_specs=[row] * 8 + [st, _const_spec((1, R_WIDTH)), _const_spec((1, R_WIDTH)),
                              _const_spec((R_WIDTH, R_WIDTH))],
        out_specs=[row, st],
        out_shape=[jax.ShapeDtypeStruct((b, t, R_WIDTH), BF16),
                   jax.ShapeDtypeStruct((b, R_HEADS, R_HEAD_DIM, R_HEAD_DIM), F32)],
        scratch_shapes=[pltpu.VMEM((nb, nt, TILE, TILE), F32),
                        pltpu.VMEM((nb, c, R_WIDTH), F32),
                        pltpu.VMEM((nb, c, R_WIDTH), F32),
                        pltpu.VMEM((nb, c, R_WIDTH), F32),
                        pltpu.VMEM((nb, c, R_WIDTH), F32),
                        pltpu.VMEM((nb, tt, R_WIDTH), F32),
                        pltpu.VMEM((nb, nt, c, TILE), BF16),
                        pltpu.VMEM((nb, nt, c, TILE), BF16),
                        pltpu.VMEM((nb, 2 * c, R_WIDTH), BF16),
                        pltpu.VMEM((nb, 1, R_WIDTH), F32),
                        pltpu.VMEM((SUB_BLOCK, nb * SUB_BLOCK, R_WIDTH), F32)]
                       + [pltpu.VMEM((SUB_BLOCK, R_WIDTH), F32)] * nb,
        compiler_params=_cparams(("parallel", "arbitrary")),
        name="rwkv_scan",
    )(r, lw, k, v, al, be, g, bonus, s0, lg, lb, bd)


def _out_ffn_kernel(x_ref, oa_ref, orw_ref, mod_ref, ng_ref, wout_ref, wg_ref, wu_ref, wd_ref, o_ref):
    gb, tr, _ = x_ref.shape
    rows = gb * tr
    x = x_ref[...]
    mod = mod_ref[...]
    gt_a = mod[:, :, 2 * D_MODEL:3 * D_MODEL]
    sh_f = mod[:, :, 3 * D_MODEL:4 * D_MODEL]
    sc_f = mod[:, :, 4 * D_MODEL:5 * D_MODEL]
    gt_f = mod[:, :, 5 * D_MODEL:6 * D_MODEL]
    half = A_HEADS * V_DIM
    m = (jnp.dot(oa_ref[...].reshape(rows, half), wout_ref[0:half, :], preferred_element_type=F32)
         + jnp.dot(orw_ref[...].reshape(rows, R_WIDTH), wout_ref[half:D_MIX, :],
                   preferred_element_type=F32))
    x = x + gt_a * _rms(m.reshape(gb, tr, D_MODEL), ng_ref[1:2, :])
    h = (_rms(x, ng_ref[2:3, :]) * (1.0 + sc_f) + sh_f).reshape(rows, D_MODEL).astype(BF16)

    def gate_up(c):
        cols = slice(c * FFN_CHUNK, (c + 1) * FFN_CHUNK)
        return (jnp.dot(h, wg_ref[:, cols], preferred_element_type=F32),
                jnp.dot(h, wu_ref[:, cols], preferred_element_type=F32))

    f = jnp.zeros((rows, D_MODEL), F32)
    n_chunks = D_FF // FFN_CHUNK
    pending = [gate_up(c) for c in range(min(FFN_LOOKAHEAD, n_chunks))]
    for c in range(n_chunks):
        if c + FFN_LOOKAHEAD < n_chunks:
            pending.append(gate_up(c + FFN_LOOKAHEAD))
        gate, up = pending.pop(0)
        act = (gate * _sigmoid(gate) * up).astype(BF16)
        f = f + jnp.dot(act, wd_ref[c * FFN_CHUNK:(c + 1) * FFN_CHUNK, :], preferred_element_type=F32)
    o_ref[...] = x + gt_f * _rms(f.reshape(gb, tr, D_MODEL), ng_ref[3:4, :])


def _out_ffn(x, o_attn, o_rw, mod, ng, wout, wg, wu, wd, layer, gb, tr):
    b, t, _ = x.shape
    row = lambda w: pl.BlockSpec((gb, tr, w), lambda i, j: (i, j, 0))
    wspec = lambda w: _layer_spec(w, layer, pipeline_mode=pl.Buffered(1))
    return pl.pallas_call(
        _out_ffn_kernel,
        grid=(b // gb, t // tr),
        in_specs=[
            row(D_MODEL), row(A_HEADS * V_DIM), row(R_WIDTH),
            pl.BlockSpec((gb, 1, 6 * D_MODEL), lambda i, j: (i, 0, 0)),
            _const_spec((4, D_MODEL)),
            wspec(wout), wspec(wg), wspec(wu), wspec(wd),
        ],
        out_specs=row(D_MODEL),
        out_shape=jax.ShapeDtypeStruct((b, t, D_MODEL), F32),
        compiler_params=_cparams(("parallel", "parallel")),
        name="out_ffn",
    )(x, o_attn, o_rw, mod, ng, wout, wg, wu, wd)


def _rope_tables(pos):
    inv = ROPE_THETA ** (-jnp.arange(0, ROPE_DIM, 2, dtype=F32) / ROPE_DIM)
    ang = pos.astype(F32)[:, None] * inv[None, :]
    cos, sin = jnp.cos(ang), jnp.sin(ang)
    cc = jnp.concatenate([cos, cos], axis=-1)
    ss = jnp.concatenate([-sin, sin], axis=-1)
    return cc, ss, jnp.tile(cc, (1, A_HEADS)).T, jnp.tile(ss, (1, A_HEADS)).T


def _pack_weights(w_in, w_uq, w_uk, w_uv):
    depth = w_in.shape[0]
    half = ROPE_DIM // 2
    o = Q_LORA + KV_LORA
    k_r = w_in[:, :, o:o + ROPE_DIM]
    k_rs = jnp.concatenate([k_r[..., half:], k_r[..., :half]], axis=-1)
    pad = jnp.zeros((depth, D_MODEL, LANE - 2 * ROPE_DIM), w_in.dtype)
    win = jnp.concatenate([w_in[:, :, :o], k_r, k_rs, pad, w_in[:, :, MLA_COLS:]], axis=-1)
    uq = w_uq.reshape(depth, Q_LORA, A_HEADS, NOPE_DIM + ROPE_DIM)
    nope = uq[..., :NOPE_DIM].reshape(depth, Q_LORA, A_HEADS * NOPE_DIM)
    rope = uq[..., NOPE_DIM:]
    rope_s = jnp.concatenate([rope[..., half:], rope[..., :half]], axis=-1)
    wuq = jnp.concatenate([nope, rope.reshape(depth, Q_LORA, -1), rope_s.reshape(depth, Q_LORA, -1)], axis=-1)
    wuq = jnp.transpose(wuq, (0, 2, 1))
    wuk = jnp.transpose(w_uk, (0, 2, 1, 3))
    wkx = jnp.pad(w_uk, ((0, 0), (0, 0), (0, 0), (0, LANE - NOPE_DIM))).reshape(depth, KV_LORA, A_HEADS * LANE)
    wuv = jnp.transpose(w_uv, (0, 2, 3, 1))
    return win.astype(BF16), wuq.astype(BF16), wuk.astype(BF16), wkx.astype(BF16), wuv.astype(BF16)


def _block_diag_ones():
    hid = jnp.arange(R_WIDTH) // R_HEAD_DIM
    bd = (hid[:, None] == hid[None, :]).astype(BF16)
    return bd


def _row_tile(t, pref):
    return pref if t % pref == 0 else t


def kernel(x_prompt, x_sample, c_prompt, c_sample, cache_ckv, cache_krope, state_wkv, state_shift,
           ada_w, ada_b, norm_g, w_in, q_norm_g, kv_norm_g, w_uq, w_uk, w_uv,
           rw_mu, rw_w0, rw_w2, rw_a0, rw_a2, rw_g2, rw_kk, rw_ka, rw_rk, lnx_g, lnx_b,
           w_out, w_gate, w_up, w_down):
    depth = w_in.shape[0]
    bp, tp, _ = x_prompt.shape
    bs, ts, _ = x_sample.shape
    past = cache_ckv.shape[2]

    mod_all = _modulation(jnp.concatenate([c_prompt, c_sample], axis=0), ada_w, ada_b)
    win, wuq, wuk, wkx, wuv = _pack_weights(w_in, w_uq, w_uk, w_uv)
    sel = (jnp.arange(ROPE_DIM)[:, None] + NOPE_DIM == jnp.arange(A_HEADS * LANE)[None, :] % LANE).astype(BF16)
    wout, wg, wu, wd = (w.astype(BF16) for w in (w_out, w_gate, w_up, w_down))
    w2, a2, g2 = (w.astype(BF16) for w in (rw_w2, rw_a2, rw_g2))
    bd = _block_diag_ones()
    tabs_p = _rope_tables(jnp.arange(tp))
    tabs_s = _rope_tables(past + jnp.arange(ts))
    cache_krope_t = jnp.swapaxes(cache_krope, 2, 3)
    vec = lambda a, l: a[l].reshape(1, -1)

    def layer(l, x, mod, tabs, s0, shift0, cache):
        b, t, _ = x.shape
        tr = _row_tile(t, IN_ROWS)
        gb = max(1, min(b, IN_ROWS // t))
        gb = gb if b % gb == 0 else 1
        tabs = tuple(jnp.tile(tb, (gb, 1) if tb.shape[0] == t else (1, gb)) for tb in tabs)
        tq = _row_tile(t, COLS)
        rw_params = (vec(rw_mu, l), vec(rw_w0, l), w2[l], vec(rw_a0, l), a2[l], g2[l],
                     vec(rw_kk, l), vec(rw_ka, l), rw_rk[l].reshape(1, -1), bd)
        ckv, krope, kmat, vt, qt, r, w, k, v, al, be, g, bonus, shift_new = _in_proj(
            x, mod, vec(norm_g[:, 0], l), win, vec(q_norm_g, l), vec(kv_norm_g, l),
            wuq, wuk, wkx, sel, *tabs, shift0, rw_params, l, gb, tr, tq, cache is None)
        if cache is None:
            o_attn = _prompt_attention(qt, kmat, vt, wuv, l, tq)
        else:
            o_attn = _sample_attention(qt, kmat, vt, cache[0], cache[1], l, wuv, min(2048, past))
        ts_ = SCAN_CHUNK
        o_rw, s_new = _rwkv_scan(r, w, k, v, al, be, g, bonus, s0, vec(lnx_g, l), vec(lnx_b, l),
                                 bd, 8 if b % 8 == 0 else b, ts_)
        gb = max(1, min(b, FFN_ROWS // t))
        x_new = _out_ffn(x, o_attn, o_rw, mod, norm_g[l], wout, wg, wu, wd, l,
                         gb if b % gb == 0 else 1, min(t, FFN_ROWS))
        return x_new, ckv, krope, s_new, shift_new[:, 0]

    xp, xs = x_prompt, x_sample
    wkv_zero = jnp.zeros((bp, R_HEADS, R_HEAD_DIM, R_HEAD_DIM), F32)
    shift_zero = jnp.zeros((bp, 1, RW_COLS), F32)
    outs_p, outs_s = [], []
    for l in range(depth):
        mod_p = mod_all[l, :bp].reshape(bp, 1, 6 * D_MODEL)
        mod_s = mod_all[l, bp:].reshape(bs, 1, 6 * D_MODEL)
        xp, *rest_p = layer(l, xp, mod_p, tabs_p, wkv_zero, shift_zero, None)
        xs, *rest_s = layer(l, xs, mod_s, tabs_s, state_wkv[l],
                            state_shift[l].reshape(bs, 1, RW_COLS), (cache_ckv, cache_krope_t))
        outs_p.append(rest_p)
        outs_s.append(rest_s)
    stack = lambda outs, i: jnp.stack([o[i] for o in outs])
    return (xp, xs,
            stack(outs_p, 0), stack(outs_p, 1), stack(outs_p, 2), stack(outs_p, 3),
            stack(outs_s, 0), stack(outs_s, 1), stack(outs_s, 2), stack(outs_s, 3))
```

```python
import functools

import jax
import jax.numpy as jnp
from jax import lax
from jax.experimental import pallas as pl
from jax.experimental.pallas import tpu as pltpu

D_MODEL = 1024
CHUNK = 64
A_HEADS = 8
NOPE_DIM = 64
ROPE_DIM = 32
V_DIM = 64
Q_LORA = 384
KV_LORA = 256
ROPE_THETA = 10000.0
SM_SCALE = (NOPE_DIM + ROPE_DIM) ** -0.5
Q_SCALE = SM_SCALE * 1.4426950408889634
R_HEADS = 8
R_HEAD_DIM = 64
R_WIDTH = R_HEADS * R_HEAD_DIM
DECAY_LORA = 64
AAA_LORA = 64
GATE_LORA = 128
RW_COLS = 3 * R_WIDTH + DECAY_LORA + AAA_LORA + GATE_LORA
MLA_COLS = Q_LORA + KV_LORA + ROPE_DIM
D_MIX = A_HEADS * V_DIM + R_WIDTH
D_FF = 2816
RMS_EPS = 1e-6
LNX_EPS = 64e-5

LANE = 128
QK_WIDTH = KV_LORA + LANE
RW_OFF = Q_LORA + KV_LORA + LANE
IN_PACKED = RW_OFF + RW_COLS
VMEM_LIMIT = 56 * 1024 * 1024
IN_ROWS = 512
FFN_ROWS = 512
FFN_CHUNK = 256
FFN_LOOKAHEAD = 2

BF16 = jnp.bfloat16
F32 = jnp.float32


def _cparams(sem):
    return pltpu.CompilerParams(dimension_semantics=sem, vmem_limit_bytes=VMEM_LIMIT)


def _const_spec(shape):
    n = len(shape)
    return pl.BlockSpec(shape, lambda *_: (0,) * n)


def _layer_spec(stacked, layer, **kwargs):
    shape = stacked.shape[1:]
    zeros = (0,) * len(shape)
    return pl.BlockSpec((None,) + shape, lambda *_: (layer,) + zeros, **kwargs)


def _bdot(a, b):
    return jnp.dot(a.astype(BF16), b.astype(BF16), preferred_element_type=F32)


def _split_dot(a, b):
    hi = a.astype(BF16)
    lo = (a - hi.astype(F32)).astype(BF16)
    return (jnp.dot(hi, b, preferred_element_type=F32)
            + jnp.dot(lo, b, preferred_element_type=F32))


def _rms(x, g):
    return x * lax.rsqrt(jnp.mean(x * x, axis=-1, keepdims=True) + RMS_EPS) * g


def _sigmoid(x):
    return 1.0 / (1.0 + jnp.exp(-x))


def _mod_kernel(c_ref, w_ref, b_ref, o_ref):
    c = c_ref[...]
    s = c * _sigmoid(c)
    o_ref[0] = _bdot(s, w_ref[0]) + b_ref[0]


def _modulation(c_all, ada_w, ada_b):
    depth = ada_w.shape[0]
    n = c_all.shape[0]
    tn = 1024
    return pl.pallas_call(
        _mod_kernel,
        grid=(depth, 6 * D_MODEL // tn),
        in_specs=[
            pl.BlockSpec((n, D_MODEL), lambda l, j: (0, 0)),
            pl.BlockSpec((1, D_MODEL, tn), lambda l, j: (l, 0, j)),
            pl.BlockSpec((1, 1, tn), lambda l, j: (l, 0, j)),
        ],
        out_specs=pl.BlockSpec((1, n, tn), lambda l, j: (l, 0, j)),
        out_shape=jax.ShapeDtypeStruct((depth, n, 6 * D_MODEL), F32),
        compiler_params=_cparams(("parallel", "parallel")),
        name="adaln_mod",
    )(c_all, ada_w, ada_b.reshape(depth, 1, 6 * D_MODEL))


def _in_proj_kernel(x_ref, mod_ref, g0_ref, win_ref, qg_ref, kvg_ref, wuq_ref, wuk_ref,
                    wkx_ref, cck_ref, ssk_ref, ccq_ref, ssq_ref, shift_ref, *rest, tq, per_head_keys):
    rw_params = rest[0:10]
    ckv_ref, krope_ref, kmat_ref, vt_ref, qt_ref = rest[10:15]
    feat_refs = rest[15:23]
    shift_out_ref, prev_ref = rest[23:25]
    gb, tr, _ = x_ref.shape
    tm = gb * tr
    mod = mod_ref[...]
    sh = mod[:, :, 0:D_MODEL]
    sc = mod[:, :, D_MODEL:2 * D_MODEL]
    h = (_rms(x_ref[...], g0_ref[...]) * (1.0 + sc) + sh).reshape(tm, D_MODEL).astype(BF16)
    proj = jnp.dot(h, win_ref[:, 0:RW_OFF], preferred_element_type=F32)
    rw = jnp.dot(h, win_ref[:, RW_OFF:IN_PACKED], preferred_element_type=F32)
    c_q = proj[:, 0:Q_LORA]
    c_kv = proj[:, Q_LORA:Q_LORA + KV_LORA]
    k_r = proj[:, Q_LORA + KV_LORA:Q_LORA + KV_LORA + ROPE_DIM]
    k_rs = proj[:, Q_LORA + KV_LORA + ROPE_DIM:Q_LORA + KV_LORA + 2 * ROPE_DIM]

    ckv = _rms(c_kv, kvg_ref[...])
    krope = k_r * cck_ref[...] + k_rs * ssk_ref[...]
    ckv_ref[...] = ckv.reshape(gb, tr, KV_LORA)
    krope_ref[...] = krope.reshape(gb, tr, ROPE_DIM)
    if per_head_keys:
        kr_tile = jnp.concatenate([jnp.zeros((tm, NOPE_DIM), F32), krope,
                                   jnp.zeros((tm, LANE - NOPE_DIM - ROPE_DIM), F32)], axis=-1)
        kmat = (jnp.dot(ckv.astype(BF16), wkx_ref[...], preferred_element_type=F32)
                + jnp.concatenate([kr_tile] * A_HEADS, axis=-1))
        kmat_ref[...] = kmat.astype(BF16).reshape(gb, tr, A_HEADS * LANE)
    else:
        zpad = jnp.zeros((tm, LANE - ROPE_DIM), F32)
        kmat_ref[...] = jnp.concatenate([ckv, krope, zpad], axis=-1).astype(BF16).reshape(gb, tr, QK_WIDTH)
    ckv_t = ckv.T.astype(BF16)
    for s in range(gb):
        vt_ref[s] = ckv_t[:, s * tr:(s + 1) * tr]

    cqn_t = _rms(c_q, qg_ref[...]).T.astype(BF16)
    q_t = jnp.dot(wuq_ref[...], cqn_t, preferred_element_type=F32)
    nope_w = A_HEADS * NOPE_DIM
    rope_w = A_HEADS * ROPE_DIM
    q_rope = (q_t[nope_w:nope_w + rope_w] * ccq_ref[...]
              + q_t[nope_w + rope_w:nope_w + 2 * rope_w] * ssq_ref[...])
    for hd in range(A_HEADS):
        q_nope = q_t[hd * NOPE_DIM:(hd + 1) * NOPE_DIM]
        qr = q_rope[hd * ROPE_DIM:(hd + 1) * ROPE_DIM]
        if per_head_keys:
            parts = [q_nope, qr, jnp.zeros((LANE - NOPE_DIM - ROPE_DIM, tm), F32)]
        else:
            q_lat = jnp.dot(wuk_ref[hd], q_nope.astype(BF16), preferred_element_type=F32)
            parts = [q_lat, qr, jnp.zeros((LANE - ROPE_DIM, tm), F32)]
        qh = (jnp.concatenate(parts, axis=0) * Q_SCALE).astype(BF16)
        nqb = tr // tq
        for s in range(gb):
            for qb in range(nqb):
                col = (s * nqb + qb) * tq
                qt_ref[s, qb, :, hd * tq:(hd + 1) * tq] = qh[:, col:col + tq]

    first = jnp.where(pl.program_id(1) == 0, shift_ref[...], prev_ref[...][None])
    rw3 = rw.reshape(gb, tr, RW_COLS)
    row_id = lax.broadcasted_iota(jnp.int32, (gb, tr, 1), 1)
    rw_prev = jnp.where(row_id == 0, first, pltpu.roll(rw3, 1, 1)).reshape(tm, RW_COLS)
    for ref, val in zip(feat_refs, _rwkv_features(rw, rw_prev, *rw_params)):
        ref[...] = val.reshape(gb, tr, R_WIDTH)
    prev_ref[...] = rw[tm - 1:tm]
    shift_out_ref[...] = rw3[:, tr - 1:tr, :]


def _in_proj(x, mod, g0, win, qg, kvg, wuq, wuk, wkx, cck, ssk, ccq, ssq, shift0, rw_params,
             layer, gb, tr, tq, per_head_keys):
    kw = A_HEADS * LANE if per_head_keys else QK_WIDTH
    qw = LANE if per_head_keys else QK_WIDTH
    b, t, _ = x.shape
    tm = gb * tr
    grid = (b // gb, t // tr)
    row = lambda w: pl.BlockSpec((gb, tr, w), lambda i, j: (i, j, 0))
    tab = lambda w: pl.BlockSpec((tm, w), lambda i, j: (j, 0))
    tab_t = pl.BlockSpec((A_HEADS * ROPE_DIM, tm), lambda i, j: (0, j))
    nqb = tr // tq
    return pl.pallas_call(
        functools.partial(_in_proj_kernel, tq=tq, per_head_keys=per_head_keys),
        grid=grid,
        in_specs=[
            row(D_MODEL),
            pl.BlockSpec((gb, 1, 6 * D_MODEL), lambda i, j: (i, 0, 0)),
            _const_spec((1, D_MODEL)),
            _layer_spec(win, layer),
            _const_spec((1, Q_LORA)),
            _const_spec((1, KV_LORA)),
            _layer_spec(wuq, layer),
            _layer_spec(wuk, layer),
            _layer_spec(wkx, layer),
            tab(ROPE_DIM), tab(ROPE_DIM), tab_t, tab_t,
            pl.BlockSpec((gb, 1, RW_COLS), lambda i, j: (i, 0, 0)),
        ] + [_const_spec(p.shape) for p in rw_params],
        out_specs=[
            row(KV_LORA), row(ROPE_DIM), row(kw),
            pl.BlockSpec((gb, KV_LORA, tr), lambda i, j: (i, 0, j)),
            pl.BlockSpec((gb, nqb, qw, A_HEADS * tq), lambda i, j: (i, j, 0, 0)),
        ] + [row(R_WIDTH)] * 8 + [pl.BlockSpec((gb, 1, RW_COLS), lambda i, j: (i, 0, 0))],
        out_shape=[
            jax.ShapeDtypeStruct((b, t, KV_LORA), F32),
            jax.ShapeDtypeStruct((b, t, ROPE_DIM), F32),
            jax.ShapeDtypeStruct((b, t, kw), BF16),
            jax.ShapeDtypeStruct((b, KV_LORA, t), BF16),
            jax.ShapeDtypeStruct((b, t // tq, qw, A_HEADS * tq), BF16),
        ] + [jax.ShapeDtypeStruct((b, t, R_WIDTH), F32)] * 8 + [jax.ShapeDtypeStruct((b, 1, RW_COLS), F32)],
        scratch_shapes=[pltpu.VMEM((1, RW_COLS), F32)],
        compiler_params=_cparams(("parallel", "arbitrary")),
        name="in_proj",
    )(x, mod, g0, win, qg, kvg, wuq, wuk, wkx, cck, ssk, ccq, ssq, shift0, *rw_params)


COLS = 2 * LANE
SCORE_LOOKAHEAD = 4
SAMPLE_SUB_KEYS = 512


def _softmax_update(s, vt, c, m_ref, l_ref, acc_ref):
    m_prev = m_ref[c]
    m_new = jnp.maximum(m_prev, jnp.max(s, axis=0, keepdims=True))
    alpha = jnp.exp2(m_prev - m_new)
    p = jnp.exp2(s - m_new)
    l_ref[c] = alpha * l_ref[c] + jnp.sum(p, axis=0, keepdims=True)
    acc_ref[c] = alpha * acc_ref[c] + jnp.dot(vt, p.astype(BF16), preferred_element_type=F32)
    m_ref[c] = m_new


def _pipelined_updates(units, m_ref, l_ref, acc_ref):
    n = len(units)
    pending = [units[u][0]() for u in range(min(SCORE_LOOKAHEAD, n))]
    for u in range(n):
        if u + SCORE_LOOKAHEAD < n:
            pending.append(units[u + SCORE_LOOKAHEAD][0]())
        _softmax_update(pending.pop(0), units[u][1], units[u][2], m_ref, l_ref, acc_ref)


def _pipelined_chunks(scores, vt, nc, m_ref, l_ref, acc_ref):
    _pipelined_updates([(functools.partial(scores, c), vt, c) for c in range(nc)], m_ref, l_ref, acc_ref)


def _attn_init(m_ref, l_ref, acc_ref):
    m_ref[...] = jnp.full(m_ref.shape, -jnp.inf, F32)
    l_ref[...] = jnp.zeros(l_ref.shape, F32)
    acc_ref[...] = jnp.zeros(acc_ref.shape, F32)


def _attn_finish(l_ref, acc_ref, wuv_ref, o_ref, tq):
    outs = []
    for hd in range(A_HEADS):
        c, off = divmod(hd * tq, COLS)
        o_lat = acc_ref[c, :, off:off + tq] / l_ref[c, :, off:off + tq]
        outs.append(jnp.dot(wuv_ref[hd], o_lat.astype(BF16), preferred_element_type=F32))
    o_ref[0] = jnp.concatenate(outs, axis=0).T.astype(o_ref.dtype)


def _attn_scratch(tq):
    nc = A_HEADS * tq // COLS
    return [pltpu.VMEM((nc, 1, COLS), F32), pltpu.VMEM((nc, 1, COLS), F32),
            pltpu.VMEM((nc, KV_LORA, COLS), F32)]


def _prompt_attn_kernel(q_ref, k_ref, vt_ref, wuv_ref, o_ref, m_ref, l_ref, acc_ref, *, tq):
    i = pl.program_id(1)
    nc = A_HEADS * tq // COLS
    _attn_init(m_ref, l_ref, acc_ref)

    def tile(j, n_tiles, bias):
        off = pl.multiple_of(j * tq, tq)
        vt = vt_ref[0, :, pl.ds(off, n_tiles * tq)]

        def scores(c):
            k = k_ref[0, pl.ds(off, n_tiles * tq), c * LANE:(c + 1) * LANE]
            s = jnp.dot(k, q_ref[0, 0, :, c * COLS:(c + 1) * COLS], preferred_element_type=F32)
            return s if bias is None else s + bias

        _pipelined_chunks(scores, vt, nc, m_ref, l_ref, acc_ref)

    def tile_pair(p, carry):
        tile(2 * p, 2, None)
        return carry

    lax.fori_loop(0, i // 2, tile_pair, 0)

    @pl.when(i % 2 == 1)
    def _():
        tile(i - 1, 1, None)

    k_chunk = lax.broadcasted_iota(jnp.int32, (tq, COLS), 0) // CHUNK
    q_chunk = (lax.broadcasted_iota(jnp.int32, (tq, COLS), 1) % tq) // CHUNK
    tile(i, 1, jnp.where(k_chunk <= q_chunk, 0.0, -jnp.inf).astype(F32))
    _attn_finish(l_ref, acc_ref, wuv_ref, o_ref, tq)


def _prompt_attention(qt, kmat, vt, wuv, layer, tq):
    b, t, _ = kmat.shape
    assert tq == COLS, "one softmax column chunk per head"
    return pl.pallas_call(
        functools.partial(_prompt_attn_kernel, tq=tq),
        grid=(b, t // tq),
        in_specs=[
            pl.BlockSpec((1, 1, LANE, A_HEADS * tq), lambda i, j: (i, j, 0, 0)),
            pl.BlockSpec((1, t, A_HEADS * LANE), lambda i, j: (i, 0, 0)),
            pl.BlockSpec((1, KV_LORA, t), lambda i, j: (i, 0, 0)),
            _layer_spec(wuv, layer),
        ],
        out_specs=pl.BlockSpec((1, tq, A_HEADS * V_DIM), lambda i, j: (i, j, 0)),
        out_shape=jax.ShapeDtypeStruct((b, t, A_HEADS * V_DIM), BF16),
        scratch_shapes=_attn_scratch(tq),
        compiler_params=_cparams(("parallel", "arbitrary")),
        name="prompt_attn",
    )(qt, kmat, vt, wuv)


def _sample_attn_kernel(q_ref, kn_ref, vtn_ref, cckv_ref, ckr_ref, wuv_ref, o_ref,
                        m_ref, l_ref, acc_ref, *, tq):
    j = pl.program_id(1)
    nc = A_HEADS * tq // COLS

    @pl.when(j == 0)
    def _():
        _attn_init(m_ref, l_ref, acc_ref)

    def cache_scores(ckv_b, kr, c):
        cols = slice(c * COLS, (c + 1) * COLS)
        return (jnp.dot(ckv_b, q_ref[0, 0, 0:KV_LORA, cols], preferred_element_type=F32)
                + jnp.dot(kr, q_ref[0, 0, KV_LORA:KV_LORA + ROPE_DIM, cols], preferred_element_type=F32))

    units = []
    for sub in range(cckv_ref.shape[2] // SAMPLE_SUB_KEYS):
        keys = slice(sub * SAMPLE_SUB_KEYS, (sub + 1) * SAMPLE_SUB_KEYS)
        ckv = cckv_ref[0, 0, keys, :]
        ckv_b = ckv.astype(BF16)
        vt = ckv.T.astype(BF16)
        kr = ckr_ref[0, 0, :, keys].T.astype(BF16)
        units += [(functools.partial(cache_scores, ckv_b, kr, c), vt, c) for c in range(nc)]
    _pipelined_updates(units, m_ref, l_ref, acc_ref)

    @pl.when(j == pl.num_programs(1) - 1)
    def _():
        kn = kn_ref[0]
        new_scores = lambda c: jnp.dot(kn, q_ref[0, 0, :, c * COLS:(c + 1) * COLS],
                                       preferred_element_type=F32)
        _pipelined_chunks(new_scores, vtn_ref[0], nc, m_ref, l_ref, acc_ref)
        _attn_finish(l_ref, acc_ref, wuv_ref, o_ref, tq)


def _sample_attention(qt, kcat, vt, cache_ckv, cache_krope_t, layer, wuv, tk):
    b, tq, _ = kcat.shape
    past = cache_ckv.shape[2]
    return pl.pallas_call(
        functools.partial(_sample_attn_kernel, tq=tq),
        grid=(b, past // tk),
        in_specs=[
            pl.BlockSpec((1, 1, QK_WIDTH, A_HEADS * tq), lambda i, j: (i, 0, 0, 0)),
            pl.BlockSpec((1, tq, QK_WIDTH), lambda i, j: (i, 0, 0)),
            pl.BlockSpec((1, KV_LORA, tq), lambda i, j: (i, 0, 0)),
            pl.BlockSpec((1, 1, tk, KV_LORA), lambda i, j: (layer, i, j, 0)),
            pl.BlockSpec((1, 1, ROPE_DIM, tk), lambda i, j: (layer, i, 0, j)),
            _layer_spec(wuv, layer),
        ],
        out_specs=pl.BlockSpec((1, tq, A_HEADS * V_DIM), lambda i, j: (i, 0, 0)),
        out_shape=jax.ShapeDtypeStruct((b, tq, A_HEADS * V_DIM), BF16),
        scratch_shapes=_attn_scratch(tq),
        compiler_params=_cparams(("parallel", "arbitrary")),
        name="sample_attn",
    )(qt, kcat, vt, cache_ckv, cache_krope_t, wuv)


def _rwkv_features(rw, rw_prev, mu_ref, w0_ref, w2_ref, a0_ref, a2_ref, g2_ref, kkw_ref, ka_ref, rk_ref,
                   bd_ref):
    mix = rw + (rw_prev - rw) * mu_ref[...]
    r = mix[:, 0:R_WIDTH]
    k = mix[:, R_WIDTH:2 * R_WIDTH]
    v = mix[:, 2 * R_WIDTH:3 * R_WIDTH]
    o = 3 * R_WIDTH
    w_lo = mix[:, o:o + DECAY_LORA]
    a_lo = mix[:, o + DECAY_LORA:o + DECAY_LORA + AAA_LORA]
    g_lo = mix[:, o + DECAY_LORA + AAA_LORA:RW_COLS]
    w_raw = w0_ref[...] + _bdot(jnp.tanh(w_lo), w2_ref[...])
    nw = -w_raw
    softplus = jnp.maximum(nw, 0.0) + jnp.log(1.0 + jnp.exp(-jnp.abs(nw)))
    log_decay = -jnp.exp(-softplus - 0.5)
    a = _sigmoid(a0_ref[...] + _bdot(a_lo, a2_ref[...]))
    g = _bdot(_sigmoid(g_lo), g2_ref[...])
    bd_half = bd_ref[0:TILE, 0:TILE]

    def head_sums(x):
        return jnp.concatenate([_split_dot(x[:, i * TILE:(i + 1) * TILE], bd_half)
                                for i in range(R_WIDTH // TILE)], axis=-1)

    kk = k * kkw_ref[...]
    kk = kk / jnp.maximum(jnp.sqrt(head_sums(kk * kk)), 1e-12)
    k2 = k * (1.0 + (a - 1.0) * ka_ref[...])
    bonus = head_sums(r * k2 * rk_ref[...]) * v
    return r, log_decay, k2, v, -kk, kk * a, g, bonus


SCAN_CHUNK = R_HEAD_DIM
TILE = 4 * R_HEAD_DIM
SUB_BLOCK = 16
BATCH_UNROLL = 8


def _contract_last(a, b):
    return lax.dot_general(a, b, (((1,), (1,)), ((), ())), preferred_element_type=F32)


def _rwkv_scan_kernel(r_ref, lw_ref, k_ref, v_ref, al_ref, be_ref, g_ref, bonus_ref, s0_ref,
                      lg_ref, lb_ref, bd_ref,
                      o_ref, sout_ref,
                      p_ref, at_ref, bh_ref, u_ref, z_ref, y_ref, aab_ref, arb_ref, bk_ref, ec_ref,
                      coef_ref, *us_refs, nb, tt):
    j = pl.program_id(1)
    n = R_HEAD_DIM
    c = SCAN_CHUNK
    nt = R_WIDTH // TILE
    group = BATCH_UNROLL if nb % BATCH_UNROLL == 0 else 1
    bd = bd_ref[...]
    bd_half = bd[0:TILE, 0:TILE]
    block_mask = (lax.broadcasted_iota(jnp.int32, (TILE, TILE), 0) // n
                  == lax.broadcasted_iota(jnp.int32, (TILE, TILE), 1) // n)
    t_id = lax.broadcasted_iota(jnp.int32, (c, TILE), 0)
    j_id = lax.broadcasted_iota(jnp.int32, (c, TILE), 1) % c
    strict = j_id < t_id
    incl = j_id <= t_id
    tri = (lax.broadcasted_iota(jnp.int32, (c, c), 1)
           <= lax.broadcasted_iota(jnp.int32, (c, c), 0)).astype(BF16)

    @pl.when(j == 0)
    def _():
        for b in range(nb):
            for i in range(nt):
                rows = []
                for hh in range(TILE // n):
                    parts = []
                    if hh:
                        parts.append(jnp.zeros((n, hh * n), F32))
                    parts.append(s0_ref[b, i * (TILE // n) + hh])
                    if TILE - (hh + 1) * n:
                        parts.append(jnp.zeros((n, TILE - (hh + 1) * n), F32))
                    rows.append(jnp.concatenate(parts, axis=-1))
                p_ref[b, i] = jnp.concatenate(rows, axis=0)

    def block_diag(x):
        return jnp.where(block_mask, jnp.concatenate([x] * (TILE // c), axis=0), 0.0).astype(BF16)

    def head_sums(x):
        xb = x.astype(BF16)
        return jnp.concatenate(
            [jnp.dot(xb[:, i * TILE:(i + 1) * TILE], bd_half, preferred_element_type=F32)
             for i in range(nt)], axis=-1)

    for ci in range(tt // c):
        rows = pl.ds(ci * c, c)

        def prepare(g, carry):
            seqs = [g * group + s for s in range(group)]
            lws = [lw_ref[b, rows, :] for b in seqs]
            lincs = []
            for lw in lws:
                hi = lw.astype(BF16)
                rem = lw - hi.astype(F32)
                mid = rem.astype(BF16)
                lo = (rem - mid.astype(F32)).astype(BF16)
                lincs.append(jnp.dot(tri, hi, preferred_element_type=F32)
                             + jnp.dot(tri, mid, preferred_element_type=F32)
                             + jnp.dot(tri, lo, preferred_element_type=F32))
            scaled = []
            for b, lw, linc in zip(seqs, lws, lincs):
                lc = linc[c - 1:c]
                e_neg = jnp.exp(-linc)
                e_rem = jnp.exp(lc - linc)
                al, be, r, k, v = (ref[b, rows, :] for ref in (al_ref, be_ref, r_ref, k_ref, v_ref))
                at = al * jnp.exp(linc - lw)
                rt = r * jnp.exp(linc)
                bh = be * e_neg
                kh = k * e_neg
                at_ref[b] = at
                bh_ref[b] = bh
                ec_ref[b] = jnp.exp(lc)
                bk_ref[b] = jnp.concatenate([be * e_rem, k * e_rem], axis=0).astype(BF16)
                scaled.append((at, rt, bh, kh, v))
            products = []
            for b, (at, rt, bh, kh, v) in zip(seqs, scaled):
                for i in range(nt):
                    ls = slice(i * TILE, (i + 1) * TILE)
                    x = jnp.concatenate([at[:, ls], rt[:, ls]], axis=0).astype(BF16)
                    a_k = _contract_last(x, block_diag(kh[:, ls]))
                    a_b = _contract_last(x, block_diag(bh[:, ls]))
                    x_p = _contract_last(x, p_ref[b, i].astype(BF16))
                    products.append((b, i, ls, a_k, a_b, x_p, block_diag(v[:, ls])))
            for b, i, ls, a_k, a_b, x_p, w_v in products:
                u_ref[b, :, ls] = x_p[0:c] + jnp.dot(
                    jnp.where(strict, a_k[0:c], 0.0).astype(BF16), w_v, preferred_element_type=F32)
                y_ref[b, rows, ls] = x_p[c:2 * c] + jnp.dot(
                    jnp.where(incl, a_k[c:2 * c], 0.0).astype(BF16), w_v, preferred_element_type=F32)
                aab_ref[b, i] = jnp.where(strict, a_b[0:c], 0.0).astype(BF16)
                arb_ref[b, i] = jnp.where(incl, a_b[c:2 * c], 0.0).astype(BF16)
            return carry

        lax.fori_loop(0, nb // group, prepare, 0)
        z_ref[...] = jnp.zeros(z_ref.shape, F32)

        for sb in range(c // SUB_BLOCK):
            srows = pl.ds(sb * SUB_BLOCK, SUB_BLOCK)

            sq = SUB_BLOCK * SUB_BLOCK
            lhs = [(bh_ref[b, srows, :][:, None, :] * at_ref[b, srows, :][None, :, :]).reshape(sq, R_WIDTH)
                   for b in range(nb)]
            coef = head_sums(jnp.concatenate(lhs, axis=0))
            for b in range(nb):
                coef_ref[:, b * SUB_BLOCK:(b + 1) * SUB_BLOCK, :] = (
                    coef[b * sq:(b + 1) * sq].reshape(SUB_BLOCK, SUB_BLOCK, R_WIDTH))
            if sb:
                for b in range(nb):
                    z = z_ref[b]
                    for i in range(nt):
                        ls = slice(i * TILE, (i + 1) * TILE)
                        u_ref[b, srows, ls] += jnp.dot(aab_ref[b, i, srows, :], block_diag(z[:, ls]),
                                                       preferred_element_type=F32)

            for b in range(nb):
                us_refs[b][...] = u_ref[b, srows, :]

            def substitute(j, carry):
                coef = coef_ref[j]
                for b in range(nb):
                    z_j = us_refs[b][pl.ds(j, 1), :]
                    z_ref[b, pl.ds(sb * SUB_BLOCK + j, 1), :] = z_j
                    us_refs[b][...] += coef[b * SUB_BLOCK:(b + 1) * SUB_BLOCK] * z_j
                return carry

            lax.fori_loop(0, SUB_BLOCK - 1, substitute, 0)
            for b in range(nb):
                last = (sb + 1) * SUB_BLOCK - 1
                z_ref[b, last:last + 1, :] = us_refs[b][SUB_BLOCK - 1:SUB_BLOCK, :]

        tiles = [(b, i, slice(i * TILE, (i + 1) * TILE)) for b in range(nb) for i in range(nt)]
        for b, i, ls in tiles:
            y_ref[b, rows, ls] += jnp.dot(arb_ref[b, i], block_diag(z_ref[b, :, ls]),
                                          preferred_element_type=F32)
        for b, i, ls in tiles:
            zv_t = jnp.concatenate([z_ref[b, :, ls], v_ref[b, rows, ls]], axis=0).T.astype(BF16)
            upd = jnp.dot(zv_t, bk_ref[b, :, ls], preferred_element_type=F32)
            p_ref[b, i] = p_ref[b, i] * ec_ref[b, :, ls] + jnp.where(block_mask, upd, 0.0)

    def head_means(x):
        return jnp.concatenate(
            [_split_dot(x[:, i * TILE:(i + 1) * TILE], bd_half) for i in range(nt)], axis=-1) * (1.0 / n)

    y = y_ref[...].reshape(nb * tt, R_WIDTH)
    d = y - head_means(y)
    yn = d * lax.rsqrt(head_means(d * d) + LNX_EPS) * lg_ref[...] + lb_ref[...]
    out = (yn + bonus_ref[...].reshape(nb * tt, R_WIDTH)) * g_ref[...].reshape(nb * tt, R_WIDTH)
    o_ref[...] = out.reshape(nb, tt, R_WIDTH).astype(o_ref.dtype)

    @pl.when(j == pl.num_programs(1) - 1)
    def _():
        for b in range(nb):
            for i in range(nt):
                p = p_ref[b, i]
                for hh in range(TILE // n):
                    sout_ref[b, i * (TILE // n) + hh] = p[hh * n:(hh + 1) * n, hh * n:(hh + 1) * n]


def _rwkv_scan(r, lw, k, v, al, be, g, bonus, s0, lg, lb, bd, nb, tt):
    b, t, _ = r.shape
    c = SCAN_CHUNK
    row = pl.BlockSpec((nb, tt, R_WIDTH), lambda i, j: (i, j, 0))
    st = pl.BlockSpec((nb, R_HEADS, R_HEAD_DIM, R_HEAD_DIM), lambda i, j: (i, 0, 0, 0))
    nt = R_WIDTH // TILE
    return pl.pallas_call(
        functools.partial(_rwkv_scan_kernel, nb=nb, tt=tt),
        grid=(b // nb, t // tt),
        in_specs=[row] * 8 + [st, _const_spec((1, R_WIDTH)), _const_spec((1, R_WIDTH)),
                              _const_spec((R_WIDTH, R_WIDTH))],
        out_specs=[row, st],
        out_shape=[jax.ShapeDtypeStruct((b, t, R_WIDTH), BF16),
                   jax.ShapeDtypeStruct((b, R_HEADS, R_HEAD_DIM, R_HEAD_DIM), F32)],
        scratch_shapes=[pltpu.VMEM((nb, nt, TILE, TILE), F32),
                        pltpu.VMEM((nb, c, R_WIDTH), F32),
                        pltpu.VMEM((nb, c, R_WIDTH), F32),
                        pltpu.VMEM((nb, c, R_WIDTH), F32),
                        pltpu.VMEM((nb, c, R_WIDTH), F32),
                        pltpu.VMEM((nb, tt, R_WIDTH), F32),
                        pltpu.VMEM((nb, nt, c, TILE), BF16),
                        pltpu.VMEM((nb, nt, c, TILE), BF16),
                        pltpu.VMEM((nb, 2 * c, R_WIDTH), BF16),
                        pltpu.VMEM((nb, 1, R_WIDTH), F32),
                        pltpu.VMEM((SUB_BLOCK, nb * SUB_BLOCK, R_WIDTH), F32)]
                       + [pltpu.VMEM((SUB_BLOCK, R_WIDTH), F32)] * nb,
        compiler_params=_cparams(("parallel", "arbitrary")),
        name="rwkv_scan",
    )(r, lw, k, v, al, be, g, bonus, s0, lg, lb, bd)


def _out_ffn_kernel(x_ref, oa_ref, orw_ref, mod_ref, ng_ref, wout_ref, wg_ref, wu_ref, wd_ref, o_ref):
    gb, tr, _ = x_ref.shape
    rows = gb * tr
    x = x_ref[...]
    mod = mod_ref[...]
    gt_a = mod[:, :, 2 * D_MODEL:3 * D_MODEL]
    sh_f = mod[:, :, 3 * D_MODEL:4 * D_MODEL]
    sc_f = mod[:, :, 4 * D_MODEL:5 * D_MODEL]
    gt_f = mod[:, :, 5 * D_MODEL:6 * D_MODEL]
    half = A_HEADS * V_DIM
    m = (jnp.dot(oa_ref[...].reshape(rows, half), wout_ref[0:half, :], preferred_element_type=F32)
         + jnp.dot(orw_ref[...].reshape(rows, R_WIDTH), wout_ref[half:D_MIX, :],
                   preferred_element_type=F32))
    x = x + gt_a * _rms(m.reshape(gb, tr, D_MODEL), ng_ref[1:2, :])
    h = (_rms(x, ng_ref[2:3, :]) * (1.0 + sc_f) + sh_f).reshape(rows, D_MODEL).astype(BF16)

    def gate_up(c):
        cols = slice(c * FFN_CHUNK, (c + 1) * FFN_CHUNK)
        return (jnp.dot(h, wg_ref[:, cols], preferred_element_type=F32),
                jnp.dot(h, wu_ref[:, cols], preferred_element_type=F32))

    f = jnp.zeros((rows, D_MODEL), F32)
    n_chunks = D_FF // FFN_CHUNK
    pending = [gate_up(c) for c in range(min(FFN_LOOKAHEAD, n_chunks))]
    for c in range(n_chunks):
        if c + FFN_LOOKAHEAD < n_chunks:
            pending.append(gate_up(c + FFN_LOOKAHEAD))
        gate, up = pending.pop(0)
        act = (gate * _sigmoid(gate) * up).astype(BF16)
        f = f + jnp.dot(act, wd_ref[c * FFN_CHUNK:(c + 1) * FFN_CHUNK, :], preferred_element_type=F32)
    o_ref[...] = x + gt_f * _rms(f.reshape(gb, tr, D_MODEL), ng_ref[3:4, :])


def _out_ffn(x, o_attn, o_rw, mod, ng, wout, wg, wu, wd, layer, gb, tr):
    b, t, _ = x.shape
    row = lambda w: pl.BlockSpec((gb, tr, w), lambda i, j: (i, j, 0))
    wspec = lambda w: _layer_spec(w, layer, pipeline_mode=pl.Buffered(1))
    return pl.pallas_call(
        _out_ffn_kernel,
        grid=(b // gb, t // tr),
        in_specs=[
            row(D_MODEL), row(A_HEADS * V_DIM), row(R_WIDTH),
            pl.BlockSpec((gb, 1, 6 * D_MODEL), lambda i, j: (i, 0, 0)),
            _const_spec((4, D_MODEL)),
            wspec(wout), wspec(wg), wspec(wu), wspec(wd),
        ],
        out_specs=row(D_MODEL),
        out_shape=jax.ShapeDtypeStruct((b, t, D_MODEL), F32),
        compiler_params=_cparams(("parallel", "parallel")),
        name="out_ffn",
    )(x, o_attn, o_rw, mod, ng, wout, wg, wu, wd)


def _rope_tables(pos):
    inv = ROPE_THETA ** (-jnp.arange(0, ROPE_DIM, 2, dtype=F32) / ROPE_DIM)
    ang = pos.astype(F32)[:, None] * inv[None, :]
    cos, sin = jnp.cos(ang), jnp.sin(ang)
    cc = jnp.concatenate([cos, cos], axis=-1)
    ss = jnp.concatenate([-sin, sin], axis=-1)
    return cc, ss, jnp.tile(cc, (1, A_HEADS)).T, jnp.tile(ss, (1, A_HEADS)).T


def _pack_weights(w_in, w_uq, w_uk, w_uv):
    depth = w_in.shape[0]
    half = ROPE_DIM // 2
    o = Q_LORA + KV_LORA
    k_r = w_in[:, :, o:o + ROPE_DIM]
    k_rs = jnp.concatenate([k_r[..., half:], k_r[..., :half]], axis=-1)
    pad = jnp.zeros((depth, D_MODEL, LANE - 2 * ROPE_DIM), w_in.dtype)
    win = jnp.concatenate([w_in[:, :, :o], k_r, k_rs, pad, w_in[:, :, MLA_COLS:]], axis=-1)
    uq = w_uq.reshape(depth, Q_LORA, A_HEADS, NOPE_DIM + ROPE_DIM)
    nope = uq[..., :NOPE_DIM].reshape(depth, Q_LORA, A_HEADS * NOPE_DIM)
    rope = uq[..., NOPE_DIM:]
    rope_s = jnp.concatenate([rope[..., half:], rope[..., :half]], axis=-1)
    wuq = jnp.concatenate([nope, rope.reshape(depth, Q_LORA, -1), rope_s.reshape(depth, Q_LORA, -1)], axis=-1)
    wuq = jnp.transpose(wuq, (0, 2, 1))
    wuk = jnp.transpose(w_uk, (0, 2, 1, 3))
    wkx = jnp.pad(w_uk, ((0, 0), (0, 0), (0, 0), (0, LANE - NOPE_DIM))).reshape(depth, KV_LORA, A_HEADS * LANE)
    wuv = jnp.transpose(w_uv, (0, 2, 3, 1))
    return win.astype(BF16), wuq.astype(BF16), wuk.astype(BF16), wkx.astype(BF16), wuv.astype(BF16)


def _block_diag_ones():
    hid = jnp.arange(R_WIDTH) // R_HEAD_DIM
    bd = (hid[:, None] == hid[None, :]).astype(BF16)
    return bd


def _row_tile(t, pref):
    return pref if t % pref == 0 else t


def kernel(x_prompt, x_sample, c_prompt, c_sample, cache_ckv, cache_krope, state_wkv, state_shift,
           ada_w, ada_b, norm_g, w_in, q_norm_g, kv_norm_g, w_uq, w_uk, w_uv,
           rw_mu, rw_w0, rw_w2, rw_a0, rw_a2, rw_g2, rw_kk, rw_ka, rw_rk, lnx_g, lnx_b,
           w_out, w_gate, w_up, w_down):
    depth = w_in.shape[0]
    bp, tp, _ = x_prompt.shape
    bs, ts, _ = x_sample.shape
    past = cache_ckv.shape[2]

    mod_all = _modulation(jnp.concatenate([c_prompt, c_sample], axis=0), ada_w, ada_b)
    win, wuq, wuk, wkx, wuv = _pack_weights(w_in, w_uq, w_uk, w_uv)
    wout, wg, wu, wd = (w.astype(BF16) for w in (w_out, w_gate, w_up, w_down))
    w2, a2, g2 = (w.astype(BF16) for w in (rw_w2, rw_a2, rw_g2))
    bd = _block_diag_ones()
    tabs_p = _rope_tables(jnp.arange(tp))
    tabs_s = _rope_tables(past + jnp.arange(ts))
    cache_krope_t = jnp.swapaxes(cache_krope, 2, 3)
    vec = lambda a, l: a[l].reshape(1, -1)

    def layer(l, x, mod, tabs, s0, shift0, cache):
        b, t, _ = x.shape
        tr = _row_tile(t, IN_ROWS)
        gb = max(1, min(b, IN_ROWS // t))
        gb = gb if b % gb == 0 else 1
        tabs = tuple(jnp.tile(tb, (gb, 1) if tb.shape[0] == t else (1, gb)) for tb in tabs)
        tq = _row_tile(t, COLS)
        rw_params = (vec(rw_mu, l), vec(rw_w0, l), w2[l], vec(rw_a0, l), a2[l], g2[l],
                     vec(rw_kk, l), vec(rw_ka, l), rw_rk[l].reshape(1, -1), bd)
        ckv, krope, kmat, vt, qt, r, w, k, v, al, be, g, bonus, shift_new = _in_proj(
            x, mod, vec(norm_g[:, 0], l), win, vec(q_norm_g, l), vec(kv_norm_g, l),
            wuq, wuk, wkx, *tabs, shift0, rw_params, l, gb, tr, tq, cache is None)
        if cache is None:
            o_attn = _prompt_attention(qt, kmat, vt, wuv, l, tq)
        else:
            o_attn = _sample_attention(qt, kmat, vt, cache[0], cache[1], l, wuv, min(2048, past))
        ts_ = SCAN_CHUNK
        o_rw, s_new = _rwkv_scan(r, w, k, v, al, be, g, bonus, s0, vec(lnx_g, l), vec(lnx_b, l),
                                 bd, 8 if b % 8 == 0 else b, ts_)
        gb = max(1, min(b, FFN_ROWS // t))
        x_new = _out_ffn(x, o_attn, o_rw, mod, norm_g[l], wout, wg, wu, wd, l,
                         gb if b % gb == 0 else 1, min(t, FFN_ROWS))
        return x_new, ckv, krope, s_new, shift_new[:, 0]

    xp, xs = x_prompt, x_sample
    wkv_zero = jnp.zeros((bp, R_HEADS, R_HEAD_DIM, R_HEAD_DIM), F32)
    shift_zero = jnp.zeros((bp, 1, RW_COLS), F32)
    outs_p, outs_s = [], []
    for l in range(depth):
        mod_p = mod_all[l, :bp].reshape(bp, 1, 6 * D_MODEL)
        mod_s = mod_all[l, bp:].reshape(bs, 1, 6 * D_MODEL)
        xp, *rest_p = layer(l, xp, mod_p, tabs_p, wkv_zero, shift_zero, None)
        xs, *rest_s = layer(l, xs, mod_s, tabs_s, state_wkv[l],
                            state_shift[l].reshape(bs, 1, RW_COLS), (cache_ckv, cache_krope_t))
        outs_p.append(rest_p)
        outs_s.append(rest_s)
    stack = lambda outs, i: jnp.stack([o[i] for o in outs])
    return (xp, xs,
            stack(outs_p, 0), stack(outs_p, 1), stack(outs_p, 2), stack(outs_p, 3),
            stack(outs_s, 0), stack(outs_s, 1), stack(outs_s, 2), stack(outs_s, 3))
```

```python
import functools

import jax
import jax.numpy as jnp
from jax import lax
from jax.experimental import pallas as pl
from jax.experimental.pallas import tpu as pltpu

D_MODEL = 1024
CHUNK = 64
A_HEADS = 8
NOPE_DIM = 64
ROPE_DIM = 32
V_DIM = 64
Q_LORA = 384
KV_LORA = 256
ROPE_THETA = 10000.0
SM_SCALE = (NOPE_DIM + ROPE_DIM) ** -0.5
Q_SCALE = SM_SCALE * 1.4426950408889634
R_HEADS = 8
R_HEAD_DIM = 64
R_WIDTH = R_HEADS * R_HEAD_DIM
DECAY_LORA = 64
AAA_LORA = 64
GATE_LORA = 128
RW_COLS = 3 * R_WIDTH + DECAY_LORA + AAA_LORA + GATE_LORA
MLA_COLS = Q_LORA + KV_LORA + ROPE_DIM
D_MIX = A_HEADS * V_DIM + R_WIDTH
D_FF = 2816
RMS_EPS = 1e-6
LNX_EPS = 64e-5

LANE = 128
QK_WIDTH = KV_LORA + LANE
RW_OFF = Q_LORA + KV_LORA + LANE
IN_PACKED = RW_OFF + RW_COLS
VMEM_LIMIT = 56 * 1024 * 1024
IN_ROWS = 512
FFN_ROWS = 512
FFN_CHUNK = 256
FFN_LOOKAHEAD = 2

BF16 = jnp.bfloat16
F32 = jnp.float32


def _cparams(sem):
    return pltpu.CompilerParams(dimension_semantics=sem, vmem_limit_bytes=VMEM_LIMIT)


def _const_spec(shape):
    n = len(shape)
    return pl.BlockSpec(shape, lambda *_: (0,) * n)


def _layer_spec(stacked, layer, **kwargs):
    shape = stacked.shape[1:]
    zeros = (0,) * len(shape)
    return pl.BlockSpec((None,) + shape, lambda *_: (layer,) + zeros, **kwargs)


def _bdot(a, b):
    return jnp.dot(a.astype(BF16), b.astype(BF16), preferred_element_type=F32)


def _split_dot(a, b):
    hi = a.astype(BF16)
    lo = (a - hi.astype(F32)).astype(BF16)
    return (jnp.dot(hi, b, preferred_element_type=F32)
            + jnp.dot(lo, b, preferred_element_type=F32))


def _rms(x, g):
    return x * lax.rsqrt(jnp.mean(x * x, axis=-1, keepdims=True) + RMS_EPS) * g


def _sigmoid(x):
    return 1.0 / (1.0 + jnp.exp(-x))


def _mod_kernel(c_ref, w_ref, b_ref, o_ref):
    c = c_ref[...]
    s = c * _sigmoid(c)
    o_ref[0] = _bdot(s, w_ref[0]) + b_ref[0]


def _modulation(c_all, ada_w, ada_b):
    depth = ada_w.shape[0]
    n = c_all.shape[0]
    tn = 1024
    return pl.pallas_call(
        _mod_kernel,
        grid=(depth, 6 * D_MODEL // tn),
        in_specs=[
            pl.BlockSpec((n, D_MODEL), lambda l, j: (0, 0)),
            pl.BlockSpec((1, D_MODEL, tn), lambda l, j: (l, 0, j)),
            pl.BlockSpec((1, 1, tn), lambda l, j: (l, 0, j)),
        ],
        out_specs=pl.BlockSpec((1, n, tn), lambda l, j: (l, 0, j)),
        out_shape=jax.ShapeDtypeStruct((depth, n, 6 * D_MODEL), F32),
        compiler_params=_cparams(("parallel", "parallel")),
        name="adaln_mod",
    )(c_all, ada_w, ada_b.reshape(depth, 1, 6 * D_MODEL))


def _in_proj_kernel(x_ref, mod_ref, g0_ref, win_ref, qg_ref, kvg_ref, wuq_ref, wuk_ref,
                    wkx_ref, cck_ref, ssk_ref, ccq_ref, ssq_ref, shift_ref, *rest, tq, per_head_keys):
    rw_params = rest[0:10]
    ckv_ref, krope_ref, kmat_ref, vt_ref, qt_ref = rest[10:15]
    feat_refs = rest[15:23]
    shift_out_ref, prev_ref = rest[23:25]
    gb, tr, _ = x_ref.shape
    tm = gb * tr
    mod = mod_ref[...]
    sh = mod[:, :, 0:D_MODEL]
    sc = mod[:, :, D_MODEL:2 * D_MODEL]
    h = (_rms(x_ref[...], g0_ref[...]) * (1.0 + sc) + sh).reshape(tm, D_MODEL).astype(BF16)
    proj = jnp.dot(h, win_ref[:, 0:RW_OFF], preferred_element_type=F32)
    rw = jnp.dot(h, win_ref[:, RW_OFF:IN_PACKED], preferred_element_type=F32)
    c_q = proj[:, 0:Q_LORA]
    c_kv = proj[:, Q_LORA:Q_LORA + KV_LORA]
    k_r = proj[:, Q_LORA + KV_LORA:Q_LORA + KV_LORA + ROPE_DIM]
    k_rs = proj[:, Q_LORA + KV_LORA + ROPE_DIM:Q_LORA + KV_LORA + 2 * ROPE_DIM]

    ckv = _rms(c_kv, kvg_ref[...])
    krope = k_r * cck_ref[...] + k_rs * ssk_ref[...]
    ckv_ref[...] = ckv.reshape(gb, tr, KV_LORA)
    krope_ref[...] = krope.reshape(gb, tr, ROPE_DIM)
    if per_head_keys:
        kr_tile = jnp.concatenate([jnp.zeros((tm, NOPE_DIM), F32), krope,
                                   jnp.zeros((tm, LANE - NOPE_DIM - ROPE_DIM), F32)], axis=-1)
        kmat = (jnp.dot(ckv.astype(BF16), wkx_ref[...], preferred_element_type=F32)
                + jnp.concatenate([kr_tile] * A_HEADS, axis=-1))
        kmat_ref[...] = kmat.astype(BF16).reshape(gb, tr, A_HEADS * LANE)
    else:
        zpad = jnp.zeros((tm, LANE - ROPE_DIM), F32)
        kmat_ref[...] = jnp.concatenate([ckv, krope, zpad], axis=-1).astype(BF16).reshape(gb, tr, QK_WIDTH)
    ckv_t = ckv.T.astype(BF16)
    for s in range(gb):
        vt_ref[s] = ckv_t[:, s * tr:(s + 1) * tr]

    cqn_t = _rms(c_q, qg_ref[...]).T.astype(BF16)
    q_t = jnp.dot(wuq_ref[...], cqn_t, preferred_element_type=F32)
    nope_w = A_HEADS * NOPE_DIM
    rope_w = A_HEADS * ROPE_DIM
    q_rope = (q_t[nope_w:nope_w + rope_w] * ccq_ref[...]
              + q_t[nope_w + rope_w:nope_w + 2 * rope_w] * ssq_ref[...])
    for hd in range(A_HEADS):
        q_nope = q_t[hd * NOPE_DIM:(hd + 1) * NOPE_DIM]
        qr = q_rope[hd * ROPE_DIM:(hd + 1) * ROPE_DIM]
        if per_head_keys:
            parts = [q_nope, qr, jnp.zeros((LANE - NOPE_DIM - ROPE_DIM, tm), F32)]
        else:
            q_lat = jnp.dot(wuk_ref[hd], q_nope.astype(BF16), preferred_element_type=F32)
            parts = [q_lat, qr, jnp.zeros((LANE - ROPE_DIM, tm), F32)]
        qh = (jnp.concatenate(parts, axis=0) * Q_SCALE).astype(BF16)
        nqb = tr // tq
        for s in range(gb):
            for qb in range(nqb):
                col = (s * nqb + qb) * tq
                qt_ref[s, qb, :, hd * tq:(hd + 1) * tq] = qh[:, col:col + tq]

    first = jnp.where(pl.program_id(1) == 0, shift_ref[...], prev_ref[...][None])
    rw3 = rw.reshape(gb, tr, RW_COLS)
    row_id = lax.broadcasted_iota(jnp.int32, (gb, tr, 1), 1)
    rw_prev = jnp.where(row_id == 0, first, pltpu.roll(rw3, 1, 1)).reshape(tm, RW_COLS)
    for ref, val in zip(feat_refs, _rwkv_features(rw, rw_prev, *rw_params)):
        ref[...] = val.reshape(gb, tr, R_WIDTH)
    prev_ref[...] = rw[tm - 1:tm]
    shift_out_ref[...] = rw3[:, tr - 1:tr, :]


def _in_proj(x, mod, g0, win, qg, kvg, wuq, wuk, wkx, cck, ssk, ccq, ssq, shift0, rw_params,
             layer, gb, tr, tq, per_head_keys):
    kw = A_HEADS * LANE if per_head_keys else QK_WIDTH
    qw = LANE if per_head_keys else QK_WIDTH
    b, t, _ = x.shape
    tm = gb * tr
    grid = (b // gb, t // tr)
    row = lambda w: pl.BlockSpec((gb, tr, w), lambda i, j: (i, j, 0))
    tab = lambda w: pl.BlockSpec((tm, w), lambda i, j: (j, 0))
    tab_t = pl.BlockSpec((A_HEADS * ROPE_DIM, tm), lambda i, j: (0, j))
    nqb = tr // tq
    return pl.pallas_call(
        functools.partial(_in_proj_kernel, tq=tq, per_head_keys=per_head_keys),
        grid=grid,
        in_specs=[
            row(D_MODEL),
            pl.BlockSpec((gb, 1, 6 * D_MODEL), lambda i, j: (i, 0, 0)),
            _const_spec((1, D_MODEL)),
            _layer_spec(win, layer),
            _const_spec((1, Q_LORA)),
            _const_spec((1, KV_LORA)),
            _layer_spec(wuq, layer),
            _layer_spec(wuk, layer),
            _layer_spec(wkx, layer),
            tab(ROPE_DIM), tab(ROPE_DIM), tab_t, tab_t,
            pl.BlockSpec((gb, 1, RW_COLS), lambda i, j: (i, 0, 0)),
        ] + [_const_spec(p.shape) for p in rw_params],
        out_specs=[
            row(KV_LORA), row(ROPE_DIM), row(kw),
            pl.BlockSpec((gb, KV_LORA, tr), lambda i, j: (i, 0, j)),
            pl.BlockSpec((gb, nqb, qw, A_HEADS * tq), lambda i, j: (i, j, 0, 0)),
        ] + [row(R_WIDTH)] * 8 + [pl.BlockSpec((gb, 1, RW_COLS), lambda i, j: (i, 0, 0))],
        out_shape=[
            jax.ShapeDtypeStruct((b, t, KV_LORA), F32),
            jax.ShapeDtypeStruct((b, t, ROPE_DIM), F32),
            jax.ShapeDtypeStruct((b, t, kw), BF16),
            jax.ShapeDtypeStruct((b, KV_LORA, t), BF16),
            jax.ShapeDtypeStruct((b, t // tq, qw, A_HEADS * tq), BF16),
        ] + [jax.ShapeDtypeStruct((b, t, R_WIDTH), F32)] * 8 + [jax.ShapeDtypeStruct((b, 1, RW_COLS), F32)],
        scratch_shapes=[pltpu.VMEM((1, RW_COLS), F32)],
        compiler_params=_cparams(("parallel", "arbitrary")),
        name="in_proj",
    )(x, mod, g0, win, qg, kvg, wuq, wuk, wkx, cck, ssk, ccq, ssq, shift0, *rw_params)


COLS = 2 * LANE
SCORE_LOOKAHEAD = 6
SAMPLE_SUB_KEYS = 512


def _softmax_update(s, vt, c, m_ref, l_ref, acc_ref):
    m_prev = m_ref[c]
    m_new = jnp.maximum(m_prev, jnp.max(s, axis=0, keepdims=True))
    alpha = jnp.exp2(m_prev - m_new)
    p = jnp.exp2(s - m_new)
    l_ref[c] = alpha * l_ref[c] + jnp.sum(p, axis=0, keepdims=True)
    acc_ref[c] = alpha * acc_ref[c] + jnp.dot(vt, p.astype(BF16), preferred_element_type=F32)
    m_ref[c] = m_new


def _pipelined_updates(units, m_ref, l_ref, acc_ref):
    n = len(units)
    pending = [units[u][0]() for u in range(min(SCORE_LOOKAHEAD, n))]
    for u in range(n):
        if u + SCORE_LOOKAHEAD < n:
            pending.append(units[u + SCORE_LOOKAHEAD][0]())
        _softmax_update(pending.pop(0), units[u][1], units[u][2], m_ref, l_ref, acc_ref)


def _pipelined_chunks(scores, vt, nc, m_ref, l_ref, acc_ref):
    _pipelined_updates([(functools.partial(scores, c), vt, c) for c in range(nc)], m_ref, l_ref, acc_ref)


def _attn_init(m_ref, l_ref, acc_ref):
    m_ref[...] = jnp.full(m_ref.shape, -jnp.inf, F32)
    l_ref[...] = jnp.zeros(l_ref.shape, F32)
    acc_ref[...] = jnp.zeros(acc_ref.shape, F32)


def _attn_finish(l_ref, acc_ref, wuv_ref, o_ref, tq):
    outs = []
    for hd in range(A_HEADS):
        c, off = divmod(hd * tq, COLS)
        o_lat = acc_ref[c, :, off:off + tq] / l_ref[c, :, off:off + tq]
        outs.append(jnp.dot(wuv_ref[hd], o_lat.astype(BF16), preferred_element_type=F32))
    o_ref[0] = jnp.concatenate(outs, axis=0).T.astype(o_ref.dtype)


def _attn_scratch(tq):
    nc = A_HEADS * tq // COLS
    return [pltpu.VMEM((nc, 1, COLS), F32), pltpu.VMEM((nc, 1, COLS), F32),
            pltpu.VMEM((nc, KV_LORA, COLS), F32)]


def _prompt_attn_kernel(q_ref, k_ref, vt_ref, wuv_ref, o_ref, m_ref, l_ref, acc_ref, *, tq):
    i = pl.program_id(1)
    nc = A_HEADS * tq // COLS
    _attn_init(m_ref, l_ref, acc_ref)

    def tile(j, n_tiles, bias):
        off = pl.multiple_of(j * tq, tq)
        vt = vt_ref[0, :, pl.ds(off, n_tiles * tq)]

        def scores(c):
            k = k_ref[0, pl.ds(off, n_tiles * tq), c * LANE:(c + 1) * LANE]
            s = jnp.dot(k, q_ref[0, 0, :, c * COLS:(c + 1) * COLS], preferred_element_type=F32)
            return s if bias is None else s + bias

        _pipelined_chunks(scores, vt, nc, m_ref, l_ref, acc_ref)

    def tile_pair(p, carry):
        tile(2 * p, 2, None)
        return carry

    lax.fori_loop(0, i // 2, tile_pair, 0)

    @pl.when(i % 2 == 1)
    def _():
        tile(i - 1, 1, None)

    k_chunk = lax.broadcasted_iota(jnp.int32, (tq, COLS), 0) // CHUNK
    q_chunk = (lax.broadcasted_iota(jnp.int32, (tq, COLS), 1) % tq) // CHUNK
    tile(i, 1, jnp.where(k_chunk <= q_chunk, 0.0, -jnp.inf).astype(F32))
    _attn_finish(l_ref, acc_ref, wuv_ref, o_ref, tq)


def _prompt_attention(qt, kmat, vt, wuv, layer, tq):
    b, t, _ = kmat.shape
    assert tq == COLS, "one softmax column chunk per head"
    return pl.pallas_call(
        functools.partial(_prompt_attn_kernel, tq=tq),
        grid=(b, t // tq),
        in_specs=[
            pl.BlockSpec((1, 1, LANE, A_HEADS * tq), lambda i, j: (i, j, 0, 0)),
            pl.BlockSpec((1, t, A_HEADS * LANE), lambda i, j: (i, 0, 0)),
            pl.BlockSpec((1, KV_LORA, t), lambda i, j: (i, 0, 0)),
            _layer_spec(wuv, layer),
        ],
        out_specs=pl.BlockSpec((1, tq, A_HEADS * V_DIM), lambda i, j: (i, j, 0)),
        out_shape=jax.ShapeDtypeStruct((b, t, A_HEADS * V_DIM), BF16),
        scratch_shapes=_attn_scratch(tq),
        compiler_params=_cparams(("parallel", "arbitrary")),
        name="prompt_attn",
    )(qt, kmat, vt, wuv)


def _sample_attn_kernel(q_ref, kn_ref, vtn_ref, cckv_ref, ckr_ref, wuv_ref, o_ref,
                        m_ref, l_ref, acc_ref, *, tq):
    j = pl.program_id(1)
    nc = A_HEADS * tq // COLS

    @pl.when(j == 0)
    def _():
        _attn_init(m_ref, l_ref, acc_ref)

    def cache_scores(ckv_b, kr, c):
        cols = slice(c * COLS, (c + 1) * COLS)
        return (jnp.dot(ckv_b, q_ref[0, 0, 0:KV_LORA, cols], preferred_element_type=F32)
                + jnp.dot(kr, q_ref[0, 0, KV_LORA:KV_LORA + ROPE_DIM, cols], preferred_element_type=F32))

    units = []
    for sub in range(cckv_ref.shape[2] // SAMPLE_SUB_KEYS):
        keys = slice(sub * SAMPLE_SUB_KEYS, (sub + 1) * SAMPLE_SUB_KEYS)
        ckv = cckv_ref[0, 0, keys, :]
        ckv_b = ckv.astype(BF16)
        vt = ckv.T.astype(BF16)
        kr = ckr_ref[0, 0, :, keys].T.astype(BF16)
        units += [(functools.partial(cache_scores, ckv_b, kr, c), vt, c) for c in range(nc)]
    _pipelined_updates(units, m_ref, l_ref, acc_ref)

    @pl.when(j == pl.num_programs(1) - 1)
    def _():
        kn = kn_ref[0]
        new_scores = lambda c: jnp.dot(kn, q_ref[0, 0, :, c * COLS:(c + 1) * COLS],
                                       preferred_element_type=F32)
        _pipelined_chunks(new_scores, vtn_ref[0], nc, m_ref, l_ref, acc_ref)
        _attn_finish(l_ref, acc_ref, wuv_ref, o_ref, tq)


def _sample_attention(qt, kcat, vt, cache_ckv, cache_krope_t, layer, wuv, tk):
    b, tq, _ = kcat.shape
    past = cache_ckv.shape[2]
    return pl.pallas_call(
        functools.partial(_sample_attn_kernel, tq=tq),
        grid=(b, past // tk),
        in_specs=[
            pl.BlockSpec((1, 1, QK_WIDTH, A_HEADS * tq), lambda i, j: (i, 0, 0, 0)),
            pl.BlockSpec((1, tq, QK_WIDTH), lambda i, j: (i, 0, 0)),
            pl.BlockSpec((1, KV_LORA, tq), lambda i, j: (i, 0, 0)),
            pl.BlockSpec((1, 1, tk, KV_LORA), lambda i, j: (layer, i, j, 0)),
            pl.BlockSpec((1, 1, ROPE_DIM, tk), lambda i, j: (layer, i, 0, j)),
            _layer_spec(wuv, layer),
        ],
        out_specs=pl.BlockSpec((1, tq, A_HEADS * V_DIM), lambda i, j: (i, 0, 0)),
        out_shape=jax.ShapeDtypeStruct((b, tq, A_HEADS * V_DIM), BF16),
        scratch_shapes=_attn_scratch(tq),
        compiler_params=_cparams(("parallel", "arbitrary")),
        name="sample_attn",
    )(qt, kcat, vt, cache_ckv, cache_krope_t, wuv)


def _rwkv_features(rw, rw_prev, mu_ref, w0_ref, w2_ref, a0_ref, a2_ref, g2_ref, kkw_ref, ka_ref, rk_ref,
                   bd_ref):
    mix = rw + (rw_prev - rw) * mu_ref[...]
    r = mix[:, 0:R_WIDTH]
    k = mix[:, R_WIDTH:2 * R_WIDTH]
    v = mix[:, 2 * R_WIDTH:3 * R_WIDTH]
    o = 3 * R_WIDTH
    w_lo = mix[:, o:o + DECAY_LORA]
    a_lo = mix[:, o + DECAY_LORA:o + DECAY_LORA + AAA_LORA]
    g_lo = mix[:, o + DECAY_LORA + AAA_LORA:RW_COLS]
    w_raw = w0_ref[...] + _bdot(jnp.tanh(w_lo), w2_ref[...])
    nw = -w_raw
    softplus = jnp.maximum(nw, 0.0) + jnp.log(1.0 + jnp.exp(-jnp.abs(nw)))
    log_decay = -jnp.exp(-softplus - 0.5)
    a = _sigmoid(a0_ref[...] + _bdot(a_lo, a2_ref[...]))
    g = _bdot(_sigmoid(g_lo), g2_ref[...])
    bd_half = bd_ref[0:TILE, 0:TILE]

    def head_sums(x):
        return jnp.concatenate([_split_dot(x[:, i * TILE:(i + 1) * TILE], bd_half)
                                for i in range(R_WIDTH // TILE)], axis=-1)

    kk = k * kkw_ref[...]
    kk = kk / jnp.maximum(jnp.sqrt(head_sums(kk * kk)), 1e-12)
    k2 = k * (1.0 + (a - 1.0) * ka_ref[...])
    bonus = head_sums(r * k2 * rk_ref[...]) * v
    return r, log_decay, k2, v, -kk, kk * a, g, bonus


SCAN_CHUNK = R_HEAD_DIM
TILE = 4 * R_HEAD_DIM
SUB_BLOCK = 16
BATCH_UNROLL = 8


def _contract_last(a, b):
    return lax.dot_general(a, b, (((1,), (1,)), ((), ())), preferred_element_type=F32)


def _rwkv_scan_kernel(r_ref, lw_ref, k_ref, v_ref, al_ref, be_ref, g_ref, bonus_ref, s0_ref,
                      lg_ref, lb_ref, bd_ref,
                      o_ref, sout_ref,
                      p_ref, at_ref, bh_ref, u_ref, z_ref, y_ref, aab_ref, arb_ref, bk_ref, ec_ref,
                      coef_ref, *us_refs, nb, tt):
    j = pl.program_id(1)
    n = R_HEAD_DIM
    c = SCAN_CHUNK
    nt = R_WIDTH // TILE
    group = BATCH_UNROLL if nb % BATCH_UNROLL == 0 else 1
    bd = bd_ref[...]
    bd_half = bd[0:TILE, 0:TILE]
    block_mask = (lax.broadcasted_iota(jnp.int32, (TILE, TILE), 0) // n
                  == lax.broadcasted_iota(jnp.int32, (TILE, TILE), 1) // n)
    t_id = lax.broadcasted_iota(jnp.int32, (c, TILE), 0)
    j_id = lax.broadcasted_iota(jnp.int32, (c, TILE), 1) % c
    strict = j_id < t_id
    incl = j_id <= t_id
    tri = (lax.broadcasted_iota(jnp.int32, (c, c), 1)
           <= lax.broadcasted_iota(jnp.int32, (c, c), 0)).astype(BF16)

    @pl.when(j == 0)
    def _():
        for b in range(nb):
            for i in range(nt):
                rows = []
                for hh in range(TILE // n):
                    parts = []
                    if hh:
                        parts.append(jnp.zeros((n, hh * n), F32))
                    parts.append(s0_ref[b, i * (TILE // n) + hh])
                    if TILE - (hh + 1) * n:
                        parts.append(jnp.zeros((n, TILE - (hh + 1) * n), F32))
                    rows.append(jnp.concatenate(parts, axis=-1))
                p_ref[b, i] = jnp.concatenate(rows, axis=0)

    def block_diag(x):
        return jnp.where(block_mask, jnp.concatenate([x] * (TILE // c), axis=0), 0.0).astype(BF16)

    def head_sums(x):
        xb = x.astype(BF16)
        return jnp.concatenate(
            [jnp.dot(xb[:, i * TILE:(i + 1) * TILE], bd_half, preferred_element_type=F32)
             for i in range(nt)], axis=-1)

    for ci in range(tt // c):
        rows = pl.ds(ci * c, c)

        def prepare(g, carry):
            seqs = [g * group + s for s in range(group)]
            lws = [lw_ref[b, rows, :] for b in seqs]
            lincs = []
            for lw in lws:
                hi = lw.astype(BF16)
                rem = lw - hi.astype(F32)
                mid = rem.astype(BF16)
                lo = (rem - mid.astype(F32)).astype(BF16)
                lincs.append(jnp.dot(tri, hi, preferred_element_type=F32)
                             + jnp.dot(tri, mid, preferred_element_type=F32)
                             + jnp.dot(tri, lo, preferred_element_type=F32))
            scaled = []
            for b, lw, linc in zip(seqs, lws, lincs):
                lc = linc[c - 1:c]
                e_neg = jnp.exp(-linc)
                e_rem = jnp.exp(lc - linc)
                al, be, r, k, v = (ref[b, rows, :] for ref in (al_ref, be_ref, r_ref, k_ref, v_ref))
                at = al * jnp.exp(linc - lw)
                rt = r * jnp.exp(linc)
                bh = be * e_neg
                kh = k * e_neg
                at_ref[b] = at
                bh_ref[b] = bh
                ec_ref[b] = jnp.exp(lc)
                bk_ref[b] = jnp.concatenate([be * e_rem, k * e_rem], axis=0).astype(BF16)
                scaled.append((at, rt, bh, kh, v))
            products = []
            for b, (at, rt, bh, kh, v) in zip(seqs, scaled):
                for i in range(nt):
                    ls = slice(i * TILE, (i + 1) * TILE)
                    x = jnp.concatenate([at[:, ls], rt[:, ls]], axis=0).astype(BF16)
                    a_k = _contract_last(x, block_diag(kh[:, ls]))
                    a_b = _contract_last(x, block_diag(bh[:, ls]))
                    x_p = _contract_last(x, p_ref[b, i].astype(BF16))
                    products.append((b, i, ls, a_k, a_b, x_p, block_diag(v[:, ls])))
            for b, i, ls, a_k, a_b, x_p, w_v in products:
                u_ref[b, :, ls] = x_p[0:c] + jnp.dot(
                    jnp.where(strict, a_k[0:c], 0.0).astype(BF16), w_v, preferred_element_type=F32)
                y_ref[b, rows, ls] = x_p[c:2 * c] + jnp.dot(
                    jnp.where(incl, a_k[c:2 * c], 0.0).astype(BF16), w_v, preferred_element_type=F32)
                aab_ref[b, i] = jnp.where(strict, a_b[0:c], 0.0).astype(BF16)
                arb_ref[b, i] = jnp.where(incl, a_b[c:2 * c], 0.0).astype(BF16)
            return carry

        lax.fori_loop(0, nb // group, prepare, 0)
        z_ref[...] = jnp.zeros(z_ref.shape, F32)

        for sb in range(c // SUB_BLOCK):
            srows = pl.ds(sb * SUB_BLOCK, SUB_BLOCK)

            sq = SUB_BLOCK * SUB_BLOCK
            lhs = [(bh_ref[b, srows, :][:, None, :] * at_ref[b, srows, :][None, :, :]).reshape(sq, R_WIDTH)
                   for b in range(nb)]
            coef = head_sums(jnp.concatenate(lhs, axis=0))
            for b in range(nb):
                coef_ref[:, b * SUB_BLOCK:(b + 1) * SUB_BLOCK, :] = (
                    coef[b * sq:(b + 1) * sq].reshape(SUB_BLOCK, SUB_BLOCK, R_WIDTH))
            if sb:
                for b in range(nb):
                    z = z_ref[b]
                    for i in range(nt):
                        ls = slice(i * TILE, (i + 1) * TILE)
                        u_ref[b, srows, ls] += jnp.dot(aab_ref[b, i, srows, :], block_diag(z[:, ls]),
                                                       preferred_element_type=F32)

            for b in range(nb):
                us_refs[b][...] = u_ref[b, srows, :]

            def substitute(j, carry):
                coef = coef_ref[j]
                for b in range(nb):
                    z_j = us_refs[b][pl.ds(j, 1), :]
                    z_ref[b, pl.ds(sb * SUB_BLOCK + j, 1), :] = z_j
                    us_refs[b][...] += coef[b * SUB_BLOCK:(b + 1) * SUB_BLOCK] * z_j
                return carry

            lax.fori_loop(0, SUB_BLOCK - 1, substitute, 0)
            for b in range(nb):
                last = (sb + 1) * SUB_BLOCK - 1
                z_ref[b, last:last + 1, :] = us_refs[b][SUB_BLOCK - 1:SUB_BLOCK, :]

        tiles = [(b, i, slice(i * TILE, (i + 1) * TILE)) for b in range(nb) for i in range(nt)]
        for b, i, ls in tiles:
            y_ref[b, rows, ls] += jnp.dot(arb_ref[b, i], block_diag(z_ref[b, :, ls]),
                                          preferred_element_type=F32)
        for b, i, ls in tiles:
            zv_t = jnp.concatenate([z_ref[b, :, ls], v_ref[b, rows, ls]], axis=0).T.astype(BF16)
            upd = jnp.dot(zv_t, bk_ref[b, :, ls], preferred_element_type=F32)
            p_ref[b, i] = p_ref[b, i] * ec_ref[b, :, ls] + jnp.where(block_mask, upd, 0.0)

    def head_means(x):
        return jnp.concatenate(
            [_split_dot(x[:, i * TILE:(i + 1) * TILE], bd_half) for i in range(nt)], axis=-1) * (1.0 / n)

    y = y_ref[...].reshape(nb * tt, R_WIDTH)
    d = y - head_means(y)
    yn = d * lax.rsqrt(head_means(d * d) + LNX_EPS) * lg_ref[...] + lb_ref[...]
    out = (yn + bonus_ref[...].reshape(nb * tt, R_WIDTH)) * g_ref[...].reshape(nb * tt, R_WIDTH)
    o_ref[...] = out.reshape(nb, tt, R_WIDTH).astype(o_ref.dtype)

    @pl.when(j == pl.num_programs(1) - 1)
    def _():
        for b in range(nb):
            for i in range(nt):
                p = p_ref[b, i]
                for hh in range(TILE // n):
                    sout_ref[b, i * (TILE // n) + hh] = p[hh * n:(hh + 1) * n, hh * n:(hh + 1) * n]


def _rwkv_scan(r, lw, k, v, al, be, g, bonus, s0, lg, lb, bd, nb, tt):
    b, t, _ = r.shape
    c = SCAN_CHUNK
    row = pl.BlockSpec((nb, tt, R_WIDTH), lambda i, j: (i, j, 0))
    st = pl.BlockSpec((nb, R_HEADS, R_HEAD_DIM, R_HEAD_DIM), lambda i, j: (i, 0, 0, 0))
    nt = R_WIDTH // TILE
    return pl.pallas_call(
        functools.partial(_rwkv_scan_kernel, nb=nb, tt=tt),
        grid=(b // nb, t // tt),
        in_specs=[row] * 8 + [st, _const_spec((1, R_WIDTH)), _const_spec((1, R_WIDTH)),
                              _const_spec((R_WIDTH, R_WIDTH))],
        out_specs=[row, st],
        out_shape=[jax.ShapeDtypeStruct((b, t, R_WIDTH), BF16),
                   jax.ShapeDtypeStruct((b, R_HEADS, R_HEAD_DIM, R_HEAD_DIM), F32)],
        scratch_shapes=[pltpu.VMEM((nb, nt, TILE, TILE), F32),
                        pltpu.VMEM((nb, c, R_WIDTH), F32),
                        pltpu.VMEM((nb, c, R_WIDTH), F32),
                        pltpu.VMEM((nb, c, R_WIDTH), F32),
                        pltpu.VMEM((nb, c, R_WIDTH), F32),
                        pltpu.VMEM((nb, tt, R_WIDTH), F32),
                        pltpu.VMEM((nb, nt, c, TILE), BF16),
                        pltpu.VMEM((nb, nt, c, TILE), BF16),
                        pltpu.VMEM((nb, 2 * c, R_WIDTH), BF16),
                        pltpu.VMEM((nb, 1, R_WIDTH), F32),
                        pltpu.VMEM((SUB_BLOCK, nb * SUB_BLOCK, R_WIDTH), F32)]
                       + [pltpu.VMEM((SUB_BLOCK, R_WIDTH), F32)] * nb,
        compiler_params=_cparams(("parallel", "arbitrary")),
        name="rwkv_scan",
    )(r, lw, k, v, al, be, g, bonus, s0, lg, lb, bd)


def _out_ffn_kernel(x_ref, oa_ref, orw_ref, mod_ref, ng_ref, wout_ref, wg_ref, wu_ref, wd_ref, o_ref):
    gb, tr, _ = x_ref.shape
    rows = gb * tr
    x = x_ref[...]
    mod = mod_ref[...]
    gt_a = mod[:, :, 2 * D_MODEL:3 * D_MODEL]
    sh_f = mod[:, :, 3 * D_MODEL:4 * D_MODEL]
    sc_f = mod[:, :, 4 * D_MODEL:5 * D_MODEL]
    gt_f = mod[:, :, 5 * D_MODEL:6 * D_MODEL]
    half = A_HEADS * V_DIM
    m = (jnp.dot(oa_ref[...].reshape(rows, half), wout_ref[0:half, :], preferred_element_type=F32)
         + jnp.dot(orw_ref[...].reshape(rows, R_WIDTH), wout_ref[half:D_MIX, :],
                   preferred_element_type=F32))
    x = x + gt_a * _rms(m.reshape(gb, tr, D_MODEL), ng_ref[1:2, :])
    h = (_rms(x, ng_ref[2:3, :]) * (1.0 + sc_f) + sh_f).reshape(rows, D_MODEL).astype(BF16)

    def gate_up(c):
        cols = slice(c * FFN_CHUNK, (c + 1) * FFN_CHUNK)
        return (jnp.dot(h, wg_ref[:, cols], preferred_element_type=F32),
                jnp.dot(h, wu_ref[:, cols], preferred_element_type=F32))

    f = jnp.zeros((rows, D_MODEL), F32)
    n_chunks = D_FF // FFN_CHUNK
    pending = [gate_up(c) for c in range(min(FFN_LOOKAHEAD, n_chunks))]
    for c in range(n_chunks):
        if c + FFN_LOOKAHEAD < n_chunks:
            pending.append(gate_up(c + FFN_LOOKAHEAD))
        gate, up = pending.pop(0)
        act = (gate * _sigmoid(gate) * up).astype(BF16)
        f = f + jnp.dot(act, wd_ref[c * FFN_CHUNK:(c + 1) * FFN_CHUNK, :], preferred_element_type=F32)
    o_ref[...] = x + gt_f * _rms(f.reshape(gb, tr, D_MODEL), ng_ref[3:4, :])


def _out_ffn(x, o_attn, o_rw, mod, ng, wout, wg, wu, wd, layer, gb, tr):
    b, t, _ = x.shape
    row = lambda w: pl.BlockSpec((gb, tr, w), lambda i, j: (i, j, 0))
    wspec = lambda w: _layer_spec(w, layer, pipeline_mode=pl.Buffered(1))
    return pl.pallas_call(
        _out_ffn_kernel,
        grid=(b // gb, t // tr),
        in_specs=[
            row(D_MODEL), row(A_HEADS * V_DIM), row(R_WIDTH),
            pl.BlockSpec((gb, 1, 6 * D_MODEL), lambda i, j: (i, 0, 0)),
            _const_spec((4, D_MODEL)),
            wspec(wout), wspec(wg), wspec(wu), wspec(wd),
        ],
        out_specs=row(D_MODEL),
        out_shape=jax.ShapeDtypeStruct((b, t, D_MODEL), F32),
        compiler_params=_cparams(("parallel", "parallel")),
        name="out_ffn",
    )(x, o_attn, o_rw, mod, ng, wout, wg, wu, wd)


def _rope_tables(pos):
    inv = ROPE_THETA ** (-jnp.arange(0, ROPE_DIM, 2, dtype=F32) / ROPE_DIM)
    ang = pos.astype(F32)[:, None] * inv[None, :]
    cos, sin = jnp.cos(ang), jnp.sin(ang)
    cc = jnp.concatenate([cos, cos], axis=-1)
    ss = jnp.concatenate([-sin, sin], axis=-1)
    return cc, ss, jnp.tile(cc, (1, A_HEADS)).T, jnp.tile(ss, (1, A_HEADS)).T


def _pack_weights(w_in, w_uq, w_uk, w_uv):
    depth = w_in.shape[0]
    half = ROPE_DIM // 2
    o = Q_LORA + KV_LORA
    k_r = w_in[:, :, o:o + ROPE_DIM]
    k_rs = jnp.concatenate([k_r[..., half:], k_r[..., :half]], axis=-1)
    pad = jnp.zeros((depth, D_MODEL, LANE - 2 * ROPE_DIM), w_in.dtype)
    win = jnp.concatenate([w_in[:, :, :o], k_r, k_rs, pad, w_in[:, :, MLA_COLS:]], axis=-1)
    uq = w_uq.reshape(depth, Q_LORA, A_HEADS, NOPE_DIM + ROPE_DIM)
    nope = uq[..., :NOPE_DIM].reshape(depth, Q_LORA, A_HEADS * NOPE_DIM)
    rope = uq[..., NOPE_DIM:]
    rope_s = jnp.concatenate([rope[..., half:], rope[..., :half]], axis=-1)
    wuq = jnp.concatenate([nope, rope.reshape(depth, Q_LORA, -1), rope_s.reshape(depth, Q_LORA, -1)], axis=-1)
    wuq = jnp.transpose(wuq, (0, 2, 1))
    wuk = jnp.transpose(w_uk, (0, 2, 1, 3))
    wkx = jnp.pad(w_uk, ((0, 0), (0, 0), (0, 0), (0, LANE - NOPE_DIM))).reshape(depth, KV_LORA, A_HEADS * LANE)
    wuv = jnp.transpose(w_uv, (0, 2, 3, 1))
    return win.astype(BF16), wuq.astype(BF16), wuk.astype(BF16), wkx.astype(BF16), wuv.astype(BF16)


def _block_diag_ones():
    hid = jnp.arange(R_WIDTH) // R_HEAD_DIM
    bd = (hid[:, None] == hid[None, :]).astype(BF16)
    return bd


def _row_tile(t, pref):
    return pref if t % pref == 0 else t


def kernel(x_prompt, x_sample, c_prompt, c_sample, cache_ckv, cache_krope, state_wkv, state_shift,
           ada_w, ada_b, norm_g, w_in, q_norm_g, kv_norm_g, w_uq, w_uk, w_uv,
           rw_mu, rw_w0, rw_w2, rw_a0, rw_a2, rw_g2, rw_kk, rw_ka, rw_rk, lnx_g, lnx_b,
           w_out, w_gate, w_up, w_down):
    depth = w_in.shape[0]
    bp, tp, _ = x_prompt.shape
    bs, ts, _ = x_sample.shape
    past = cache_ckv.shape[2]

    mod_all = _modulation(jnp.concatenate([c_prompt, c_sample], axis=0), ada_w, ada_b)
    win, wuq, wuk, wkx, wuv = _pack_weights(w_in, w_uq, w_uk, w_uv)
    wout, wg, wu, wd = (w.astype(BF16) for w in (w_out, w_gate, w_up, w_down))
    w2, a2, g2 = (w.astype(BF16) for w in (rw_w2, rw_a2, rw_g2))
    bd = _block_diag_ones()
    tabs_p = _rope_tables(jnp.arange(tp))
    tabs_s = _rope_tables(past + jnp.arange(ts))
    cache_krope_t = jnp.swapaxes(cache_krope, 2, 3)
    vec = lambda a, l: a[l].reshape(1, -1)

    def layer(l, x, mod, tabs, s0, shift0, cache):
        b, t, _ = x.shape
        tr = _row_tile(t, IN_ROWS)
        gb = max(1, min(b, IN_ROWS // t))
        gb = gb if b % gb == 0 else 1
        tabs = tuple(jnp.tile(tb, (gb, 1) if tb.shape[0] == t else (1, gb)) for tb in tabs)
        tq = _row_tile(t, COLS)
        rw_params = (vec(rw_mu, l), vec(rw_w0, l), w2[l], vec(rw_a0, l), a2[l], g2[l],
                     vec(rw_kk, l), vec(rw_ka, l), rw_rk[l].reshape(1, -1), bd)
        ckv, krope, kmat, vt, qt, r, w, k, v, al, be, g, bonus, shift_new = _in_proj(
            x, mod, vec(norm_g[:, 0], l), win, vec(q_norm_g, l), vec(kv_norm_g, l),
            wuq, wuk, wkx, *tabs, shift0, rw_params, l, gb, tr, tq, cache is None)
        if cache is None:
            o_attn = _prompt_attention(qt, kmat, vt, wuv, l, tq)
        else:
            o_attn = _sample_attention(qt, kmat, vt, cache[0], cache[1], l, wuv, min(4096, past))
        ts_ = SCAN_CHUNK
        o_rw, s_new = _rwkv_scan(r, w, k, v, al, be, g, bonus, s0, vec(lnx_g, l), vec(lnx_b, l),
                                 bd, 8 if b % 8 == 0 else b, ts_)
        gb = max(1, min(b, FFN_ROWS // t))
        x_new = _out_ffn(x, o_attn, o_rw, mod, norm_g[l], wout, wg, wu, wd, l,
                         gb if b % gb == 0 else 1, min(t, FFN_ROWS))
        return x_new, ckv, krope, s_new, shift_new[:, 0]

    xp, xs = x_prompt, x_sample
    wkv_zero = jnp.zeros((bp, R_HEADS, R_HEAD_DIM, R_HEAD_DIM), F32)
    shift_zero = jnp.zeros((bp, 1, RW_COLS), F32)
    outs_p, outs_s = [], []
    for l in range(depth):
        mod_p = mod_all[l, :bp].reshape(bp, 1, 6 * D_MODEL)
        mod_s = mod_all[l, bp:].reshape(bs, 1, 6 * D_MODEL)
        xp, *rest_p = layer(l, xp, mod_p, tabs_p, wkv_zero, shift_zero, None)
        xs, *rest_s = layer(l, xs, mod_s, tabs_s, state_wkv[l],
                            state_shift[l].reshape(bs, 1, RW_COLS), (cache_ckv, cache_krope_t))
        outs_p.append(rest_p)
        outs_s.append(rest_s)
    stack = lambda outs, i: jnp.stack([o[i] for o in outs])
    return (xp, xs,
            stack(outs_p, 0), stack(outs_p, 1), stack(outs_p, 2), stack(outs_p, 3),
            stack(outs_s, 0), stack(outs_s, 1), stack(outs_s, 2), stack(outs_s, 3))
```
